```python
import jax, jax.numpy as jnp
from jax import lax
import numpy as np

D_MODEL = 1024
BATCH = 2
SEQ = 16384
DEPTH = 1
DEC_BATCH = 8
DEC_SEQ = 32
PAST_LEN = 4096

CHUNK = 64
Q_BLOCK = 128
ROPE_THETA = 500000.0
EPS = 1e-6
N_HEADS_A = 8
NOPE_DIM = 64
ROPE_DIM = 32
V_DIM_A = 64
Q_LORA = 384
KV_LORA = 256
N_HEADS_B = 8
N_KV_B = 2
HEAD_DIM_B = 64
ROT_DIM_B = HEAD_DIM_B // 4
N_IDX_HEADS = 8
IDX_DIM = 64
ROT_DIM_IDX = IDX_DIM // 4
TOPK_MAX = 256
D_FF = 2816
MIX_WIDTH = N_HEADS_A * V_DIM_A + N_HEADS_B * HEAD_DIM_B
IN_SIZES = (Q_LORA, KV_LORA, ROPE_DIM,
            N_HEADS_B * HEAD_DIM_B, N_KV_B * HEAD_DIM_B, N_KV_B * HEAD_DIM_B,
            N_IDX_HEADS * IDX_DIM, IDX_DIM, N_IDX_HEADS)
IN_COLS = sum(IN_SIZES)

kernel_name = "hybrid_mla_dsa_streaming_step"


def rms_norm(x, g):
    xf = x.astype(jnp.float32)
    y = xf * lax.rsqrt(jnp.mean(xf * xf, axis=-1, keepdims=True) + EPS)
    return (y * g.astype(jnp.float32)).astype(x.dtype)


def rope(x, pos):
    d = x.shape[-1]
    inv = jnp.power(ROPE_THETA, -jnp.arange(0, d, 2, dtype=jnp.float32) / d)
    ang = pos.astype(jnp.float32)[:, None] * inv[None, :]
    ang = ang.reshape((1, pos.shape[0]) + (1,) * (x.ndim - 3) + (d // 2,))
    cos, sin = jnp.cos(ang), jnp.sin(ang)
    x1, x2 = jnp.split(x.astype(jnp.float32), 2, axis=-1)
    return jnp.concatenate([x1 * cos - x2 * sin, x1 * sin + x2 * cos], axis=-1).astype(x.dtype)


def partial_rope(x, pos, rot):
    return jnp.concatenate([rope(x[..., :rot], pos), x[..., rot:]], axis=-1)


def _qblock(t):
    return Q_BLOCK if t % Q_BLOCK == 0 else t


def _to_blocks(x, qb):
    b, t = x.shape[:2]
    return x.reshape((b, t // qb, qb) + x.shape[2:]).swapaxes(0, 1)


def _from_blocks(x):
    n, b, qb = x.shape[:3]
    return x.swapaxes(0, 1).reshape((b, n * qb) + x.shape[3:])


def mla_attention(q_nope, q_rope, k_nope, k_rope, v, q_pos, k_pos):
    qb = _qblock(q_nope.shape[1])
    scale = (NOPE_DIM + ROPE_DIM) ** -0.5
    k_chunk = k_pos // CHUNK

    def one_block(args):
        qn, qr, qp = args
        s = (jnp.einsum('bqhd,bshd->bhqs', qn, k_nope)
             + jnp.einsum('bqhd,bsd->bhqs', qr, k_rope)).astype(jnp.float32) * scale
        mask = (qp[:, None] // CHUNK) >= k_chunk[None, :]
        s = jnp.where(mask[None, None], s, -jnp.inf)
        p = jax.nn.softmax(s, axis=-1).astype(v.dtype)
        return jnp.einsum('bhqs,bshd->bqhd', p, v)

    out = lax.map(one_block, (_to_blocks(q_nope, qb), _to_blocks(q_rope, qb), q_pos.reshape(-1, qb)))
    return _from_blocks(out)


def indexed_sparse_attention(q, k, v, iq, iw, ik, q_pos, k_pos):
    b, t, h, dh = q.shape
    s_len = k.shape[1]
    topk = min(TOPK_MAX, s_len // 4)
    qb = _qblock(t)
    k_chunk = k_pos // CHUNK
    w_scale = (N_IDX_HEADS * IDX_DIM) ** -0.5

    def one_block(args):
        qblk, iqb, iwb, qp = args
        rel = jax.nn.relu(jnp.einsum('bqhd,bsd->bqhs', iqb, ik).astype(jnp.float32))
        score = jnp.einsum('bqh,bqhs->bqs', iwb.astype(jnp.float32) * w_scale, rel)
        adm = (qp[:, None] // CHUNK) >= k_chunk[None, :]
        score = jnp.where(adm[None], score, -jnp.inf)
        top_val, top_idx = lax.top_k(score, topk)
        valid = jnp.isfinite(top_val)
        kg = jax.vmap(lambda kk, ii: kk[ii])(k, top_idx)
        vg = jax.vmap(lambda vv, ii: vv[ii])(v, top_idx)
        qg = qblk.reshape(b, qb, N_KV_B, h // N_KV_B, dh)
        s = jnp.einsum('bqjgd,bqnjd->bqjgn', qg, kg).astype(jnp.float32) * (dh ** -0.5)
        s = jnp.where(valid[:, :, None, None, :], s, -jnp.inf)
        p = jax.nn.softmax(s, axis=-1).astype(v.dtype)
        o = jnp.einsum('bqjgn,bqnjd->bqjgd', p, vg)
        return o.reshape(b, qb, h, dh)

    out = lax.map(one_block, (_to_blocks(q, qb), _to_blocks(iq, qb), _to_blocks(iw, qb), q_pos.reshape(-1, qb)))
    return _from_blocks(out)


def layer(x, pos, past, attn_norm, w_in, q_a_norm, w_q_b, kv_a_norm, w_kv_b, w_o,
          ffn_norm, w_gate, w_up, w_down):
    b, l, _ = x.shape
    h = rms_norm(x, attn_norm)
    splits = np.cumsum(IN_SIZES)[:-1].tolist()
    q_lat, kv_lat, kr_raw, qb_raw, kb_raw, vb_raw, iq_raw, ik_raw, iw = jnp.split(h @ w_in, splits, axis=-1)
    qa = (rms_norm(q_lat, q_a_norm) @ w_q_b).reshape(b, l, N_HEADS_A, NOPE_DIM + ROPE_DIM)
    qa_nope, qa_rope = qa[..., :NOPE_DIM], rope(qa[..., NOPE_DIM:], pos)
    c_kv = rms_norm(kv_lat, kv_a_norm)
    k_rope = rope(kr_raw, pos)
    qB = partial_rope(qb_raw.reshape(b, l, N_HEADS_B, HEAD_DIM_B), pos, ROT_DIM_B)
    kB = partial_rope(kb_raw.reshape(b, l, N_KV_B, HEAD_DIM_B), pos, ROT_DIM_B)
    vB = vb_raw.reshape(b, l, N_KV_B, HEAD_DIM_B)
    iq = partial_rope(iq_raw.reshape(b, l, N_IDX_HEADS, IDX_DIM), pos, ROT_DIM_IDX)
    ik = partial_rope(ik_raw, pos, ROT_DIM_IDX)
    new = (c_kv, k_rope, kB, vB, ik)
    if past is None:
        c_all, kr_all, kB_all, vB_all, ik_all = new
    else:
        c_all, kr_all, kB_all, vB_all, ik_all = [jnp.concatenate([pc, nc], axis=1) for pc, nc in zip(past, new)]
    s_len = c_all.shape[1]
    k_pos = jnp.arange(s_len)
    kv = (c_all @ w_kv_b).reshape(b, s_len, N_HEADS_A, NOPE_DIM + V_DIM_A)
    o_a = mla_attention(qa_nope, qa_rope, kv[..., :NOPE_DIM], kr_all, kv[..., NOPE_DIM:], pos, k_pos)
    o_b = indexed_sparse_attention(qB, kB_all, vB_all, iq, iw, ik_all, pos, k_pos)
    o = jnp.concatenate([o_a.reshape(b, l, -1), o_b.reshape(b, l, -1)], axis=-1) @ w_o
    x = x + o
    h2 = rms_norm(x, ffn_norm)
    x = x + (jax.nn.silu(h2 @ w_gate) * (h2 @ w_up)) @ w_down
    return x, new


def setup_inputs(seed: int = 0) -> dict:
    key = jax.random.key(seed)
    ks = jax.random.split(key, 20)
    nrm = lambda k, shape, s=1.0: jax.random.normal(k, shape, jnp.float32) * s
    gain = lambda k, n: 1.0 + 0.02 * jax.random.normal(k, (DEPTH, n), jnp.float32)
    return {
        "x_prompt": nrm(ks[0], (BATCH, SEQ, D_MODEL)),
        "x_sample": nrm(ks[1], (DEC_BATCH, DEC_SEQ, D_MODEL)),
        "cache_mla_latent": nrm(ks[2], (DEPTH, DEC_BATCH, PAST_LEN, KV_LORA)),
        "cache_mla_krope": nrm(ks[3], (DEPTH, DEC_BATCH, PAST_LEN, ROPE_DIM)),
        "cache_dsa_k": nrm(ks[4], (DEPTH, DEC_BATCH, PAST_LEN, N_KV_B, HEAD_DIM_B)),
        "cache_dsa_v": nrm(ks[5], (DEPTH, DEC_BATCH, PAST_LEN, N_KV_B, HEAD_DIM_B)),
        "cache_idx_k": nrm(ks[6], (DEPTH, DEC_BATCH, PAST_LEN, IDX_DIM)),
        "attn_norm": gain(ks[7], D_MODEL),
        "w_in": nrm(ks[8], (DEPTH, D_MODEL, IN_COLS), D_MODEL ** -0.5),
        "q_a_norm": gain(ks[9], Q_LORA),
        "w_q_b": nrm(ks[10], (DEPTH, Q_LORA, N_HEADS_A * (NOPE_DIM + ROPE_DIM)), Q_LORA ** -0.5),
        "kv_a_norm": gain(ks[11], KV_LORA),
        "w_kv_b": nrm(ks[12], (DEPTH, KV_LORA, N_HEADS_A * (NOPE_DIM + V_DIM_A)), KV_LORA ** -0.5),
        "w_o": nrm(ks[13], (DEPTH, MIX_WIDTH, D_MODEL), MIX_WIDTH ** -0.5),
        "ffn_norm": gain(ks[14], D_MODEL),
        "w_gate": nrm(ks[15], (DEPTH, D_MODEL, D_FF), D_MODEL ** -0.5),
        "w_up": nrm(ks[16], (DEPTH, D_MODEL, D_FF), D_MODEL ** -0.5),
        "w_down": nrm(ks[17], (DEPTH, D_FF, D_MODEL), D_FF ** -0.5),
        "final_norm": 1.0 + 0.02 * jax.random.normal(ks[18], (D_MODEL,), jnp.float32),
    }


def reference(x_prompt, x_sample, cache_mla_latent, cache_mla_krope, cache_dsa_k, cache_dsa_v, cache_idx_k,
              attn_norm, w_in, q_a_norm, w_q_b, kv_a_norm, w_kv_b, w_o, ffn_norm, w_gate, w_up, w_down,
              final_norm):
    past_len = cache_mla_latent.shape[2]
    pos_p = jnp.arange(x_prompt.shape[1])
    pos_s = past_len + jnp.arange(x_sample.shape[1])
    yp, ys = x_prompt, x_sample
    p_new = [[] for _ in range(5)]
    s_new = [[] for _ in range(5)]
    for li in range(DEPTH):
        params = (attn_norm[li], w_in[li], q_a_norm[li], w_q_b[li], kv_a_norm[li], w_kv_b[li], w_o[li],
                  ffn_norm[li], w_gate[li], w_up[li], w_down[li])
        past = (cache_mla_latent[li], cache_mla_krope[li], cache_dsa_k[li], cache_dsa_v[li], cache_idx_k[li])
        yp, np_ = layer(yp, pos_p, None, *params)
        ys, ns_ = layer(ys, pos_s, past, *params)
        for j in range(5):
            p_new[j].append(np_[j])
            s_new[j].append(ns_[j])
    y_prompt = rms_norm(yp, final_norm)
    y_sample = rms_norm(ys, final_norm)
    p_lat, p_kr, p_k, p_v, p_ik = [jnp.stack(a, axis=0) for a in p_new]
    s_lat, s_kr, s_k, s_v, s_ik = [jnp.stack(a, axis=0) for a in s_new]
    return (y_prompt, y_sample, p_lat, p_kr, p_k, p_v, p_ik, s_lat, s_kr, s_k, s_v, s_ik)
```

```python
import functools

import numpy as np
import jax
import jax.numpy as jnp
from jax import lax
from jax.experimental import pallas as pl
from jax.experimental.pallas import tpu as pltpu

D_MODEL = 1024
CHUNK = 64
ROPE_THETA = 500000.0
EPS = 1e-6
N_HEADS_A = 8
NOPE_DIM = 64
ROPE_DIM = 32
V_DIM_A = 64
Q_LORA = 384
KV_LORA = 256
N_HEADS_B = 8
N_KV_B = 2
HEAD_DIM_B = 64
ROT_DIM_B = 16
N_IDX_HEADS = 8
IDX_DIM = 64
ROT_DIM_IDX = 16
TOPK_MAX = 256
D_FF = 2816
IN_SIZES = (Q_LORA, KV_LORA, ROPE_DIM, N_HEADS_B * HEAD_DIM_B, N_KV_B * HEAD_DIM_B, N_KV_B * HEAD_DIM_B,
            N_IDX_HEADS * IDX_DIM, IDX_DIM, N_IDX_HEADS)

LANES = 128
MXU_DTYPE = jnp.bfloat16
VMEM_LIMIT = 56 * 1024 * 1024
NEG_BIG = -1e30
LOWEST = -3.0e38

_C_QLAT = 0
_C_KVLAT = _C_QLAT + Q_LORA
_C_QB = _C_KVLAT + KV_LORA
_C_IQ = _C_QB + N_HEADS_B * LANES
_C_KB = _C_IQ + N_IDX_HEADS * LANES
_C_VB = _C_KB + LANES
_C_IK = _C_VB + LANES
_C_KR = _C_IK + LANES
_C_IW = _C_KR + LANES
_C_END = _C_IW + LANES


def _cparams(n_axes):
    return pltpu.CompilerParams(dimension_semantics=("arbitrary",) * n_axes, vmem_limit_bytes=VMEM_LIMIT)


def _const_spec(shape):
    return pl.BlockSpec(shape, lambda *_: (0,) * len(shape), pipeline_mode=pl.Buffered(1))


def _rms(x, g):
    return (x * lax.rsqrt(jnp.mean(x * x, axis=-1, keepdims=True) + EPS)) * g


def _rope_tile(x, cos, sin, half, first_half):
    partner = jnp.where(first_half, pltpu.roll(x, LANES - half, 1), pltpu.roll(x, half, 1))
    return x * cos + partner * sin


def _proj_kernel(x_ref, cq_ref, sq_ref, c64_ref, s64_ref, gat_ref, win_ref, gq_ref, wqb_ref, gkv_ref,
                 ckv_ref, kr_ref, kb_ref, vb_ref, ik_ref, iw_ref, qm_ref, qb_ref, iq_ref,
                 kb16_ref, vb16_ref, ik16_ref):
    x = x_ref[...]
    hb = _rms(x, gat_ref[...]).astype(MXU_DTYPE)
    y = jnp.dot(hb, win_ref[...], preferred_element_type=jnp.float32)

    lane = lax.broadcasted_iota(jnp.int32, (1, LANES), 1)
    first_q = lane < NOPE_DIM + ROPE_DIM // 2
    first_64 = (lane & (HEAD_DIM_B - 1)) < ROT_DIM_B // 2
    cq, sq, c64, s64 = cq_ref[...], sq_ref[...], c64_ref[...], s64_ref[...]

    qn = _rms(y[:, _C_QLAT:_C_QLAT + Q_LORA], gq_ref[...]).astype(MXU_DTYPE)
    qa = jnp.dot(qn, wqb_ref[...], preferred_element_type=jnp.float32)
    scale_a = (NOPE_DIM + ROPE_DIM) ** -0.5
    for h in range(N_HEADS_A):
        t = _rope_tile(qa[:, h * LANES:(h + 1) * LANES], cq, sq, ROPE_DIM // 2, first_q)
        qm_ref[:, h * LANES:(h + 1) * LANES] = (t * scale_a).astype(qm_ref.dtype)

    ckv_ref[...] = _rms(y[:, _C_KVLAT:_C_KVLAT + KV_LORA], gkv_ref[...])
    kr_ref[...] = _rope_tile(y[:, _C_KR:_C_KR + LANES], cq, sq, ROPE_DIM // 2, first_q)

    scale_b = HEAD_DIM_B ** -0.5
    for h in range(N_HEADS_B):
        t = _rope_tile(y[:, _C_QB + h * LANES:_C_QB + (h + 1) * LANES], c64, s64, ROT_DIM_B // 2, first_64)
        qb_ref[:, h * LANES:(h + 1) * LANES] = (t * scale_b).astype(qb_ref.dtype)
    for h in range(N_IDX_HEADS):
        t = _rope_tile(y[:, _C_IQ + h * LANES:_C_IQ + (h + 1) * LANES], c64, s64, ROT_DIM_IDX // 2, first_64)
        iq_ref[:, h * LANES:(h + 1) * LANES] = t.astype(iq_ref.dtype)

    kb = _rope_tile(y[:, _C_KB:_C_KB + LANES], c64, s64, ROT_DIM_B // 2, first_64)
    kb_ref[...] = kb
    kb16_ref[...] = kb.astype(kb16_ref.dtype)
    vb = y[:, _C_VB:_C_VB + LANES]
    vb_ref[...] = vb
    vb16_ref[...] = vb.astype(vb16_ref.dtype)
    ik = _rope_tile(y[:, _C_IK:_C_IK + LANES], c64, s64, ROT_DIM_IDX // 2, first_64)
    ik_ref[...] = ik
    ik16_ref[...] = ik.astype(ik16_ref.dtype)
    iw_ref[...] = y[:, _C_IW:_C_IW + LANES] * ((N_IDX_HEADS * IDX_DIM) ** -0.5)


def _proj_call(x2d, tabs, n_tab_tiles, tm, g_attn, w_in_p, g_q, w_qb_p, g_kv):
    t_tok = x2d.shape[0]
    grid = (t_tok // tm,)
    row = lambda w: pl.BlockSpec((tm, w), lambda i: (i, 0))
    tab = pl.BlockSpec((tm, LANES), lambda i: (i % n_tab_tiles, 0))
    f32, b16 = jnp.float32, MXU_DTYPE
    out_shapes = [
        jax.ShapeDtypeStruct((t_tok, KV_LORA), f32),
        jax.ShapeDtypeStruct((t_tok, LANES), f32),
        jax.ShapeDtypeStruct((t_tok, LANES), f32),
        jax.ShapeDtypeStruct((t_tok, LANES), f32),
        jax.ShapeDtypeStruct((t_tok, LANES), f32),
        jax.ShapeDtypeStruct((t_tok, LANES), f32),
        jax.ShapeDtypeStruct((t_tok, N_HEADS_A * LANES), b16),
        jax.ShapeDtypeStruct((t_tok, N_HEADS_B * LANES), b16),
        jax.ShapeDtypeStruct((t_tok, N_IDX_HEADS * LANES), b16),
        jax.ShapeDtypeStruct((t_tok, LANES), b16),
        jax.ShapeDtypeStruct((t_tok, LANES), b16),
        jax.ShapeDtypeStruct((t_tok, LANES), b16),
    ]
    out_specs = [row(s.shape[1]) for s in out_shapes]
    in_specs = [row(D_MODEL), tab, tab, tab, tab, _const_spec((1, D_MODEL)), _const_spec(w_in_p.shape),
                _const_spec((1, Q_LORA)), _const_spec(w_qb_p.shape), _const_spec((1, KV_LORA))]
    return pl.pallas_call(
        _proj_kernel, grid=grid, in_specs=in_specs, out_specs=out_specs, out_shape=out_shapes,
        compiler_params=_cparams(1), name="proj",
    )(x2d, *tabs, g_attn, w_in_p, g_q, w_qb_p, g_kv)


def _kvup_kernel(c_ref, kr_ref, wk_ref, wv_ref, k_ref, v_ref):
    cb = c_ref[...].astype(MXU_DTYPE)
    kn = jnp.dot(cb, wk_ref[...], preferred_element_type=jnp.float32)
    kr = kr_ref[...]
    for h in range(N_HEADS_A):
        k_ref[:, h * LANES:(h + 1) * LANES] = (kn[:, h * LANES:(h + 1) * LANES] + kr).astype(k_ref.dtype)
    v_ref[...] = jnp.dot(cb, wv_ref[...], preferred_element_type=jnp.float32).astype(v_ref.dtype)


def _kvup_call(c2d, kr2d, wk, wv, tm):
    t_tok = c2d.shape[0]
    row = lambda w: pl.BlockSpec((tm, w), lambda i: (i, 0))
    return pl.pallas_call(
        _kvup_kernel, grid=(t_tok // tm,),
        in_specs=[row(KV_LORA), row(LANES), _const_spec(wk.shape), _const_spec(wv.shape)],
        out_specs=[row(N_HEADS_A * LANES), row(N_HEADS_A * V_DIM_A)],
        out_shape=[jax.ShapeDtypeStruct((t_tok, N_HEADS_A * LANES), MXU_DTYPE),
                   jax.ShapeDtypeStruct((t_tok, N_HEADS_A * V_DIM_A), MXU_DTYPE)],
        compiler_params=_cparams(1), name="kvup",
    )(c2d, kr2d, wk, wv)


_CHUNK_SHIFT = CHUNK.bit_length() - 1


def _admissible(p0, kb, tq, tk, s_len):
    qpos = p0 + lax.broadcasted_iota(jnp.int32, (tq, 1), 0)
    kpos = kb * tk + lax.broadcasted_iota(jnp.int32, (1, tk), 1)
    return ((qpos >> _CHUNK_SHIFT) >= (kpos >> _CHUNK_SHIFT)) & (kpos < s_len)


def _last_block(p0, tq, tk):
    return (p0 + tq - 1) >> (tk.bit_length() - 1)


def _softmax_step(carry, s, v):
    m, l, acc = carry
    m_new = jnp.maximum(m, jnp.max(s, axis=-1, keepdims=True))
    alpha = jnp.exp(m - m_new)
    p = jnp.exp(s - m_new)
    l = alpha * l + jnp.sum(p, axis=-1, keepdims=True)
    acc = alpha * acc + jnp.dot(p.astype(v.dtype), v, preferred_element_type=jnp.float32)
    return m_new, l, acc


def _softmax_init(tq):
    return (jnp.full((tq, 1), NEG_BIG, jnp.float32), jnp.zeros((tq, 1), jnp.float32),
            jnp.zeros((tq, LANES), jnp.float32))


_NT = (((1,), (1,)), ((), ()))


def _mla_kernel(q_ref, k_ref, v_ref, o_ref, *, tq, tk, q_off, s_len):
    p0 = q_off + pl.program_id(2) * tq
    last = _last_block(p0, tq, tk)
    outs = []
    for hh in range(2):
        q = q_ref[:, hh * LANES:(hh + 1) * LANES]

        def step(kb, carry, masked, hh=hh, q=q):
            start = pl.multiple_of(kb * tk, tk)
            k = k_ref[pl.ds(start, tk), hh * LANES:(hh + 1) * LANES]
            v = v_ref[pl.ds(start, tk), :]
            s = lax.dot_general(q, k, _NT, preferred_element_type=jnp.float32)
            if masked:
                s = jnp.where(_admissible(p0, kb, tq, tk, s_len), s, NEG_BIG)
            return _softmax_step(carry, s, v)

        carry = lax.fori_loop(0, last, functools.partial(step, masked=False), _softmax_init(tq))
        _, l, acc = step(last, carry, True)
        outs.append(acc / l)
    lane = lax.broadcasted_iota(jnp.int32, (1, LANES), 1)
    o_ref[...] = jnp.where(lane < V_DIM_A, outs[0], outs[1]).astype(o_ref.dtype)


def _mla_call(q, k, v, *, tq, tk, q_off, s_len):
    b, lq, _ = q.shape
    s_pad = k.shape[1]
    kern = functools.partial(_mla_kernel, tq=tq, tk=tk, q_off=q_off, s_len=s_len)
    return pl.pallas_call(
        kern, grid=(b, N_HEADS_A // 2, lq // tq),
        in_specs=[pl.BlockSpec((None, tq, 2 * LANES), lambda bi, j, qi: (bi, qi, j)),
                  pl.BlockSpec((None, s_pad, 2 * LANES), lambda bi, j, qi: (bi, 0, j)),
                  pl.BlockSpec((None, s_pad, LANES), lambda bi, j, qi: (bi, 0, j))],
        out_specs=pl.BlockSpec((None, tq, LANES), lambda bi, j, qi: (bi, qi, j)),
        out_shape=jax.ShapeDtypeStruct((b, lq, N_HEADS_A * V_DIM_A), MXU_DTYPE),
        compiler_params=_cparams(3), name="mla",
    )(q, k, v)


def _f2key(x):
    b = lax.bitcast_convert_type(x, jnp.int32)
    return jnp.where(b < 0, b ^ jnp.int32(0x7FFFFFFF), b)


def _key2f(k):
    return lax.bitcast_convert_type(jnp.where(k < 0, k ^ jnp.int32(0x7FFFFFFF), k), jnp.float32)


def _dsa_kernel(iq_ref, qb_ref, iw_ref, ik_ref, kb_ref, vb_ref, o_ref, sc_ref, st_ref, wb_ref, *, tq, tk, q_off, s_len, topk):
    p0 = q_off + pl.program_id(1) * tq
    last = _last_block(p0, tq, tk)
    n_blk = last + 1

    iw = iw_ref[...]
    for h in range(N_IDX_HEADS):
        wb_ref[h] = jnp.broadcast_to(iw[:, h:h + 1], (tq, LANES))

    def score_block(kb):
        start = pl.multiple_of(kb * tk, tk)
        ikb = ik_ref[pl.ds(start, tk), :]
        cols = [jnp.zeros((tq, LANES), jnp.float32)] * (tk // LANES)
        for h in range(N_IDX_HEADS):
            x = lax.dot_general(iq_ref[:, h * LANES:(h + 1) * LANES], ikb, _NT, preferred_element_type=jnp.float32)
            w = wb_ref[h]
            cols = [a + w * jnp.maximum(x[:, c * LANES:(c + 1) * LANES], 0.0) for c, a in enumerate(cols)]
        return jnp.concatenate(cols, axis=1)

    def p1_body(kb, carry):
        mx, sm = carry
        acc = score_block(kb)
        sc_ref[kb] = acc
        return jnp.maximum(mx, jnp.max(acc, axis=-1, keepdims=True)), sm + jnp.sum(acc, axis=-1, keepdims=True)

    mx, sm = lax.fori_loop(0, last, p1_body,
                           (jnp.full((tq, 1), -jnp.inf, jnp.float32), jnp.zeros((tq, 1), jnp.float32)))
    acc = score_block(last)
    adm = _admissible(p0, last, tq, tk, s_len)
    sc_ref[last] = jnp.where(adm, acc, -jnp.inf)
    mx = jnp.maximum(mx, jnp.max(jnp.where(adm, acc, -jnp.inf), axis=-1, keepdims=True))
    sm = sm + jnp.sum(jnp.where(adm, acc, 0.0), axis=-1, keepdims=True)

    qpos = p0 + lax.broadcasted_iota(jnp.int32, (tq, 1), 0)
    n_adm = jnp.minimum(((qpos >> _CHUNK_SHIFT) + 1) * CHUNK, s_len).astype(jnp.float32)
    k_f = jnp.float32(topk)
    st_ref[0] = jnp.full((tq, 1), LOWEST, jnp.float32)
    st_ref[1] = _key2f(_f2key(jnp.maximum(mx, LOWEST)) + 1)
    st_ref[2] = (n_adm <= k_f).astype(jnp.float32)
    mean = sm / n_adm

    def count_ge(g):
        g128 = jnp.broadcast_to(g, (tq, LANES))

        def body(kb, cnt):
            blk = sc_ref[kb]
            for c in range(tk // LANES):
                cnt = cnt + jnp.where(blk[:, c * LANES:(c + 1) * LANES] >= g128, 1.0, 0.0)
            return cnt

        cnt = lax.fori_loop(0, n_blk, body, jnp.zeros((tq, LANES), jnp.float32))
        return jnp.sum(cnt, axis=-1, keepdims=True)

    def search_cond(carry):
        it, pending = carry
        return jnp.logical_and(pending > 0, it < 80)

    def search_body(carry):
        it, _ = carry
        lo, hi, done = st_ref[0], st_ref[1], st_ref[2]
        lok, hik = _f2key(lo), _f2key(hi)
        midk = (lok >> 1) + (hik >> 1) + (lok & hik & 1)
        g = jnp.where(it == 0, jnp.clip(mean, lo, hi), _key2f(midk))
        gk = _f2key(g)
        gk = jnp.minimum(jnp.maximum(gk, lok + 1), jnp.maximum(hik - 1, lok + 1))
        g = _key2f(gk)
        c = count_ge(g)
        live = done == 0.0
        ge = c >= k_f
        lo = jnp.where(live & ge, g, lo)
        hi = jnp.where(live & (~ge), g, hi)
        fin = (live & ge & (c == k_f)) | (_f2key(lo) + 1 >= _f2key(hi))
        done = jnp.where(fin, 1.0, done)
        st_ref[0], st_ref[1], st_ref[2] = lo, hi, done
        pending = jnp.sum(1.0 - done)
        return it + 1, pending.astype(jnp.int32)

    pending0 = jnp.sum(1.0 - st_ref[2]).astype(jnp.int32)
    lax.while_loop(search_cond, search_body, (jnp.int32(0), pending0))
    tau = st_ref[0]

    outs = []
    for h in range(N_HEADS_B):
        q = qb_ref[:, h * LANES:(h + 1) * LANES]

        def step(kb, carry, q=q):
            start = pl.multiple_of(kb * tk, tk)
            s = lax.dot_general(q, kb_ref[pl.ds(start, tk), :], _NT, preferred_element_type=jnp.float32)
            s = jnp.where(sc_ref[kb] >= tau, s, NEG_BIG)
            return _softmax_step(carry, s, vb_ref[pl.ds(start, tk), :])

        _, l, acc = lax.fori_loop(0, n_blk, step, _softmax_init(tq))
        outs.append(acc / l)
    lane = lax.broadcasted_iota(jnp.int32, (1, LANES), 1)
    half = N_HEADS_B // N_KV_B
    for t in range(half):
        o_ref[:, t * LANES:(t + 1) * LANES] = jnp.where(lane < HEAD_DIM_B, outs[t], outs[t + half]).astype(o_ref.dtype)


def _dsa_call(iq, qb, iw, ik, kb, vb, *, tq, tk, q_off, s_len):
    b, lq, _ = iq.shape
    s_pad = ik.shape[1]
    topk = min(TOPK_MAX, s_len // 4)
    kern = functools.partial(_dsa_kernel, tq=tq, tk=tk, q_off=q_off, s_len=s_len, topk=topk)
    qspec = lambda w: pl.BlockSpec((None, tq, w), lambda bi, qi: (bi, qi, 0))
    kspec = pl.BlockSpec((None, s_pad, LANES), lambda bi, qi: (bi, 0, 0))
    return pl.pallas_call(
        kern, grid=(b, lq // tq),
        in_specs=[qspec(N_IDX_HEADS * LANES), qspec(N_HEADS_B * LANES), qspec(LANES), kspec, kspec, kspec],
        out_specs=qspec(N_HEADS_B * HEAD_DIM_B),
        out_shape=jax.ShapeDtypeStruct((b, lq, N_HEADS_B * HEAD_DIM_B), MXU_DTYPE),
        scratch_shapes=[pltpu.VMEM((s_pad // tk, tq, tk), jnp.float32), pltpu.VMEM((3, tq, 1), jnp.float32),
                        pltpu.VMEM((N_IDX_HEADS, tq, LANES), jnp.float32)],
        compiler_params=_cparams(2), name="dsa",
    )(iq, qb, iw, ik, kb, vb)


_FF_CHUNK = 256


def _out_kernel(x_ref, oa_ref, ob_ref, woa_ref, wob_ref, gf_ref, wg_ref, wu_ref, wd_ref, gfin_ref, y_ref):
    o = (jnp.dot(oa_ref[...], woa_ref[...], preferred_element_type=jnp.float32)
         + jnp.dot(ob_ref[...], wob_ref[...], preferred_element_type=jnp.float32))
    x1 = x_ref[...] + o
    h2 = _rms(x1, gf_ref[...]).astype(MXU_DTYPE)
    ffn = jnp.zeros_like(x1)
    for j in range(D_FF // _FF_CHUNK):
        sl = slice(j * _FF_CHUNK, (j + 1) * _FF_CHUNK)
        gate = jnp.dot(h2, wg_ref[:, sl], preferred_element_type=jnp.float32)
        up = jnp.dot(h2, wu_ref[:, sl], preferred_element_type=jnp.float32)
        act = (gate * jax.nn.sigmoid(gate)) * up
        ffn = ffn + jnp.dot(act.astype(MXU_DTYPE), wd_ref[sl, :], preferred_element_type=jnp.float32)
    y_ref[...] = _rms(x1 + ffn, gfin_ref[...])


def _out_call(x2d, oa, ob, woa, wob, g_ffn, wg, wu, wd, g_fin, tm):
    t_tok = x2d.shape[0]
    row = lambda w: pl.BlockSpec((tm, w), lambda i: (i, 0))
    half = N_HEADS_A * V_DIM_A
    return pl.pallas_call(
        _out_kernel, grid=(t_tok // tm,),
        in_specs=[row(D_MODEL), row(half), row(half), _const_spec(woa.shape), _const_spec(wob.shape),
                  _const_spec((1, D_MODEL)), _const_spec(wg.shape), _const_spec(wu.shape), _const_spec(wd.shape),
                  _const_spec((1, D_MODEL))],
        out_specs=row(D_MODEL), out_shape=jax.ShapeDtypeStruct((t_tok, D_MODEL), jnp.float32),
        compiler_params=_cparams(1), name="out",
    )(x2d, oa, ob, woa, wob, g_ffn, wg, wu, wd, g_fin)


def _prep_weights(w_in, w_q_b, w_kv_b, w_o, w_gate, w_up, w_down):
    off = np.concatenate([[0], np.cumsum(IN_SIZES)])
    o_qlat, o_kvlat, o_kr, o_qb, o_kb, o_vb, o_iq, o_ik, o_iw = off[:9]
    zeros = lambda n: jnp.zeros((D_MODEL, n), w_in.dtype)
    cols = [w_in[:, o_qlat:o_qlat + Q_LORA], w_in[:, o_kvlat:o_kvlat + KV_LORA]]
    for h in range(N_HEADS_B):
        head = w_in[:, o_qb + h * HEAD_DIM_B:o_qb + (h + 1) * HEAD_DIM_B]
        cols += [head, zeros(HEAD_DIM_B)] if h < N_HEADS_B // N_KV_B else [zeros(HEAD_DIM_B), head]
    for h in range(N_IDX_HEADS):
        cols += [w_in[:, o_iq + h * IDX_DIM:o_iq + (h + 1) * IDX_DIM], zeros(LANES - IDX_DIM)]
    cols += [w_in[:, o_kb:o_kb + LANES], w_in[:, o_vb:o_vb + LANES],
             w_in[:, o_ik:o_ik + IDX_DIM], zeros(LANES - IDX_DIM),
             zeros(NOPE_DIM), w_in[:, o_kr:o_kr + ROPE_DIM], zeros(LANES - NOPE_DIM - ROPE_DIM),
             w_in[:, o_iw:o_iw + N_IDX_HEADS], zeros(LANES - N_IDX_HEADS)]
    w_in_p = jnp.concatenate(cols, axis=1).astype(MXU_DTYPE)
    assert w_in_p.shape[1] == _C_END

    qd = NOPE_DIM + ROPE_DIM
    w_qb_p = jnp.pad(w_q_b.reshape(Q_LORA, N_HEADS_A, qd), ((0, 0), (0, 0), (0, LANES - qd)))
    w_qb_p = w_qb_p.reshape(Q_LORA, N_HEADS_A * LANES).astype(MXU_DTYPE)

    kv = w_kv_b.reshape(KV_LORA, N_HEADS_A, NOPE_DIM + V_DIM_A)
    wk = jnp.pad(kv[:, :, :NOPE_DIM], ((0, 0), (0, 0), (0, LANES - NOPE_DIM)))
    wk = wk.reshape(KV_LORA, N_HEADS_A * LANES).astype(MXU_DTYPE)
    wv = kv[:, :, NOPE_DIM:].reshape(KV_LORA, N_HEADS_A * V_DIM_A).astype(MXU_DTYPE)

    half = N_HEADS_A * V_DIM_A
    woa = w_o[:half].astype(MXU_DTYPE)
    g = N_HEADS_B // N_KV_B
    wob = w_o[half:].reshape(N_KV_B, g, HEAD_DIM_B, D_MODEL).transpose(1, 0, 2, 3).reshape(half, D_MODEL)
    return (w_in_p, w_qb_p, wk, wv, woa, wob.astype(MXU_DTYPE),
            w_gate.astype(MXU_DTYPE), w_up.astype(MXU_DTYPE), w_down.astype(MXU_DTYPE))


def _rope_tables(pos):
    posf = pos.astype(jnp.float32)[:, None]
    n = pos.shape[0]

    def cs(d):
        inv = jnp.power(ROPE_THETA, -jnp.arange(0, d, 2, dtype=jnp.float32) / d)
        ang = posf * inv[None, :]
        return jnp.cos(ang), jnp.sin(ang)

    one, zero = (lambda w: jnp.ones((n, w), jnp.float32)), (lambda w: jnp.zeros((n, w), jnp.float32))
    c, s = cs(ROPE_DIM)
    cq = jnp.concatenate([one(NOPE_DIM), c, c, one(LANES - NOPE_DIM - ROPE_DIM)], axis=1)
    sq = jnp.concatenate([zero(NOPE_DIM), -s, s, zero(LANES - NOPE_DIM - ROPE_DIM)], axis=1)
    c, s = cs(ROT_DIM_B)
    c64 = jnp.tile(jnp.concatenate([c, c, one(HEAD_DIM_B - ROT_DIM_B)], axis=1), (1, LANES // HEAD_DIM_B))
    s64 = jnp.tile(jnp.concatenate([-s, s, zero(HEAD_DIM_B - ROT_DIM_B)], axis=1), (1, LANES // HEAD_DIM_B))
    return cq, sq, c64, s64


def _pick_tile(n, pref):
    t = min(pref, n)
    while n % t:
        t //= 2
    return t


def _layer(x, pos0, past, weights, norms, final_norm):
    b, l, _ = x.shape
    g_attn, g_q, g_kv, g_ffn = norms
    w_in_p, w_qb_p, wk, wv, woa, wob, wg, wu, wd = weights
    t_tok = b * l
    x2d = x.reshape(t_tok, D_MODEL)

    tabs = _rope_tables(pos0 + jnp.arange(l))
    if l >= 256:
        tm = _pick_tile(l, 512)
        n_tab_tiles = l // tm
    else:
        tm = _pick_tile(t_tok, 512)
        reps = tm // l
        tabs = tuple(jnp.tile(t, (reps, 1)) for t in tabs)
        n_tab_tiles = 1
    (ckv, kr128, kb, vb, ik128, iw, q_mla, q_b, iq, kb16, vb16, ik16) = _proj_call(
        x2d, tabs, n_tab_tiles, tm, g_attn, w_in_p, g_q, w_qb_p, g_kv)

    new = (ckv.reshape(b, l, KV_LORA), kr128.reshape(b, l, LANES)[..., NOPE_DIM:NOPE_DIM + ROPE_DIM],
           kb.reshape(b, l, N_KV_B, HEAD_DIM_B), vb.reshape(b, l, N_KV_B, HEAD_DIM_B),
           ik128.reshape(b, l, LANES)[..., :IDX_DIM])

    tq = _pick_tile(l, 256)
    tk = 256
    if past is None:
        s_len = l
        c_all, kr_all = ckv, kr128
        kb_all, vb_all, ik_all = (a.reshape(b, l, LANES) for a in (kb16, vb16, ik16))
        s_pad = l
    else:
        p_lat, p_kr, p_k, p_v, p_ik = past
        p_len = p_lat.shape[1]
        s_len = p_len + l
        s_pad = -(-s_len // tk) * tk
        padn = s_pad - s_len

        def cat(old, newpart):
            return jnp.concatenate([old, newpart, jnp.zeros((b, padn, old.shape[-1]), old.dtype)], axis=1)

        c_all = cat(p_lat, ckv.reshape(b, l, KV_LORA)).reshape(b * s_pad, KV_LORA)
        p_kr128 = jnp.pad(p_kr, ((0, 0), (0, 0), (NOPE_DIM, LANES - NOPE_DIM - ROPE_DIM)))
        kr_all = cat(p_kr128, kr128.reshape(b, l, LANES)).reshape(b * s_pad, LANES)
        kb_all = cat(p_k.reshape(b, p_len, LANES).astype(MXU_DTYPE), kb16.reshape(b, l, LANES))
        vb_all = cat(p_v.reshape(b, p_len, LANES).astype(MXU_DTYPE), vb16.reshape(b, l, LANES))
        p_ik128 = jnp.pad(p_ik, ((0, 0), (0, 0), (0, LANES - IDX_DIM))).astype(MXU_DTYPE)
        ik_all = cat(p_ik128, ik16.reshape(b, l, LANES))

    k_mla, v_mla = _kvup_call(c_all, kr_all, wk, wv, _pick_tile(b * s_pad, 512))
    k_mla = k_mla.reshape(b, s_pad, N_HEADS_A * LANES)
    v_mla = v_mla.reshape(b, s_pad, N_HEADS_A * V_DIM_A)

    o_a = _mla_call(q_mla.reshape(b, l, -1), k_mla, v_mla, tq=tq, tk=tk, q_off=pos0, s_len=s_len)
    o_b = _dsa_call(iq.reshape(b, l, -1), q_b.reshape(b, l, -1), iw.reshape(b, l, LANES), ik_all, kb_all, vb_all,
                    tq=tq, tk=tk, q_off=pos0, s_len=s_len)

    y = _out_call(x2d, o_a.reshape(t_tok, -1), o_b.reshape(t_tok, -1), woa, wob, g_ffn, wg, wu, wd, final_norm, tm)
    return y.reshape(b, l, D_MODEL), new


def kernel(x_prompt, x_sample, cache_mla_latent, cache_mla_krope, cache_dsa_k, cache_dsa_v, cache_idx_k, attn_norm, w_in, q_a_norm, w_q_b, kv_a_norm, w_kv_b, w_o, ffn_norm, w_gate, w_up, w_down, final_norm):
    depth = w_in.shape[0]
    assert depth == 1, "the fused final norm assumes a single layer"
    past_len = cache_mla_latent.shape[2]
    li = 0
    weights = _prep_weights(w_in[li], w_q_b[li], w_kv_b[li], w_o[li], w_gate[li], w_up[li], w_down[li])
    norms = (attn_norm[li][None, :], q_a_norm[li][None, :], kv_a_norm[li][None, :], ffn_norm[li][None, :])
    fin = final_norm[None, :]
    past = (cache_mla_latent[li], cache_mla_krope[li], cache_dsa_k[li], cache_dsa_v[li], cache_idx_k[li])
    y_p, new_p = _layer(x_prompt, 0, None, weights, norms, fin)
    y_s, new_s = _layer(x_sample, past_len, past, weights, norms, fin)
    return (y_p, y_s) + tuple(a[None] for a in new_p) + tuple(a[None] for a in new_s)
```

```python
import functools

import numpy as np
import jax
import jax.numpy as jnp
from jax import lax
from jax.experimental import pallas as pl
from jax.experimental.pallas import tpu as pltpu

D_MODEL = 1024
CHUNK = 64
ROPE_THETA = 500000.0
EPS = 1e-6
N_HEADS_A = 8
NOPE_DIM = 64
ROPE_DIM = 32
V_DIM_A = 64
Q_LORA = 384
KV_LORA = 256
N_HEADS_B = 8
N_KV_B = 2
HEAD_DIM_B = 64
ROT_DIM_B = 16
N_IDX_HEADS = 8
IDX_DIM = 64
ROT_DIM_IDX = 16
TOPK_MAX = 256
D_FF = 2816
IN_SIZES = (Q_LORA, KV_LORA, ROPE_DIM, N_HEADS_B * HEAD_DIM_B, N_KV_B * HEAD_DIM_B, N_KV_B * HEAD_DIM_B,
            N_IDX_HEADS * IDX_DIM, IDX_DIM, N_IDX_HEADS)

LANES = 128
SUBLANES = 8
MXU_DTYPE = jnp.bfloat16
VMEM_LIMIT = 56 * 1024 * 1024
NEG_BIG = -1e30
LOWEST = -3.0e38
FLT_MIN = 1.1754943508222875e-38
LOG2E = 1.4426950408889634

TM = 512
TQ = 256
TK_MLA = 512
TK_DSA = 256
MLA_HEADS_PER_STEP = 4

_C_QLAT = 0
_C_KVLAT = _C_QLAT + Q_LORA
_C_QB = _C_KVLAT + KV_LORA
_C_IQ = _C_QB + N_HEADS_B * LANES
_C_KB = _C_IQ + N_IDX_HEADS * LANES
_C_VB = _C_KB + LANES
_C_IK = _C_VB + LANES
_C_KR = _C_IK + LANES
_C_IW = _C_KR + LANES
_C_END = _C_IW + LANES


def _cparams(n_axes):
    return pltpu.CompilerParams(dimension_semantics=("arbitrary",) * n_axes, vmem_limit_bytes=VMEM_LIMIT)


def _const_spec(shape):
    return pl.BlockSpec(shape, lambda *_: (0,) * len(shape), pipeline_mode=pl.Buffered(1))


def _rms(x, g):
    return (x * lax.rsqrt(jnp.mean(x * x, axis=-1, keepdims=True) + EPS)) * g


def _rope_tile(x, cos, sin, half, first_half):
    partner = jnp.where(first_half, pltpu.roll(x, LANES - half, 1), pltpu.roll(x, half, 1))
    return x * cos + partner * sin


def _proj_kernel(x_ref, cq_ref, sq_ref, c64_ref, s64_ref, gat_ref, win_ref, gq_ref, wqb_ref, gkv_ref,
                 ckv_ref, kr_ref, kb_ref, vb_ref, ik_ref, iw_ref, qm_ref, qb_ref, iq_ref,
                 kb16_ref, vb16_ref, ik16_ref):
    x = x_ref[...]
    hb = _rms(x, gat_ref[...]).astype(MXU_DTYPE)
    y = jnp.dot(hb, win_ref[...], preferred_element_type=jnp.float32)

    lane = lax.broadcasted_iota(jnp.int32, (1, LANES), 1)
    first_q = lane < NOPE_DIM + ROPE_DIM // 2
    first_64 = (lane & (HEAD_DIM_B - 1)) < ROT_DIM_B // 2
    cq, sq, c64, s64 = cq_ref[...], sq_ref[...], c64_ref[...], s64_ref[...]

    qn = _rms(y[:, _C_QLAT:_C_QLAT + Q_LORA], gq_ref[...]).astype(MXU_DTYPE)
    qa = jnp.dot(qn, wqb_ref[...], preferred_element_type=jnp.float32)
    scale_a = (NOPE_DIM + ROPE_DIM) ** -0.5 * LOG2E
    for h in range(N_HEADS_A):
        t = _rope_tile(qa[:, h * LANES:(h + 1) * LANES], cq, sq, ROPE_DIM // 2, first_q)
        qm_ref[:, h * LANES:(h + 1) * LANES] = (t * scale_a).astype(qm_ref.dtype)

    ckv_ref[...] = _rms(y[:, _C_KVLAT:_C_KVLAT + KV_LORA], gkv_ref[...])
    kr_ref[...] = _rope_tile(y[:, _C_KR:_C_KR + LANES], cq, sq, ROPE_DIM // 2, first_q)

    scale_b = HEAD_DIM_B ** -0.5 * LOG2E
    for h in range(N_HEADS_B):
        t = _rope_tile(y[:, _C_QB + h * LANES:_C_QB + (h + 1) * LANES], c64, s64, ROT_DIM_B // 2, first_64)
        qb_ref[:, h * LANES:(h + 1) * LANES] = (t * scale_b).astype(qb_ref.dtype)
    for h in range(N_IDX_HEADS):
        t = _rope_tile(y[:, _C_IQ + h * LANES:_C_IQ + (h + 1) * LANES], c64, s64, ROT_DIM_IDX // 2, first_64)
        iq_ref[:, h * LANES:(h + 1) * LANES] = t.astype(iq_ref.dtype)

    kb = _rope_tile(y[:, _C_KB:_C_KB + LANES], c64, s64, ROT_DIM_B // 2, first_64)
    kb_ref[...] = kb
    kb16_ref[...] = kb.astype(kb16_ref.dtype)
    vb = y[:, _C_VB:_C_VB + LANES]
    vb_ref[...] = vb
    vb16_ref[...] = vb.astype(vb16_ref.dtype)
    ik = _rope_tile(y[:, _C_IK:_C_IK + LANES], c64, s64, ROT_DIM_IDX // 2, first_64)
    ik_ref[...] = ik
    ik16_ref[...] = ik.astype(ik16_ref.dtype)
    iw_ref[...] = y[:, _C_IW:_C_IW + LANES] * ((N_IDX_HEADS * IDX_DIM) ** -0.5)


def _proj_call(x2d, tabs, n_tab_tiles, tm, g_attn, w_in_p, g_q, w_qb_p, g_kv):
    t_tok = x2d.shape[0]
    grid = (t_tok // tm,)
    row = lambda w: pl.BlockSpec((tm, w), lambda i: (i, 0))
    tab = pl.BlockSpec((tm, LANES), lambda i: (i % n_tab_tiles, 0))
    f32, b16 = jnp.float32, MXU_DTYPE
    out_shapes = [
        jax.ShapeDtypeStruct((t_tok, KV_LORA), f32),
        jax.ShapeDtypeStruct((t_tok, LANES), f32),
        jax.ShapeDtypeStruct((t_tok, LANES), f32),
        jax.ShapeDtypeStruct((t_tok, LANES), f32),
        jax.ShapeDtypeStruct((t_tok, LANES), f32),
        jax.ShapeDtypeStruct((t_tok, LANES), f32),
        jax.ShapeDtypeStruct((t_tok, N_HEADS_A * LANES), b16),
        jax.ShapeDtypeStruct((t_tok, N_HEADS_B * LANES), b16),
        jax.ShapeDtypeStruct((t_tok, N_IDX_HEADS * LANES), b16),
        jax.ShapeDtypeStruct((t_tok, LANES), b16),
        jax.ShapeDtypeStruct((t_tok, LANES), b16),
        jax.ShapeDtypeStruct((t_tok, LANES), b16),
    ]
    out_specs = [row(s.shape[1]) for s in out_shapes]
    in_specs = [row(D_MODEL), tab, tab, tab, tab, _const_spec((1, D_MODEL)), _const_spec(w_in_p.shape),
                _const_spec((1, Q_LORA)), _const_spec(w_qb_p.shape), _const_spec((1, KV_LORA))]
    return pl.pallas_call(
        _proj_kernel, grid=grid, in_specs=in_specs, out_specs=out_specs, out_shape=out_shapes,
        compiler_params=_cparams(1), name="proj",
    )(x2d, *tabs, g_attn, w_in_p, g_q, w_qb_p, g_kv)


_NT = (((1,), (1,)), ((), ()))


def _kvup_kernel(c_ref, kr_ref, wk_ref, wvt_ref, k_ref, vt_ref):
    cb = c_ref[...].astype(MXU_DTYPE)
    kn = jnp.dot(cb, wk_ref[...], preferred_element_type=jnp.float32)
    kr = kr_ref[...]
    for h in range(N_HEADS_A):
        k_ref[:, h * LANES:(h + 1) * LANES] = (kn[:, h * LANES:(h + 1) * LANES] + kr).astype(k_ref.dtype)
    vt_ref[...] = lax.dot_general(wvt_ref[...], cb, _NT, preferred_element_type=jnp.float32).astype(vt_ref.dtype)


def _kvup_call(c3d, kr3d, wk, wvt, tk):
    b, s_pad, _ = c3d.shape
    row = lambda w: pl.BlockSpec((None, tk, w), lambda bi, i: (bi, i, 0))
    vdim = N_HEADS_A * V_DIM_A
    return pl.pallas_call(
        _kvup_kernel, grid=(b, s_pad // tk),
        in_specs=[row(KV_LORA), row(LANES), _const_spec(wk.shape), _const_spec(wvt.shape)],
        out_specs=[row(N_HEADS_A * LANES), pl.BlockSpec((None, None, vdim, tk), lambda bi, i: (bi, i, 0, 0))],
        out_shape=[jax.ShapeDtypeStruct((b, s_pad, N_HEADS_A * LANES), MXU_DTYPE),
                   jax.ShapeDtypeStruct((b, s_pad // tk, vdim, tk), MXU_DTYPE)],
        compiler_params=_cparams(2), name="kvup",
    )(c3d, kr3d, wk, wvt)


_CHUNK_SHIFT = CHUNK.bit_length() - 1


def _admissible_t(p0, kb, tq, tk, s_len):
    kpos = kb * tk + lax.broadcasted_iota(jnp.int32, (tk, 1), 0)
    qpos = p0 + lax.broadcasted_iota(jnp.int32, (1, tq), 1)
    return ((qpos >> _CHUNK_SHIFT) >= (kpos >> _CHUNK_SHIFT)) & (kpos < s_len)


def _last_block(p0, tq, tk, s_len):
    return jnp.minimum(p0 + tq - 1, s_len - 1) >> (tk.bit_length() - 1)


def _softmax_init(n_heads, tq, acc_ref):
    acc_ref[...] = jnp.zeros(acc_ref.shape, jnp.float32)
    return jnp.full((n_heads, tq), NEG_BIG, jnp.float32), jnp.zeros((n_heads, tq), jnp.float32)


def _softmax_block(carry, scores, values, acc_ref):
    m_prev, l_prev = carry
    m_new = jnp.maximum(m_prev, jnp.concatenate([jnp.max(s, axis=0, keepdims=True) for s in scores], axis=0))
    alpha = jnp.exp2(m_prev - m_new)
    probs = [jnp.exp2(s - m_new[h:h + 1, :]) for h, s in enumerate(scores)]
    l_new = alpha * l_prev + jnp.concatenate([jnp.sum(p, axis=0, keepdims=True) for p in probs], axis=0)
    pv = [jnp.dot(v, p.astype(v.dtype), preferred_element_type=jnp.float32) for v, p in zip(values, probs)]
    for h, x in enumerate(pv):
        acc_ref[h] = alpha[h:h + 1, :] * acc_ref[h] + x
    return m_new, l_new


def _softmax_finish(l, acc_ref, o_ref):
    inv = 1.0 / l
    o_t = jnp.concatenate([acc_ref[h] * inv[h:h + 1, :] for h in range(l.shape[0])], axis=0)
    o_ref[...] = o_t.T.astype(o_ref.dtype)


def _mla_kernel(q_ref, k_ref, vt_ref, o_ref, acc_ref, *, tq, tk, q_off, s_len):
    p0 = q_off + pl.program_id(2) * tq
    last = _last_block(p0, tq, tk, s_len)
    heads = range(MLA_HEADS_PER_STEP)

    def block(kb, carry, masked):
        start = pl.multiple_of(kb * tk, tk)
        vt_all = vt_ref[kb]
        scores = [lax.dot_general(k_ref[pl.ds(start, tk), h * LANES:(h + 1) * LANES],
                                  q_ref[:, h * LANES:(h + 1) * LANES], _NT, preferred_element_type=jnp.float32)
                  for h in heads]
        if masked:
            bias = jnp.where(_admissible_t(p0, kb, tq, tk, s_len), 0.0, NEG_BIG)
            scores = [s + bias for s in scores]
        values = [vt_all[h * V_DIM_A:(h + 1) * V_DIM_A, :] for h in heads]
        return _softmax_block(carry, scores, values, acc_ref)

    carry = lax.fori_loop(0, last, functools.partial(block, masked=False),
                          _softmax_init(MLA_HEADS_PER_STEP, tq, acc_ref))
    _, l = block(last, carry, True)
    _softmax_finish(l, acc_ref, o_ref)


def _mla_call(q, k, vt, *, tq, tk, q_off, s_len):
    b, lq, _ = q.shape
    s_pad = k.shape[1]
    hg = MLA_HEADS_PER_STEP
    kern = functools.partial(_mla_kernel, tq=tq, tk=tk, q_off=q_off, s_len=s_len)
    once = dict(pipeline_mode=pl.Buffered(1))
    return pl.pallas_call(
        kern, grid=(b, N_HEADS_A // hg, lq // tq),
        in_specs=[pl.BlockSpec((None, tq, hg * LANES), lambda bi, g, qi: (bi, qi, g)),
                  pl.BlockSpec((None, s_pad, hg * LANES), lambda bi, g, qi: (bi, 0, g), **once),
                  pl.BlockSpec((None, s_pad // tk, hg * V_DIM_A, tk), lambda bi, g, qi: (bi, 0, g, 0), **once)],
        out_specs=pl.BlockSpec((None, tq, hg * V_DIM_A), lambda bi, g, qi: (bi, qi, g)),
        out_shape=jax.ShapeDtypeStruct((b, lq, N_HEADS_A * V_DIM_A), MXU_DTYPE),
        scratch_shapes=[pltpu.VMEM((hg, V_DIM_A, tq), jnp.float32)],
        compiler_params=_cparams(3), name="mla",
    )(q, k, vt)


def _f2key(x):
    b = lax.bitcast_convert_type(x, jnp.int32)
    return jnp.where(b < 0, b ^ jnp.int32(0x7FFFFFFF), b)


def _key2f(k):
    return lax.bitcast_convert_type(jnp.where(k < 0, k ^ jnp.int32(0x7FFFFFFF), k), jnp.float32)


_SEARCH_BISECT_EVERY = 8
_SEARCH_MAX_PROBES = 40 * _SEARCH_BISECT_EVERY


def _dsa_kernel(iq_ref, qb_ref, iw_ref, ik_ref, kb_ref, vbt_ref, tri_ref, o_ref,
                sc_ref, wt_ref, acc_ref, *, tq, tk, q_off, s_len, topk):
    p0 = q_off + pl.program_id(1) * tq
    last = _last_block(p0, tq, tk, s_len)
    n_blk = last + 1
    row = lambda v: jnp.full((1, tq), v, jnp.float32)
    colsum = lambda x: jnp.sum(x, axis=0, keepdims=True)

    wt_ref[...] = iw_ref[...].T[:SUBLANES, :]

    def score_block(kb):
        start = pl.multiple_of(kb * tk, tk)
        ikb = ik_ref[pl.ds(start, tk), :]
        acc = jnp.zeros((tk, tq), jnp.float32)
        for h in range(N_IDX_HEADS):
            x = lax.dot_general(ikb, iq_ref[:, h * LANES:(h + 1) * LANES], _NT, preferred_element_type=jnp.float32)
            acc = acc + wt_ref[h:h + 1, :] * jnp.maximum(x, 0.0)
        return acc

    def stats(carry, hi_fill, lo_fill):
        mx, mn, n_pos, n_nonneg = carry
        return (jnp.maximum(mx, jnp.max(hi_fill, axis=0, keepdims=True)),
                jnp.minimum(mn, jnp.min(lo_fill, axis=0, keepdims=True)),
                n_pos + colsum(jnp.where(hi_fill >= FLT_MIN, 1.0, 0.0)),
                n_nonneg + colsum(jnp.where(hi_fill >= 0.0, 1.0, 0.0)))

    def p1_body(kb, carry):
        acc = score_block(kb)
        sc_ref[kb] = acc
        return stats(carry, acc, acc)

    carry = lax.fori_loop(0, last, p1_body, (row(-jnp.inf), row(jnp.inf), row(0.0), row(0.0)))
    acc = score_block(last)
    adm = _admissible_t(p0, last, tq, tk, s_len)
    masked = jnp.where(adm, acc, -jnp.inf)
    sc_ref[last] = masked
    mx, mn, n_pos, n_nonneg = stats(carry, masked, jnp.where(adm, acc, jnp.inf))

    qpos = p0 + lax.broadcasted_iota(jnp.int32, (1, tq), 1)
    n_adm = jnp.minimum(((qpos >> _CHUNK_SHIFT) + 1) * CHUNK, s_len).astype(jnp.float32)
    k_f = jnp.float32(topk)
    log_k = jnp.log(k_f)
    phi = lambda c: jnp.log(jnp.maximum(c, 0.25)) - log_k

    few = n_adm <= k_f
    pos = n_pos >= k_f
    at_zero = jnp.logical_and(~pos, n_nonneg >= k_f)
    above_max = _key2f(_f2key(jnp.maximum(mx, LOWEST)) + 1)
    lo = jnp.where(pos, FLT_MIN, LOWEST)
    c_lo = jnp.where(pos, n_pos, n_adm)
    hi = jnp.where(pos | at_zero, above_max, 0.0)
    c_hi = jnp.where(pos | at_zero, 0.0, n_nonneg)
    lo_val = jnp.where(pos, 0.0, jnp.maximum(mn, LOWEST))
    tau = jnp.where(few, LOWEST, jnp.where(at_zero, 0.0, lo))
    r = jnp.where(at_zero & ~few, k_f - n_pos, 0.0)
    done = jnp.where(few | at_zero, 1.0, 0.0)

    def count_ge(g):
        def body(kb, cnt):
            return cnt + colsum(jnp.where(sc_ref[kb] >= g, 1.0, 0.0))

        return lax.fori_loop(0, n_blk, body, row(0.0))

    def search_cond(st):
        return jnp.logical_and(st[1] > 0, st[0] < _SEARCH_MAX_PROBES)

    def search_body(st):
        it, _, lo, hi, c_lo, c_hi, f_lo, f_hi, lo_val, side, tau, r, done = st
        lok, hik = _f2key(lo), _f2key(hi)
        midk = (lok >> 1) + (hik >> 1) + (lok & hik & 1)
        frac = f_lo / jnp.maximum(f_lo - f_hi, 1e-9)
        g_int = lo_val + (hi - lo_val) * frac
        bisect = (it % _SEARCH_BISECT_EVERY) == (_SEARCH_BISECT_EVERY - 1)
        g = jnp.where(bisect, _key2f(midk), g_int)
        gk = jnp.minimum(jnp.maximum(_f2key(g), lok + 1), jnp.maximum(hik - 1, lok + 1))
        g = _key2f(gk)
        c = count_ge(g)
        f_g = phi(c)
        live = done == 0.0
        up = live & (c >= k_f)
        dn = live & (c < k_f)
        f_hi = jnp.where(up & (side > 0.0), f_hi * 0.5, f_hi)
        f_lo = jnp.where(dn & (side < 0.0), f_lo * 0.5, f_lo)
        lo, c_lo, f_lo, lo_val = (jnp.where(up, g, lo), jnp.where(up, c, c_lo), jnp.where(up, f_g, f_lo),
                                  jnp.where(up, g, lo_val))
        hi, c_hi, f_hi = jnp.where(dn, g, hi), jnp.where(dn, c, c_hi), jnp.where(dn, f_g, f_hi)
        side = jnp.where(up, 1.0, jnp.where(dn, -1.0, side))
        adjacent = (_f2key(lo) + 1) >= _f2key(hi)
        fin = live & ((c_lo == k_f) | adjacent)
        tau = jnp.where(fin, lo, tau)
        r = jnp.where(fin & (c_lo > k_f), k_f - c_hi, r)
        done = jnp.where(fin, 1.0, done)
        pending = jnp.sum(1.0 - done).astype(jnp.int32)
        return (it + 1, pending, lo, hi, c_lo, c_hi, f_lo, f_hi, lo_val, side, tau, r, done)

    pending0 = jnp.sum(1.0 - done).astype(jnp.int32)
    st = lax.while_loop(search_cond, search_body,
                        (jnp.int32(0), pending0, lo, hi, c_lo, c_hi, phi(c_lo), phi(c_hi), lo_val, row(0.0), tau, r,
                         done))
    tau, r = st[10], st[11]

    @pl.when(jnp.sum(r) > 0.0)
    def _():
        tied = r > 0.0
        budget = jnp.where(tied, r, jnp.inf)

        def body(kb, seen):
            blk = sc_ref[kb]
            eq = jnp.logical_and(blk == tau, tied)
            eq_f = jnp.where(eq, 1.0, 0.0)
            rank = jnp.dot(tri_ref[...], eq_f.astype(MXU_DTYPE), preferred_element_type=jnp.float32) + seen
            sc_ref[kb] = jnp.where(jnp.logical_and(eq, rank > budget), -jnp.inf, blk)
            return seen + colsum(eq_f)

        lax.fori_loop(0, n_blk, body, row(0.0))

    heads_per_kv = N_HEADS_B // N_KV_B

    def p3_body(kb, carry):
        start = pl.multiple_of(kb * tk, tk)
        kblk = kb_ref[pl.ds(start, tk), :]
        vt_all = vbt_ref[kb]
        bias = jnp.where(sc_ref[kb] >= tau, 0.0, NEG_BIG)
        scores = [lax.dot_general(kblk, qb_ref[:, h * LANES:(h + 1) * LANES], _NT,
                                  preferred_element_type=jnp.float32) + bias for h in range(N_HEADS_B)]
        values = [vt_all[(h // heads_per_kv) * HEAD_DIM_B:(h // heads_per_kv + 1) * HEAD_DIM_B, :]
                  for h in range(N_HEADS_B)]
        return _softmax_block(carry, scores, values, acc_ref)

    _, l = lax.fori_loop(0, n_blk, p3_body, _softmax_init(N_HEADS_B, tq, acc_ref))
    _softmax_finish(l, acc_ref, o_ref)


def _dsa_call(iq, qb, iw, ik, kb, vbt, tri, *, tq, tk, q_off, s_len):
    b, lq, _ = iq.shape
    s_pad = ik.shape[1]
    n_kb = s_pad // tk
    topk = min(TOPK_MAX, s_len // 4)
    kern = functools.partial(_dsa_kernel, tq=tq, tk=tk, q_off=q_off, s_len=s_len, topk=topk)
    qspec = lambda w: pl.BlockSpec((None, tq, w), lambda bi, qi: (bi, qi, 0))
    once = dict(pipeline_mode=pl.Buffered(1))
    kspec = pl.BlockSpec((None, s_pad, LANES), lambda bi, qi: (bi, 0, 0), **once)
    return pl.pallas_call(
        kern, grid=(b, lq // tq),
        in_specs=[qspec(N_IDX_HEADS * LANES), qspec(N_HEADS_B * LANES), qspec(LANES), kspec, kspec,
                  pl.BlockSpec((None, n_kb, LANES, tk), lambda bi, qi: (bi, 0, 0, 0), **once),
                  _const_spec(tri.shape)],
        out_specs=qspec(N_HEADS_B * HEAD_DIM_B),
        out_shape=jax.ShapeDtypeStruct((b, lq, N_HEADS_B * HEAD_DIM_B), MXU_DTYPE),
        scratch_shapes=[pltpu.VMEM((n_kb, tk, tq), jnp.float32), pltpu.VMEM((SUBLANES, tq), jnp.float32),
                        pltpu.VMEM((N_HEADS_B, HEAD_DIM_B, tq), jnp.float32)],
        compiler_params=_cparams(2), name="dsa",
    )(iq, qb, iw, ik, kb, vbt, tri)


_FF_CHUNK = 256


def _out_kernel(x_ref, oa_ref, ob_ref, woa_ref, wob_ref, gf_ref, wg_ref, wu_ref, wd_ref, gfin_ref, y_ref):
    o = (jnp.dot(oa_ref[...], woa_ref[...], preferred_element_type=jnp.float32)
         + jnp.dot(ob_ref[...], wob_ref[...], preferred_element_type=jnp.float32))
    x1 = x_ref[...] + o
    h2 = _rms(x1, gf_ref[...]).astype(MXU_DTYPE)
    ffn = jnp.zeros_like(x1)
    for j in range(D_FF // _FF_CHUNK):
        sl = slice(j * _FF_CHUNK, (j + 1) * _FF_CHUNK)
        gate = jnp.dot(h2, wg_ref[:, sl], preferred_element_type=jnp.float32)
        up = jnp.dot(h2, wu_ref[:, sl], preferred_element_type=jnp.float32)
        act = (gate * jax.nn.sigmoid(gate)) * up
        ffn = ffn + jnp.dot(act.astype(MXU_DTYPE), wd_ref[sl, :], preferred_element_type=jnp.float32)
    y_ref[...] = _rms(x1 + ffn, gfin_ref[...])


def _out_call(x2d, oa, ob, woa, wob, g_ffn, wg, wu, wd, g_fin, tm):
    t_tok = x2d.shape[0]
    row = lambda w: pl.BlockSpec((tm, w), lambda i: (i, 0))
    half = N_HEADS_A * V_DIM_A
    return pl.pallas_call(
        _out_kernel, grid=(t_tok // tm,),
        in_specs=[row(D_MODEL), row(half), row(half), _const_spec(woa.shape), _const_spec(wob.shape),
                  _const_spec((1, D_MODEL)), _const_spec(wg.shape), _const_spec(wu.shape), _const_spec(wd.shape),
                  _const_spec((1, D_MODEL))],
        out_specs=row(D_MODEL), out_shape=jax.ShapeDtypeStruct((t_tok, D_MODEL), jnp.float32),
        compiler_params=_cparams(1), name="out",
    )(x2d, oa, ob, woa, wob, g_ffn, wg, wu, wd, g_fin)


def _prep_weights(w_in, w_q_b, w_kv_b, w_o, w_gate, w_up, w_down):
    off = np.concatenate([[0], np.cumsum(IN_SIZES)])
    o_qlat, o_kvlat, o_kr, o_qb, o_kb, o_vb, o_iq, o_ik, o_iw = off[:9]
    zeros = lambda n: jnp.zeros((D_MODEL, n), w_in.dtype)
    cols = [w_in[:, o_qlat:o_qlat + Q_LORA], w_in[:, o_kvlat:o_kvlat + KV_LORA]]
    for h in range(N_HEADS_B):
        head = w_in[:, o_qb + h * HEAD_DIM_B:o_qb + (h + 1) * HEAD_DIM_B]
        cols += [head, zeros(HEAD_DIM_B)] if h < N_HEADS_B // N_KV_B else [zeros(HEAD_DIM_B), head]
    for h in range(N_IDX_HEADS):
        cols += [w_in[:, o_iq + h * IDX_DIM:o_iq + (h + 1) * IDX_DIM], zeros(LANES - IDX_DIM)]
    cols += [w_in[:, o_kb:o_kb + LANES], w_in[:, o_vb:o_vb + LANES],
             w_in[:, o_ik:o_ik + IDX_DIM], zeros(LANES - IDX_DIM),
             zeros(NOPE_DIM), w_in[:, o_kr:o_kr + ROPE_DIM], zeros(LANES - NOPE_DIM - ROPE_DIM),
             w_in[:, o_iw:o_iw + N_IDX_HEADS], zeros(LANES - N_IDX_HEADS)]
    w_in_p = jnp.concatenate(cols, axis=1).astype(MXU_DTYPE)
    assert w_in_p.shape[1] == _C_END

    qd = NOPE_DIM + ROPE_DIM
    w_qb_p = jnp.pad(w_q_b.reshape(Q_LORA, N_HEADS_A, qd), ((0, 0), (0, 0), (0, LANES - qd)))
    w_qb_p = w_qb_p.reshape(Q_LORA, N_HEADS_A * LANES).astype(MXU_DTYPE)

    kv = w_kv_b.reshape(KV_LORA, N_HEADS_A, NOPE_DIM + V_DIM_A)
    wk = jnp.pad(kv[:, :, :NOPE_DIM], ((0, 0), (0, 0), (0, LANES - NOPE_DIM)))
    wk = wk.reshape(KV_LORA, N_HEADS_A * LANES).astype(MXU_DTYPE)
    wvt = kv[:, :, NOPE_DIM:].reshape(KV_LORA, N_HEADS_A * V_DIM_A).T.astype(MXU_DTYPE)

    half = N_HEADS_A * V_DIM_A
    return (w_in_p, w_qb_p, wk, wvt, w_o[:half].astype(MXU_DTYPE), w_o[half:].astype(MXU_DTYPE),
            w_gate.astype(MXU_DTYPE), w_up.astype(MXU_DTYPE), w_down.astype(MXU_DTYPE))


def _rope_tables(pos):
    posf = pos.astype(jnp.float32)[:, None]
    n = pos.shape[0]

    def cs(d):
        inv = jnp.power(ROPE_THETA, -jnp.arange(0, d, 2, dtype=jnp.float32) / d)
        ang = posf * inv[None, :]
        return jnp.cos(ang), jnp.sin(ang)

    one, zero = (lambda w: jnp.ones((n, w), jnp.float32)), (lambda w: jnp.zeros((n, w), jnp.float32))
    c, s = cs(ROPE_DIM)
    cq = jnp.concatenate([one(NOPE_DIM), c, c, one(LANES - NOPE_DIM - ROPE_DIM)], axis=1)
    sq = jnp.concatenate([zero(NOPE_DIM), -s, s, zero(LANES - NOPE_DIM - ROPE_DIM)], axis=1)
    c, s = cs(ROT_DIM_B)
    c64 = jnp.tile(jnp.concatenate([c, c, one(HEAD_DIM_B - ROT_DIM_B)], axis=1), (1, LANES // HEAD_DIM_B))
    s64 = jnp.tile(jnp.concatenate([-s, s, zero(HEAD_DIM_B - ROT_DIM_B)], axis=1), (1, LANES // HEAD_DIM_B))
    return cq, sq, c64, s64


def _pick_tile(n, pref):
    t = min(pref, n)
    while n % t:
        t //= 2
    return t


def _round_up(n, m):
    return -(-n // m) * m


def _layer(x, pos0, past, weights, norms, final_norm):
    b, l, _ = x.shape
    g_attn, g_q, g_kv, g_ffn = norms
    w_in_p, w_qb_p, wk, wvt, woa, wob, wg, wu, wd = weights
    t_tok = b * l
    x2d = x.reshape(t_tok, D_MODEL)

    tabs = _rope_tables(pos0 + jnp.arange(l))
    if l >= TM // 2:
        tm = _pick_tile(l, TM)
        n_tab_tiles = l // tm
    else:
        tm = _pick_tile(t_tok, TM)
        tabs = tuple(jnp.tile(t, (tm // l, 1)) for t in tabs)
        n_tab_tiles = 1
    (ckv, kr128, kb, vb, ik128, iw, q_mla, q_b, iq, kb16, vb16, ik16) = _proj_call(
        x2d, tabs, n_tab_tiles, tm, g_attn, w_in_p, g_q, w_qb_p, g_kv)

    new = (ckv.reshape(b, l, KV_LORA), kr128.reshape(b, l, LANES)[..., NOPE_DIM:NOPE_DIM + ROPE_DIM],
           kb.reshape(b, l, N_KV_B, HEAD_DIM_B), vb.reshape(b, l, N_KV_B, HEAD_DIM_B),
           ik128.reshape(b, l, LANES)[..., :IDX_DIM])

    per_row = lambda a: a.reshape(b, l, a.shape[-1])
    p_len = 0 if past is None else past[0].shape[1]
    s_len = p_len + l
    s_pad = _round_up(s_len, max(TK_MLA, TK_DSA))
    if past is None:
        old = [None] * 5
    else:
        p_lat, p_kr, p_k, p_v, p_ik = past
        old = [p_lat,
               jnp.pad(p_kr, ((0, 0), (0, 0), (NOPE_DIM, LANES - NOPE_DIM - ROPE_DIM))),
               p_k.reshape(b, p_len, LANES).astype(MXU_DTYPE), p_v.reshape(b, p_len, LANES).astype(MXU_DTYPE),
               jnp.pad(p_ik, ((0, 0), (0, 0), (0, LANES - IDX_DIM))).astype(MXU_DTYPE)]

    def full_seq(prev, cur):
        parts = ([] if prev is None else [prev]) + [per_row(cur)]
        if s_pad > s_len:
            parts.append(jnp.zeros((b, s_pad - s_len, cur.shape[-1]), cur.dtype))
        return parts[0] if len(parts) == 1 else jnp.concatenate(parts, axis=1)

    c_all, kr_all, kb_all, vb_all, ik_all = (full_seq(o, c) for o, c in zip(old, (ckv, kr128, kb16, vb16, ik16)))
    vbt_all = vb_all.reshape(b, s_pad // TK_DSA, TK_DSA, LANES).transpose(0, 1, 3, 2)

    k_mla, vt_mla = _kvup_call(c_all, kr_all, wk, wvt, TK_MLA)

    lq = _round_up(l, LANES)
    tq = _pick_tile(lq, TQ)
    padq = lambda a: per_row(a) if lq == l else jnp.pad(per_row(a), ((0, 0), (0, lq - l), (0, 0)))
    o_a = _mla_call(padq(q_mla), k_mla, vt_mla, tq=tq, tk=TK_MLA, q_off=pos0, s_len=s_len)
    tri = jnp.tril(jnp.ones((TK_DSA, TK_DSA), MXU_DTYPE))
    o_b = _dsa_call(padq(iq), padq(q_b), padq(iw), ik_all, kb_all, vbt_all, tri,
                    tq=tq, tk=TK_DSA, q_off=pos0, s_len=s_len)
    o_a, o_b = (o[:, :l].reshape(t_tok, -1) for o in (o_a, o_b))

    y = _out_call(x2d, o_a, o_b, woa, wob, g_ffn, wg, wu, wd, final_norm, tm)
    return y.reshape(b, l, D_MODEL), new


def kernel(x_prompt, x_sample, cache_mla_latent, cache_mla_krope, cache_dsa_k, cache_dsa_v, cache_idx_k, attn_norm, w_in, q_a_norm, w_q_b, kv_a_norm, w_kv_b, w_o, ffn_norm, w_gate, w_up, w_down, final_norm):
    depth = w_in.shape[0]
    assert depth == 1, "the fused final norm assumes a single layer"
    past_len = cache_mla_latent.shape[2]
    li = 0
    weights = _prep_weights(w_in[li], w_q_b[li], w_kv_b[li], w_o[li], w_gate[li], w_up[li], w_down[li])
    norms = (attn_norm[li][None, :], q_a_norm[li][None, :], kv_a_norm[li][None, :], ffn_norm[li][None, :])
    fin = final_norm[None, :]
    past = (cache_mla_latent[li], cache_mla_krope[li], cache_dsa_k[li], cache_dsa_v[li], cache_idx_k[li])
    y_p, new_p = _layer(x_prompt, 0, None, weights, norms, fin)
    y_s, new_s = _layer(x_sample, past_len, past, weights, norms, fin)
    return (y_p, y_s) + tuple(a[None] for a in new_p) + tuple(a[None] for a in new_s)
```

```python
import functools

import numpy as np
import jax
import jax.numpy as jnp
from jax import lax
from jax.experimental import pallas as pl
from jax.experimental.pallas import tpu as pltpu

D_MODEL = 1024
CHUNK = 64
ROPE_THETA = 500000.0
EPS = 1e-6
N_HEADS_A = 8
NOPE_DIM = 64
ROPE_DIM = 32
V_DIM_A = 64
Q_LORA = 384
KV_LORA = 256
N_HEADS_B = 8
N_KV_B = 2
HEAD_DIM_B = 64
ROT_DIM_B = 16
N_IDX_HEADS = 8
IDX_DIM = 64
ROT_DIM_IDX = 16
TOPK_MAX = 256
D_FF = 2816
IN_SIZES = (Q_LORA, KV_LORA, ROPE_DIM, N_HEADS_B * HEAD_DIM_B, N_KV_B * HEAD_DIM_B, N_KV_B * HEAD_DIM_B,
            N_IDX_HEADS * IDX_DIM, IDX_DIM, N_IDX_HEADS)

LANES = 128
SUBLANES = 8
MXU_DTYPE = jnp.bfloat16
VMEM_LIMIT = 56 * 1024 * 1024
NEG_BIG = -1e30
LOWEST = -3.0e38
FLT_MIN = 1.1754943508222875e-38
LOG2E = 1.4426950408889634

TM = 512
TQ_MLA = 512
TK_MLA = 512
TQ_DSA = 256
TK_DSA = 512
MLA_HEADS_PER_STEP = 4

_C_QLAT = 0
_C_KVLAT = _C_QLAT + Q_LORA
_C_QB = _C_KVLAT + KV_LORA
_C_IQ = _C_QB + N_HEADS_B * LANES
_C_KB = _C_IQ + N_IDX_HEADS * LANES
_C_VB = _C_KB + LANES
_C_IK = _C_VB + LANES
_C_KR = _C_IK + LANES
_C_IW = _C_KR + LANES
_C_END = _C_IW + LANES


def _cparams(n_axes):
    return pltpu.CompilerParams(dimension_semantics=("arbitrary",) * n_axes, vmem_limit_bytes=VMEM_LIMIT)


def _const_spec(shape):
    return pl.BlockSpec(shape, lambda *_: (0,) * len(shape), pipeline_mode=pl.Buffered(1))


def _rms(x, g):
    return (x * lax.rsqrt(jnp.mean(x * x, axis=-1, keepdims=True) + EPS)) * g


def _rope_tile(x, cos, sin, half, first_half):
    partner = jnp.where(first_half, pltpu.roll(x, LANES - half, 1), pltpu.roll(x, half, 1))
    return x * cos + partner * sin


def _proj_kernel(x_ref, cq_ref, sq_ref, c64_ref, s64_ref, gat_ref, win_ref, gq_ref, wqb_ref, gkv_ref,
                 ckv_ref, kr_ref, kb_ref, vb_ref, ik_ref, iw_ref, qm_ref, qb_ref, iq_ref,
                 kb16_ref, vb16_ref, ik16_ref):
    x = x_ref[...]
    hb = _rms(x, gat_ref[...]).astype(MXU_DTYPE)
    y = jnp.dot(hb, win_ref[...], preferred_element_type=jnp.float32)

    lane = lax.broadcasted_iota(jnp.int32, (1, LANES), 1)
    first_q = lane < NOPE_DIM + ROPE_DIM // 2
    first_64 = (lane & (HEAD_DIM_B - 1)) < ROT_DIM_B // 2
    cq, sq, c64, s64 = cq_ref[...], sq_ref[...], c64_ref[...], s64_ref[...]

    qn = _rms(y[:, _C_QLAT:_C_QLAT + Q_LORA], gq_ref[...]).astype(MXU_DTYPE)
    qa = jnp.dot(qn, wqb_ref[...], preferred_element_type=jnp.float32)
    scale_a = (NOPE_DIM + ROPE_DIM) ** -0.5 * LOG2E
    for h in range(N_HEADS_A):
        t = _rope_tile(qa[:, h * LANES:(h + 1) * LANES], cq, sq, ROPE_DIM // 2, first_q)
        qm_ref[:, h * LANES:(h + 1) * LANES] = (t * scale_a).astype(qm_ref.dtype)

    ckv_ref[...] = _rms(y[:, _C_KVLAT:_C_KVLAT + KV_LORA], gkv_ref[...])
    kr_ref[...] = _rope_tile(y[:, _C_KR:_C_KR + LANES], cq, sq, ROPE_DIM // 2, first_q)

    scale_b = HEAD_DIM_B ** -0.5 * LOG2E
    for h in range(N_HEADS_B):
        t = _rope_tile(y[:, _C_QB + h * LANES:_C_QB + (h + 1) * LANES], c64, s64, ROT_DIM_B // 2, first_64)
        qb_ref[:, h * LANES:(h + 1) * LANES] = (t * scale_b).astype(qb_ref.dtype)
    for h in range(N_IDX_HEADS):
        t = _rope_tile(y[:, _C_IQ + h * LANES:_C_IQ + (h + 1) * LANES], c64, s64, ROT_DIM_IDX // 2, first_64)
        iq_ref[:, h * LANES:(h + 1) * LANES] = t.astype(iq_ref.dtype)

    kb = _rope_tile(y[:, _C_KB:_C_KB + LANES], c64, s64, ROT_DIM_B // 2, first_64)
    kb_ref[...] = kb
    kb16_ref[...] = kb.astype(kb16_ref.dtype)
    vb = y[:, _C_VB:_C_VB + LANES]
    vb_ref[...] = vb
    vb16_ref[...] = vb.astype(vb16_ref.dtype)
    ik = _rope_tile(y[:, _C_IK:_C_IK + LANES], c64, s64, ROT_DIM_IDX // 2, first_64)
    ik_ref[...] = ik
    ik16_ref[...] = ik.astype(ik16_ref.dtype)
    iw_ref[...] = y[:, _C_IW:_C_IW + LANES] * ((N_IDX_HEADS * IDX_DIM) ** -0.5)


def _proj_call(x2d, tabs, n_tab_tiles, tm, g_attn, w_in_p, g_q, w_qb_p, g_kv):
    t_tok = x2d.shape[0]
    grid = (t_tok // tm,)
    row = lambda w: pl.BlockSpec((tm, w), lambda i: (i, 0))
    tab = pl.BlockSpec((tm, LANES), lambda i: (i % n_tab_tiles, 0))
    f32, b16 = jnp.float32, MXU_DTYPE
    out_shapes = [
        jax.ShapeDtypeStruct((t_tok, KV_LORA), f32),
        jax.ShapeDtypeStruct((t_tok, LANES), f32),
        jax.ShapeDtypeStruct((t_tok, LANES), f32),
        jax.ShapeDtypeStruct((t_tok, LANES), f32),
        jax.ShapeDtypeStruct((t_tok, LANES), f32),
        jax.ShapeDtypeStruct((t_tok, LANES), f32),
        jax.ShapeDtypeStruct((t_tok, N_HEADS_A * LANES), b16),
        jax.ShapeDtypeStruct((t_tok, N_HEADS_B * LANES), b16),
        jax.ShapeDtypeStruct((t_tok, N_IDX_HEADS * LANES), b16),
        jax.ShapeDtypeStruct((t_tok, LANES), b16),
        jax.ShapeDtypeStruct((t_tok, LANES), b16),
        jax.ShapeDtypeStruct((t_tok, LANES), b16),
    ]
    out_specs = [row(s.shape[1]) for s in out_shapes]
    in_specs = [row(D_MODEL), tab, tab, tab, tab, _const_spec((1, D_MODEL)), _const_spec(w_in_p.shape),
                _const_spec((1, Q_LORA)), _const_spec(w_qb_p.shape), _const_spec((1, KV_LORA))]
    return pl.pallas_call(
        _proj_kernel, grid=grid, in_specs=in_specs, out_specs=out_specs, out_shape=out_shapes,
        compiler_params=_cparams(1), name="proj",
    )(x2d, *tabs, g_attn, w_in_p, g_q, w_qb_p, g_kv)


_NT = (((1,), (1,)), ((), ()))


def _kvup_kernel(c_ref, kr_ref, wk_ref, wvt_ref, k_ref, vt_ref):
    cb = c_ref[...].astype(MXU_DTYPE)
    kn = jnp.dot(cb, wk_ref[...], preferred_element_type=jnp.float32)
    kr = kr_ref[...]
    for h in range(N_HEADS_A):
        k_ref[:, h * LANES:(h + 1) * LANES] = (kn[:, h * LANES:(h + 1) * LANES] + kr).astype(k_ref.dtype)
    vt_ref[...] = lax.dot_general(wvt_ref[...], cb, _NT, preferred_element_type=jnp.float32).astype(vt_ref.dtype)


def _kvup_call(c3d, kr3d, wk, wvt, tk):
    b, s_pad, _ = c3d.shape
    row = lambda w: pl.BlockSpec((None, tk, w), lambda bi, i: (bi, i, 0))
    vdim = N_HEADS_A * V_DIM_A
    return pl.pallas_call(
        _kvup_kernel, grid=(b, s_pad // tk),
        in_specs=[row(KV_LORA), row(LANES), _const_spec(wk.shape), _const_spec(wvt.shape)],
        out_specs=[row(N_HEADS_A * LANES), pl.BlockSpec((None, None, vdim, tk), lambda bi, i: (bi, i, 0, 0))],
        out_shape=[jax.ShapeDtypeStruct((b, s_pad, N_HEADS_A * LANES), MXU_DTYPE),
                   jax.ShapeDtypeStruct((b, s_pad // tk, vdim, tk), MXU_DTYPE)],
        compiler_params=_cparams(2), name="kvup",
    )(c3d, kr3d, wk, wvt)


_CHUNK_SHIFT = CHUNK.bit_length() - 1


def _admissible_t(p0, kb, tq, tk, s_len):
    kpos = kb * tk + lax.broadcasted_iota(jnp.int32, (tk, 1), 0)
    qpos = p0 + lax.broadcasted_iota(jnp.int32, (1, tq), 1)
    return ((qpos >> _CHUNK_SHIFT) >= (kpos >> _CHUNK_SHIFT)) & (kpos < s_len)


def _last_block(p0, tq, tk, s_len):
    return jnp.minimum(p0 + tq - 1, s_len - 1) >> (tk.bit_length() - 1)


def _softmax_init(n_heads, tq, acc_ref):
    acc_ref[...] = jnp.zeros(acc_ref.shape, jnp.float32)
    return jnp.full((n_heads, tq), NEG_BIG, jnp.float32), jnp.zeros((n_heads, tq), jnp.float32)


def _attn_step(state, values, next_scores, bias, s_ref, acc_ref):
    (m_prev, l_prev), maxima = state
    n_heads = s_ref.shape[0]
    sums, next_max = [], []
    if values is not None:
        m_new = jnp.maximum(m_prev, maxima)
        alpha = jnp.exp2(m_prev - m_new)
    for h in range(n_heads):
        if values is not None:
            p = jnp.exp2(s_ref[h] - m_new[h:h + 1, :])
            sums.append(jnp.sum(p, axis=0, keepdims=True))
            acc_ref[h] = alpha[h:h + 1, :] * acc_ref[h] + jnp.dot(values[h], p.astype(values[h].dtype),
                                                                   preferred_element_type=jnp.float32)
        if next_scores is not None:
            s = next_scores(h)
            if bias is not None:
                s = s + bias
            s_ref[h] = s
            next_max.append(jnp.max(s, axis=0, keepdims=True))
    carry = (m_new, alpha * l_prev + jnp.concatenate(sums, axis=0)) if values is not None else (m_prev, l_prev)
    return carry, (jnp.concatenate(next_max, axis=0) if next_scores is not None else maxima)


def _softmax_finish(l, acc_ref, o_ref):
    inv = 1.0 / l
    o_t = jnp.concatenate([acc_ref[h] * inv[h:h + 1, :] for h in range(l.shape[0])], axis=0)
    o_ref[...] = o_t.T.astype(o_ref.dtype)


def _mla_kernel(q_ref, k_ref, vt_ref, o_ref, s_ref, acc_ref, *, tq, tk, q_off, s_len):
    p0 = q_off + pl.program_id(2) * tq
    last = _last_block(p0, tq, tk, s_len)
    heads = range(MLA_HEADS_PER_STEP)

    def qk(kb):
        start = pl.multiple_of(kb * tk, tk)
        return lambda h: lax.dot_general(k_ref[pl.ds(start, tk), h * LANES:(h + 1) * LANES],
                                         q_ref[:, h * LANES:(h + 1) * LANES], _NT,
                                         preferred_element_type=jnp.float32)

    def mask_bias(kb):
        return jnp.where(_admissible_t(p0, kb, tq, tk, s_len), 0.0, NEG_BIG)

    def values(kb):
        vt_all = vt_ref[kb]
        return [vt_all[h * V_DIM_A:(h + 1) * V_DIM_A, :] for h in heads]

    def body(kb, state):
        return _attn_step(state, values(kb), qk(kb + 1), None, s_ref, acc_ref)

    def tail(state):
        return _attn_step(state, values(last - 1), qk(last), mask_bias(last), s_ref, acc_ref)

    state = _attn_step((_softmax_init(MLA_HEADS_PER_STEP, tq, acc_ref), None), None, qk(0), mask_bias(0), s_ref,
                       acc_ref)
    state = lax.fori_loop(0, jnp.maximum(last - 1, 0), body, state)
    state = lax.cond(last > 0, tail, lambda st: st, state)
    (_, l), _ = _attn_step(state, values(last), None, None, s_ref, acc_ref)
    _softmax_finish(l, acc_ref, o_ref)


def _mla_call(q, k, vt, *, tq, tk, q_off, s_len):
    b, lq, _ = q.shape
    s_pad = k.shape[1]
    hg = MLA_HEADS_PER_STEP
    kern = functools.partial(_mla_kernel, tq=tq, tk=tk, q_off=q_off, s_len=s_len)
    once = dict(pipeline_mode=pl.Buffered(1))
    return pl.pallas_call(
        kern, grid=(b, N_HEADS_A // hg, lq // tq),
        in_specs=[pl.BlockSpec((None, tq, hg * LANES), lambda bi, g, qi: (bi, qi, g)),
                  pl.BlockSpec((None, s_pad, hg * LANES), lambda bi, g, qi: (bi, 0, g), **once),
                  pl.BlockSpec((None, s_pad // tk, hg * V_DIM_A, tk), lambda bi, g, qi: (bi, 0, g, 0), **once)],
        out_specs=pl.BlockSpec((None, tq, hg * V_DIM_A), lambda bi, g, qi: (bi, qi, g)),
        out_shape=jax.ShapeDtypeStruct((b, lq, N_HEADS_A * V_DIM_A), MXU_DTYPE),
        scratch_shapes=[pltpu.VMEM((hg, tk, tq), jnp.float32), pltpu.VMEM((hg, V_DIM_A, tq), jnp.float32)],
        compiler_params=_cparams(3), name="mla",
    )(q, k, vt)


def _f2key(x):
    b = lax.bitcast_convert_type(x, jnp.int32)
    return jnp.where(b < 0, b ^ jnp.int32(0x7FFFFFFF), b)


def _key2f(k):
    return lax.bitcast_convert_type(jnp.where(k < 0, k ^ jnp.int32(0x7FFFFFFF), k), jnp.float32)


_COUNT_SLAB = 4 * SUBLANES
_SEARCH_BISECT_EVERY = 8
_SEARCH_MAX_PROBES = 40 * _SEARCH_BISECT_EVERY


def _dsa_kernel(iq_ref, qb_ref, iw_ref, ik_ref, kb_ref, vbt_ref, tri_ref, o_ref,
                sc_ref, wt_ref, s_ref, acc_ref, *, tq, tk, q_off, s_len, topk):
    p0 = q_off + pl.program_id(1) * tq
    last = _last_block(p0, tq, tk, s_len)
    n_blk = last + 1
    row = lambda v: jnp.full((1, tq), v, jnp.float32)
    colsum = lambda x: jnp.sum(x, axis=0, keepdims=True)

    wt_ref[...] = iw_ref[...].T[:SUBLANES, :]

    def score_block(kb):
        start = pl.multiple_of(kb * tk, tk)
        ikb = ik_ref[pl.ds(start, tk), :]
        acc = jnp.zeros((tk, tq), jnp.float32)
        for h in range(N_IDX_HEADS):
            x = lax.dot_general(ikb, iq_ref[:, h * LANES:(h + 1) * LANES], _NT, preferred_element_type=jnp.float32)
            acc = acc + wt_ref[h:h + 1, :] * jnp.maximum(x, 0.0)
        return acc

    def stats(carry, hi_fill, lo_fill):
        mx, mn, n_pos, n_nonneg = carry
        return (jnp.maximum(mx, jnp.max(hi_fill, axis=0, keepdims=True)),
                jnp.minimum(mn, jnp.min(lo_fill, axis=0, keepdims=True)),
                n_pos + colsum(jnp.where(hi_fill >= FLT_MIN, 1.0, 0.0)),
                n_nonneg + colsum(jnp.where(hi_fill >= 0.0, 1.0, 0.0)))

    def p1_body(kb, carry):
        acc = score_block(kb)
        sc_ref[kb] = acc
        return stats(carry, acc, acc)

    carry = lax.fori_loop(0, last, p1_body, (row(-jnp.inf), row(jnp.inf), row(0.0), row(0.0)))
    acc = score_block(last)
    adm = _admissible_t(p0, last, tq, tk, s_len)
    masked = jnp.where(adm, acc, -jnp.inf)
    sc_ref[last] = masked
    mx, mn, n_pos, n_nonneg = stats(carry, masked, jnp.where(adm, acc, jnp.inf))

    qpos = p0 + lax.broadcasted_iota(jnp.int32, (1, tq), 1)
    n_adm = jnp.minimum(((qpos >> _CHUNK_SHIFT) + 1) * CHUNK, s_len).astype(jnp.float32)
    k_f = jnp.float32(topk)
    log_k = jnp.log(k_f)
    phi = lambda c: jnp.log(jnp.maximum(c, 0.25)) - log_k

    few = n_adm <= k_f
    pos = n_pos >= k_f
    at_zero = jnp.logical_and(~pos, n_nonneg >= k_f)
    above_max = _key2f(_f2key(jnp.maximum(mx, LOWEST)) + 1)
    lo = jnp.where(pos, FLT_MIN, LOWEST)
    c_lo = jnp.where(pos, n_pos, n_adm)
    hi = jnp.where(pos | at_zero, above_max, 0.0)
    c_hi = jnp.where(pos | at_zero, 0.0, n_nonneg)
    lo_val = jnp.where(pos, 0.0, jnp.maximum(mn, LOWEST))
    tau = jnp.where(few, LOWEST, jnp.where(at_zero, 0.0, lo))
    r = jnp.where(at_zero & ~few, k_f - n_pos, 0.0)
    done = jnp.where(few | at_zero, 1.0, 0.0)

    def count_ge(g):
        def body(kb, cnt):
            hit = jnp.where(sc_ref[kb] >= g, 1.0, 0.0)
            return cnt + jnp.sum(hit.reshape(tk // _COUNT_SLAB, _COUNT_SLAB, tq), axis=0)

        return colsum(lax.fori_loop(0, n_blk, body, jnp.zeros((_COUNT_SLAB, tq), jnp.float32)))

    def search_cond(st):
        return jnp.logical_and(st[1] > 0, st[0] < _SEARCH_MAX_PROBES)

    def search_body(st):
        it, _, lo, hi, c_lo, c_hi, f_lo, f_hi, lo_val, side, tau, r, done = st
        lok, hik = _f2key(lo), _f2key(hi)
        midk = (lok >> 1) + (hik >> 1) + (lok & hik & 1)
        frac = f_lo / jnp.maximum(f_lo - f_hi, 1e-9)
        g_int = lo_val + (hi - lo_val) * frac
        bisect = (it % _SEARCH_BISECT_EVERY) == (_SEARCH_BISECT_EVERY - 1)
        g = jnp.where(bisect, _key2f(midk), g_int)
        gk = jnp.minimum(jnp.maximum(_f2key(g), lok + 1), jnp.maximum(hik - 1, lok + 1))
        g = _key2f(gk)
        c = count_ge(g)
        f_g = phi(c)
        live = done == 0.0
        up = live & (c >= k_f)
        dn = live & (c < k_f)
        f_hi = jnp.where(up & (side > 0.0), f_hi * 0.5, f_hi)
        f_lo = jnp.where(dn & (side < 0.0), f_lo * 0.5, f_lo)
        lo, c_lo, f_lo, lo_val = (jnp.where(up, g, lo), jnp.where(up, c, c_lo), jnp.where(up, f_g, f_lo),
                                  jnp.where(up, g, lo_val))
        hi, c_hi, f_hi = jnp.where(dn, g, hi), jnp.where(dn, c, c_hi), jnp.where(dn, f_g, f_hi)
        side = jnp.where(up, 1.0, jnp.where(dn, -1.0, side))
        adjacent = (_f2key(lo) + 1) >= _f2key(hi)
        fin = live & ((c_lo == k_f) | adjacent)
        tau = jnp.where(fin, lo, tau)
        r = jnp.where(fin & (c_lo > k_f), k_f - c_hi, r)
        done = jnp.where(fin, 1.0, done)
        pending = jnp.sum(1.0 - done).astype(jnp.int32)
        return (it + 1, pending, lo, hi, c_lo, c_hi, f_lo, f_hi, lo_val, side, tau, r, done)

    pending0 = jnp.sum(1.0 - done).astype(jnp.int32)
    st = lax.while_loop(search_cond, search_body,
                        (jnp.int32(0), pending0, lo, hi, c_lo, c_hi, phi(c_lo), phi(c_hi), lo_val, row(0.0), tau, r,
                         done))
    tau, r = st[10], st[11]

    @pl.when(jnp.sum(r) > 0.0)
    def _():
        tied = r > 0.0
        budget = jnp.where(tied, r, jnp.inf)

        def body(kb, seen):
            blk = sc_ref[kb]
            eq = jnp.logical_and(blk == tau, tied)
            eq_f = jnp.where(eq, 1.0, 0.0)
            rank = jnp.dot(tri_ref[...], eq_f.astype(MXU_DTYPE), preferred_element_type=jnp.float32) + seen
            sc_ref[kb] = jnp.where(jnp.logical_and(eq, rank > budget), -jnp.inf, blk)
            return seen + colsum(eq_f)

        lax.fori_loop(0, n_blk, body, row(0.0))

    heads_per_kv = N_HEADS_B // N_KV_B

    def qk(kb):
        start = pl.multiple_of(kb * tk, tk)
        kblk = kb_ref[pl.ds(start, tk), :]
        return lambda h: lax.dot_general(kblk, qb_ref[:, h * LANES:(h + 1) * LANES], _NT,
                                         preferred_element_type=jnp.float32)

    def sel_bias(kb):
        return jnp.where(sc_ref[kb] >= tau, 0.0, NEG_BIG)

    def values(kb):
        vt_all = vbt_ref[kb]
        return [vt_all[(h // heads_per_kv) * HEAD_DIM_B:(h // heads_per_kv + 1) * HEAD_DIM_B, :]
                for h in range(N_HEADS_B)]

    def p3_body(kb, state):
        return _attn_step(state, values(kb), qk(kb + 1), sel_bias(kb + 1), s_ref, acc_ref)

    state = _attn_step((_softmax_init(N_HEADS_B, tq, acc_ref), None), None, qk(0), sel_bias(0), s_ref, acc_ref)
    state = lax.fori_loop(0, last, p3_body, state)
    (_, l), _ = _attn_step(state, values(last), None, None, s_ref, acc_ref)
    _softmax_finish(l, acc_ref, o_ref)


def _dsa_call(iq, qb, iw, ik, kb, vbt, tri, *, tq, tk, q_off, s_len):
    b, lq, _ = iq.shape
    s_pad = ik.shape[1]
    n_kb = s_pad // tk
    topk = min(TOPK_MAX, s_len // 4)
    kern = functools.partial(_dsa_kernel, tq=tq, tk=tk, q_off=q_off, s_len=s_len, topk=topk)
    qspec = lambda w: pl.BlockSpec((None, tq, w), lambda bi, qi: (bi, qi, 0))
    once = dict(pipeline_mode=pl.Buffered(1))
    kspec = pl.BlockSpec((None, s_pad, LANES), lambda bi, qi: (bi, 0, 0), **once)
    return pl.pallas_call(
        kern, grid=(b, lq // tq),
        in_specs=[qspec(N_IDX_HEADS * LANES), qspec(N_HEADS_B * LANES), qspec(LANES), kspec, kspec,
                  pl.BlockSpec((None, n_kb, LANES, tk), lambda bi, qi: (bi, 0, 0, 0), **once),
                  _const_spec(tri.shape)],
        out_specs=qspec(N_HEADS_B * HEAD_DIM_B),
        out_shape=jax.ShapeDtypeStruct((b, lq, N_HEADS_B * HEAD_DIM_B), MXU_DTYPE),
        scratch_shapes=[pltpu.VMEM((n_kb, tk, tq), jnp.float32), pltpu.VMEM((SUBLANES, tq), jnp.float32),
                        pltpu.VMEM((N_HEADS_B, tk, tq), jnp.float32),
                        pltpu.VMEM((N_HEADS_B, HEAD_DIM_B, tq), jnp.float32)],
        compiler_params=_cparams(2), name="dsa",
    )(iq, qb, iw, ik, kb, vbt, tri)


_FF_CHUNK = 256


def _out_kernel(x_ref, oa_ref, ob_ref, woa_ref, wob_ref, gf_ref, wg_ref, wu_ref, wd_ref, gfin_ref, y_ref):
    o = (jnp.dot(oa_ref[...], woa_ref[...], preferred_element_type=jnp.float32)
         + jnp.dot(ob_ref[...], wob_ref[...], preferred_element_type=jnp.float32))
    x1 = x_ref[...] + o
    h2 = _rms(x1, gf_ref[...]).astype(MXU_DTYPE)
    ffn = jnp.zeros_like(x1)
    for j in range(D_FF // _FF_CHUNK):
        sl = slice(j * _FF_CHUNK, (j + 1) * _FF_CHUNK)
        gate = jnp.dot(h2, wg_ref[:, sl], preferred_element_type=jnp.float32)
        up = jnp.dot(h2, wu_ref[:, sl], preferred_element_type=jnp.float32)
        act = (gate * jax.nn.sigmoid(gate)) * up
        ffn = ffn + jnp.dot(act.astype(MXU_DTYPE), wd_ref[sl, :], preferred_element_type=jnp.float32)
    y_ref[...] = _rms(x1 + ffn, gfin_ref[...])


def _out_call(x2d, oa, ob, woa, wob, g_ffn, wg, wu, wd, g_fin, tm):
    t_tok = x2d.shape[0]
    row = lambda w: pl.BlockSpec((tm, w), lambda i: (i, 0))
    half = N_HEADS_A * V_DIM_A
    return pl.pallas_call(
        _out_kernel, grid=(t_tok // tm,),
        in_specs=[row(D_MODEL), row(half), row(half), _const_spec(woa.shape), _const_spec(wob.shape),
                  _const_spec((1, D_MODEL)), _const_spec(wg.shape), _const_spec(wu.shape), _const_spec(wd.shape),
                  _const_spec((1, D_MODEL))],
        out_specs=row(D_MODEL), out_shape=jax.ShapeDtypeStruct((t_tok, D_MODEL), jnp.float32),
        compiler_params=_cparams(1), name="out",
    )(x2d, oa, ob, woa, wob, g_ffn, wg, wu, wd, g_fin)


def _prep_weights(w_in, w_q_b, w_kv_b, w_o, w_gate, w_up, w_down):
    off = np.concatenate([[0], np.cumsum(IN_SIZES)])
    o_qlat, o_kvlat, o_kr, o_qb, o_kb, o_vb, o_iq, o_ik, o_iw = off[:9]
    zeros = lambda n: jnp.zeros((D_MODEL, n), w_in.dtype)
    cols = [w_in[:, o_qlat:o_qlat + Q_LORA], w_in[:, o_kvlat:o_kvlat + KV_LORA]]
    for h in range(N_HEADS_B):
        head = w_in[:, o_qb + h * HEAD_DIM_B:o_qb + (h + 1) * HEAD_DIM_B]
        cols += [head, zeros(HEAD_DIM_B)] if h < N_HEADS_B // N_KV_B else [zeros(HEAD_DIM_B), head]
    for h in range(N_IDX_HEADS):
        cols += [w_in[:, o_iq + h * IDX_DIM:o_iq + (h + 1) * IDX_DIM], zeros(LANES - IDX_DIM)]
    cols += [w_in[:, o_kb:o_kb + LANES], w_in[:, o_vb:o_vb + LANES],
             w_in[:, o_ik:o_ik + IDX_DIM], zeros(LANES - IDX_DIM),
             zeros(NOPE_DIM), w_in[:, o_kr:o_kr + ROPE_DIM], zeros(LANES - NOPE_DIM - ROPE_DIM),
             w_in[:, o_iw:o_iw + N_IDX_HEADS], zeros(LANES - N_IDX_HEADS)]
    w_in_p = jnp.concatenate(cols, axis=1).astype(MXU_DTYPE)
    assert w_in_p.shape[1] == _C_END

    qd = NOPE_DIM + ROPE_DIM
    w_qb_p = jnp.pad(w_q_b.reshape(Q_LORA, N_HEADS_A, qd), ((0, 0), (0, 0), (0, LANES - qd)))
    w_qb_p = w_qb_p.reshape(Q_LORA, N_HEADS_A * LANES).astype(MXU_DTYPE)

    kv = w_kv_b.reshape(KV_LORA, N_HEADS_A, NOPE_DIM + V_DIM_A)
    wk = jnp.pad(kv[:, :, :NOPE_DIM], ((0, 0), (0, 0), (0, LANES - NOPE_DIM)))
    wk = wk.reshape(KV_LORA, N_HEADS_A * LANES).astype(MXU_DTYPE)
    wvt = kv[:, :, NOPE_DIM:].reshape(KV_LORA, N_HEADS_A * V_DIM_A).T.astype(MXU_DTYPE)

    half = N_HEADS_A * V_DIM_A
    return (w_in_p, w_qb_p, wk, wvt, w_o[:half].astype(MXU_DTYPE), w_o[half:].astype(MXU_DTYPE),
            w_gate.astype(MXU_DTYPE), w_up.astype(MXU_DTYPE), w_down.astype(MXU_DTYPE))


def _rope_tables(pos):
    posf = pos.astype(jnp.float32)[:, None]
    n = pos.shape[0]

    def cs(d):
        inv = jnp.power(ROPE_THETA, -jnp.arange(0, d, 2, dtype=jnp.float32) / d)
        ang = posf * inv[None, :]
        return jnp.cos(ang), jnp.sin(ang)

    one, zero = (lambda w: jnp.ones((n, w), jnp.float32)), (lambda w: jnp.zeros((n, w), jnp.float32))
    c, s = cs(ROPE_DIM)
    cq = jnp.concatenate([one(NOPE_DIM), c, c, one(LANES - NOPE_DIM - ROPE_DIM)], axis=1)
    sq = jnp.concatenate([zero(NOPE_DIM), -s, s, zero(LANES - NOPE_DIM - ROPE_DIM)], axis=1)
    c, s = cs(ROT_DIM_B)
    c64 = jnp.tile(jnp.concatenate([c, c, one(HEAD_DIM_B - ROT_DIM_B)], axis=1), (1, LANES // HEAD_DIM_B))
    s64 = jnp.tile(jnp.concatenate([-s, s, zero(HEAD_DIM_B - ROT_DIM_B)], axis=1), (1, LANES // HEAD_DIM_B))
    return cq, sq, c64, s64


def _pick_tile(n, pref):
    t = min(pref, n)
    while n % t:
        t //= 2
    return t


def _round_up(n, m):
    return -(-n // m) * m


def _layer(x, pos0, past, weights, norms, final_norm):
    b, l, _ = x.shape
    g_attn, g_q, g_kv, g_ffn = norms
    w_in_p, w_qb_p, wk, wvt, woa, wob, wg, wu, wd = weights
    t_tok = b * l
    x2d = x.reshape(t_tok, D_MODEL)

    tabs = _rope_tables(pos0 + jnp.arange(l))
    if l >= TM // 2:
        tm = _pick_tile(l, TM)
        n_tab_tiles = l // tm
    else:
        tm = _pick_tile(t_tok, TM)
        tabs = tuple(jnp.tile(t, (tm // l, 1)) for t in tabs)
        n_tab_tiles = 1
    (ckv, kr128, kb, vb, ik128, iw, q_mla, q_b, iq, kb16, vb16, ik16) = _proj_call(
        x2d, tabs, n_tab_tiles, tm, g_attn, w_in_p, g_q, w_qb_p, g_kv)

    new = (ckv.reshape(b, l, KV_LORA), kr128.reshape(b, l, LANES)[..., NOPE_DIM:NOPE_DIM + ROPE_DIM],
           kb.reshape(b, l, N_KV_B, HEAD_DIM_B), vb.reshape(b, l, N_KV_B, HEAD_DIM_B),
           ik128.reshape(b, l, LANES)[..., :IDX_DIM])

    per_row = lambda a: a.reshape(b, l, a.shape[-1])
    p_len = 0 if past is None else past[0].shape[1]
    s_len = p_len + l
    s_pad = _round_up(s_len, max(TK_MLA, TK_DSA))
    if past is None:
        old = [None] * 5
    else:
        p_lat, p_kr, p_k, p_v, p_ik = past
        old = [p_lat,
               jnp.pad(p_kr, ((0, 0), (0, 0), (NOPE_DIM, LANES - NOPE_DIM - ROPE_DIM))),
               p_k.reshape(b, p_len, LANES).astype(MXU_DTYPE), p_v.reshape(b, p_len, LANES).astype(MXU_DTYPE),
               jnp.pad(p_ik, ((0, 0), (0, 0), (0, LANES - IDX_DIM))).astype(MXU_DTYPE)]

    def full_seq(prev, cur):
        parts = ([] if prev is None else [prev]) + [per_row(cur)]
        if s_pad > s_len:
            parts.append(jnp.zeros((b, s_pad - s_len, cur.shape[-1]), cur.dtype))
        return parts[0] if len(parts) == 1 else jnp.concatenate(parts, axis=1)

    c_all, kr_all, kb_all, vb_all, ik_all = (full_seq(o, c) for o, c in zip(old, (ckv, kr128, kb16, vb16, ik16)))
    vbt_all = vb_all.reshape(b, s_pad // TK_DSA, TK_DSA, LANES).transpose(0, 1, 3, 2)

    k_mla, vt_mla = _kvup_call(c_all, kr_all, wk, wvt, TK_MLA)

    lq = _round_up(l, LANES)
    padq = lambda a: per_row(a) if lq == l else jnp.pad(per_row(a), ((0, 0), (0, lq - l), (0, 0)))
    o_a = _mla_call(padq(q_mla), k_mla, vt_mla, tq=_pick_tile(lq, TQ_MLA), tk=TK_MLA, q_off=pos0, s_len=s_len)
    tri = jnp.tril(jnp.ones((TK_DSA, TK_DSA), MXU_DTYPE))
    o_b = _dsa_call(padq(iq), padq(q_b), padq(iw), ik_all, kb_all, vbt_all, tri,
                    tq=_pick_tile(lq, TQ_DSA), tk=TK_DSA, q_off=pos0, s_len=s_len)
    o_a, o_b = (o[:, :l].reshape(t_tok, -1) for o in (o_a, o_b))

    y = _out_call(x2d, o_a, o_b, woa, wob, g_ffn, wg, wu, wd, final_norm, tm)
    return y.reshape(b, l, D_MODEL), new


def kernel(x_prompt, x_sample, cache_mla_latent, cache_mla_krope, cache_dsa_k, cache_dsa_v, cache_idx_k, attn_norm, w_in, q_a_norm, w_q_b, kv_a_norm, w_kv_b, w_o, ffn_norm, w_gate, w_up, w_down, final_norm):
    depth = w_in.shape[0]
    assert depth == 1, "the fused final norm assumes a single layer"
    past_len = cache_mla_latent.shape[2]
    li = 0
    weights = _prep_weights(w_in[li], w_q_b[li], w_kv_b[li], w_o[li], w_gate[li], w_up[li], w_down[li])
    norms = (attn_norm[li][None, :], q_a_norm[li][None, :], kv_a_norm[li][None, :], ffn_norm[li][None, :])
    fin = final_norm[None, :]
    past = (cache_mla_latent[li], cache_mla_krope[li], cache_dsa_k[li], cache_dsa_v[li], cache_idx_k[li])
    y_p, new_p = _layer(x_prompt, 0, None, weights, norms, fin)
    y_s, new_s = _layer(x_sample, past_len, past, weights, norms, fin)
    return (y_p, y_s) + tuple(a[None] for a in new_p) + tuple(a[None] for a in new_s)
```

```python
import functools

import numpy as np
import jax
import jax.numpy as jnp
from jax import lax
from jax.experimental import pallas as pl
from jax.experimental.pallas import tpu as pltpu

D_MODEL = 1024
CHUNK = 64
ROPE_THETA = 500000.0
EPS = 1e-6
N_HEADS_A = 8
NOPE_DIM = 64
ROPE_DIM = 32
V_DIM_A = 64
Q_LORA = 384
KV_LORA = 256
N_HEADS_B = 8
N_KV_B = 2
HEAD_DIM_B = 64
ROT_DIM_B = 16
N_IDX_HEADS = 8
IDX_DIM = 64
ROT_DIM_IDX = 16
TOPK_MAX = 256
D_FF = 2816
IN_SIZES = (Q_LORA, KV_LORA, ROPE_DIM, N_HEADS_B * HEAD_DIM_B, N_KV_B * HEAD_DIM_B, N_KV_B * HEAD_DIM_B,
            N_IDX_HEADS * IDX_DIM, IDX_DIM, N_IDX_HEADS)

LANES = 128
SUBLANES = 8
MXU_DTYPE = jnp.bfloat16
VMEM_LIMIT = 56 * 1024 * 1024
NEG_BIG = -1e30
LOWEST = -3.0e38
FLT_MIN = 1.1754943508222875e-38
LOG2E = 1.4426950408889634

TM = 512
TQ_MLA = 512
TK_MLA = 512
TQ_DSA = 256
TK_DSA = 512
MLA_HEADS_PER_STEP = 4

_C_QLAT = 0
_C_KVLAT = _C_QLAT + Q_LORA
_C_QB = _C_KVLAT + KV_LORA
_C_IQ = _C_QB + N_HEADS_B * LANES
_C_KB = _C_IQ + N_IDX_HEADS * LANES
_C_VB = _C_KB + LANES
_C_IK = _C_VB + LANES
_C_KR = _C_IK + LANES
_C_IW = _C_KR + LANES
_C_END = _C_IW + LANES


def _cparams(n_axes):
    return pltpu.CompilerParams(dimension_semantics=("arbitrary",) * n_axes, vmem_limit_bytes=VMEM_LIMIT)


def _const_spec(shape):
    return pl.BlockSpec(shape, lambda *_: (0,) * len(shape), pipeline_mode=pl.Buffered(1))


def _rms(x, g):
    return (x * lax.rsqrt(jnp.mean(x * x, axis=-1, keepdims=True) + EPS)) * g


def _rope_tile(x, cos, sin, half, first_half):
    partner = jnp.where(first_half, pltpu.roll(x, LANES - half, 1), pltpu.roll(x, half, 1))
    return x * cos + partner * sin


def _proj_kernel(x_ref, cq_ref, sq_ref, c64_ref, s64_ref, gat_ref, win_ref, gq_ref, wqb_ref, gkv_ref,
                 ckv_ref, kr_ref, kb_ref, vb_ref, ik_ref, iw_ref, qm_ref, qb_ref, iq_ref,
                 kb16_ref, vb16_ref, ik16_ref):
    x = x_ref[...]
    hb = _rms(x, gat_ref[...]).astype(MXU_DTYPE)
    y = jnp.dot(hb, win_ref[...], preferred_element_type=jnp.float32)

    lane = lax.broadcasted_iota(jnp.int32, (1, LANES), 1)
    first_q = lane < NOPE_DIM + ROPE_DIM // 2
    first_64 = (lane & (HEAD_DIM_B - 1)) < ROT_DIM_B // 2
    cq, sq, c64, s64 = cq_ref[...], sq_ref[...], c64_ref[...], s64_ref[...]

    qn = _rms(y[:, _C_QLAT:_C_QLAT + Q_LORA], gq_ref[...]).astype(MXU_DTYPE)
    qa = jnp.dot(qn, wqb_ref[...], preferred_element_type=jnp.float32)
    scale_a = (NOPE_DIM + ROPE_DIM) ** -0.5 * LOG2E
    for h in range(N_HEADS_A):
        t = _rope_tile(qa[:, h * LANES:(h + 1) * LANES], cq, sq, ROPE_DIM // 2, first_q)
        qm_ref[:, h * LANES:(h + 1) * LANES] = (t * scale_a).astype(qm_ref.dtype)

    ckv_ref[...] = _rms(y[:, _C_KVLAT:_C_KVLAT + KV_LORA], gkv_ref[...])
    kr_ref[...] = _rope_tile(y[:, _C_KR:_C_KR + LANES], cq, sq, ROPE_DIM // 2, first_q)

    scale_b = HEAD_DIM_B ** -0.5 * LOG2E
    for h in range(N_HEADS_B):
        t = _rope_tile(y[:, _C_QB + h * LANES:_C_QB + (h + 1) * LANES], c64, s64, ROT_DIM_B // 2, first_64)
        qb_ref[:, h * LANES:(h + 1) * LANES] = (t * scale_b).astype(qb_ref.dtype)
    for h in range(N_IDX_HEADS):
        t = _rope_tile(y[:, _C_IQ + h * LANES:_C_IQ + (h + 1) * LANES], c64, s64, ROT_DIM_IDX // 2, first_64)
        iq_ref[:, h * LANES:(h + 1) * LANES] = t.astype(iq_ref.dtype)

    kb = _rope_tile(y[:, _C_KB:_C_KB + LANES], c64, s64, ROT_DIM_B // 2, first_64)
    kb_ref[...] = kb
    kb16_ref[...] = kb.astype(kb16_ref.dtype)
    vb = y[:, _C_VB:_C_VB + LANES]
    vb_ref[...] = vb
    vb16_ref[...] = vb.astype(vb16_ref.dtype)
    ik = _rope_tile(y[:, _C_IK:_C_IK + LANES], c64, s64, ROT_DIM_IDX // 2, first_64)
    ik_ref[...] = ik
    ik16_ref[...] = ik.astype(ik16_ref.dtype)
    iw_ref[...] = y[:, _C_IW:_C_IW + LANES] * ((N_IDX_HEADS * IDX_DIM) ** -0.5)


def _proj_call(x2d, tabs, n_tab_tiles, tm, g_attn, w_in_p, g_q, w_qb_p, g_kv):
    t_tok = x2d.shape[0]
    grid = (t_tok // tm,)
    row = lambda w: pl.BlockSpec((tm, w), lambda i: (i, 0))
    tab = pl.BlockSpec((tm, LANES), lambda i: (i % n_tab_tiles, 0))
    f32, b16 = jnp.float32, MXU_DTYPE
    out_shapes = [
        jax.ShapeDtypeStruct((t_tok, KV_LORA), f32),
        jax.ShapeDtypeStruct((t_tok, LANES), f32),
        jax.ShapeDtypeStruct((t_tok, LANES), f32),
        jax.ShapeDtypeStruct((t_tok, LANES), f32),
        jax.ShapeDtypeStruct((t_tok, LANES), f32),
        jax.ShapeDtypeStruct((t_tok, LANES), f32),
        jax.ShapeDtypeStruct((t_tok, N_HEADS_A * LANES), b16),
        jax.ShapeDtypeStruct((t_tok, N_HEADS_B * LANES), b16),
        jax.ShapeDtypeStruct((t_tok, N_IDX_HEADS * LANES), b16),
        jax.ShapeDtypeStruct((t_tok, LANES), b16),
        jax.ShapeDtypeStruct((t_tok, LANES), b16),
        jax.ShapeDtypeStruct((t_tok, LANES), b16),
    ]
    out_specs = [row(s.shape[1]) for s in out_shapes]
    in_specs = [row(D_MODEL), tab, tab, tab, tab, _const_spec((1, D_MODEL)), _const_spec(w_in_p.shape),
                _const_spec((1, Q_LORA)), _const_spec(w_qb_p.shape), _const_spec((1, KV_LORA))]
    return pl.pallas_call(
        _proj_kernel, grid=grid, in_specs=in_specs, out_specs=out_specs, out_shape=out_shapes,
        compiler_params=_cparams(1), name="proj",
    )(x2d, *tabs, g_attn, w_in_p, g_q, w_qb_p, g_kv)


_NT = (((1,), (1,)), ((), ()))


def _kvup_kernel(c_ref, kr_ref, wk_ref, wvt_ref, k_ref, vt_ref):
    cb = c_ref[...].astype(MXU_DTYPE)
    kn = jnp.dot(cb, wk_ref[...], preferred_element_type=jnp.float32)
    kr = kr_ref[...]
    for h in range(N_HEADS_A):
        k_ref[:, h * LANES:(h + 1) * LANES] = (kn[:, h * LANES:(h + 1) * LANES] + kr).astype(k_ref.dtype)
    vt_ref[...] = lax.dot_general(wvt_ref[...], cb, _NT, preferred_element_type=jnp.float32).astype(vt_ref.dtype)


def _kvup_call(c3d, kr3d, wk, wvt, tk):
    b, s_pad, _ = c3d.shape
    row = lambda w: pl.BlockSpec((None, tk, w), lambda bi, i: (bi, i, 0))
    vdim = N_HEADS_A * V_DIM_A
    return pl.pallas_call(
        _kvup_kernel, grid=(b, s_pad // tk),
        in_specs=[row(KV_LORA), row(LANES), _const_spec(wk.shape), _const_spec(wvt.shape)],
        out_specs=[row(N_HEADS_A * LANES), pl.BlockSpec((None, None, vdim, tk), lambda bi, i: (bi, i, 0, 0))],
        out_shape=[jax.ShapeDtypeStruct((b, s_pad, N_HEADS_A * LANES), MXU_DTYPE),
                   jax.ShapeDtypeStruct((b, s_pad // tk, vdim, tk), MXU_DTYPE)],
        compiler_params=_cparams(2), name="kvup",
    )(c3d, kr3d, wk, wvt)


_CHUNK_SHIFT = CHUNK.bit_length() - 1


def _admissible_t(p0, kb, tq, tk, s_len):
    kpos = kb * tk + lax.broadcasted_iota(jnp.int32, (tk, 1), 0)
    qpos = p0 + lax.broadcasted_iota(jnp.int32, (1, tq), 1)
    return ((qpos >> _CHUNK_SHIFT) >= (kpos >> _CHUNK_SHIFT)) & (kpos < s_len)


def _last_block(p0, tq, tk, s_len):
    return jnp.minimum(p0 + tq - 1, s_len - 1) >> (tk.bit_length() - 1)


def _fori_grouped(n, body, carry, group):
    shift = group.bit_length() - 1

    def several(j, c):
        for i in range(group):
            c = body(group * j + i, c)
        return c

    carry = lax.fori_loop(0, n >> shift, several, carry)
    return lax.fori_loop((n >> shift) << shift, n, body, carry)


def _softmax_init(n_heads, tq, acc_ref):
    acc_ref[...] = jnp.zeros(acc_ref.shape, jnp.float32)
    return jnp.full((n_heads, tq), NEG_BIG, jnp.float32), jnp.zeros((n_heads, tq), jnp.float32)


def _attn_step(state, values, next_scores, bias, s_ref, acc_ref):
    (m_prev, l_prev), maxima = state
    n_heads = s_ref.shape[0]
    sums, next_max = [], []
    if values is not None:
        m_new = jnp.maximum(m_prev, maxima)
        alpha = jnp.exp2(m_prev - m_new)
    for h in range(n_heads):
        if values is not None:
            p = jnp.exp2(s_ref[h] - m_new[h:h + 1, :])
            sums.append(jnp.sum(p, axis=0, keepdims=True))
            acc_ref[h] = alpha[h:h + 1, :] * acc_ref[h] + jnp.dot(values[h], p.astype(values[h].dtype),
                                                                   preferred_element_type=jnp.float32)
        if next_scores is not None:
            s = next_scores(h)
            if bias is not None:
                s = s + bias
            s_ref[h] = s
            next_max.append(jnp.max(s, axis=0, keepdims=True))
    carry = (m_new, alpha * l_prev + jnp.concatenate(sums, axis=0)) if values is not None else (m_prev, l_prev)
    return carry, (jnp.concatenate(next_max, axis=0) if next_scores is not None else maxima)


def _softmax_finish(l, acc_ref, o_ref):
    inv = 1.0 / l
    o_t = jnp.concatenate([acc_ref[h] * inv[h:h + 1, :] for h in range(l.shape[0])], axis=0)
    o_ref[...] = o_t.T.astype(o_ref.dtype)


def _mla_kernel(q_ref, k_ref, vt_ref, o_ref, s_ref, acc_ref, *, tq, tk, q_off, s_len):
    p0 = q_off + pl.program_id(2) * tq
    last = _last_block(p0, tq, tk, s_len)
    heads = range(MLA_HEADS_PER_STEP)

    def qk(kb):
        start = pl.multiple_of(kb * tk, tk)
        return lambda h: lax.dot_general(k_ref[pl.ds(start, tk), h * LANES:(h + 1) * LANES],
                                         q_ref[:, h * LANES:(h + 1) * LANES], _NT,
                                         preferred_element_type=jnp.float32)

    def mask_bias(kb):
        return jnp.where(_admissible_t(p0, kb, tq, tk, s_len), 0.0, NEG_BIG)

    def values(kb):
        vt_all = vt_ref[kb]
        return [vt_all[h * V_DIM_A:(h + 1) * V_DIM_A, :] for h in heads]

    def body(kb, state):
        return _attn_step(state, values(kb), qk(kb + 1), None, s_ref, acc_ref)

    def tail(state):
        return _attn_step(state, values(last - 1), qk(last), mask_bias(last), s_ref, acc_ref)

    state = _attn_step((_softmax_init(MLA_HEADS_PER_STEP, tq, acc_ref), None), None, qk(0), mask_bias(0), s_ref,
                       acc_ref)
    state = lax.fori_loop(0, jnp.maximum(last - 1, 0), body, state)
    state = lax.cond(last > 0, tail, lambda st: st, state)
    (_, l), _ = _attn_step(state, values(last), None, None, s_ref, acc_ref)
    _softmax_finish(l, acc_ref, o_ref)


def _mla_call(q, k, vt, *, tq, tk, q_off, s_len):
    b, lq, _ = q.shape
    s_pad = k.shape[1]
    hg = MLA_HEADS_PER_STEP
    kern = functools.partial(_mla_kernel, tq=tq, tk=tk, q_off=q_off, s_len=s_len)
    once = dict(pipeline_mode=pl.Buffered(1))
    return pl.pallas_call(
        kern, grid=(b, N_HEADS_A // hg, lq // tq),
        in_specs=[pl.BlockSpec((None, tq, hg * LANES), lambda bi, g, qi: (bi, qi, g)),
                  pl.BlockSpec((None, s_pad, hg * LANES), lambda bi, g, qi: (bi, 0, g), **once),
                  pl.BlockSpec((None, s_pad // tk, hg * V_DIM_A, tk), lambda bi, g, qi: (bi, 0, g, 0), **once)],
        out_specs=pl.BlockSpec((None, tq, hg * V_DIM_A), lambda bi, g, qi: (bi, qi, g)),
        out_shape=jax.ShapeDtypeStruct((b, lq, N_HEADS_A * V_DIM_A), MXU_DTYPE),
        scratch_shapes=[pltpu.VMEM((hg, tk, tq), jnp.float32), pltpu.VMEM((hg, V_DIM_A, tq), jnp.float32)],
        compiler_params=_cparams(3), name="mla",
    )(q, k, vt)


def _f2key(x):
    b = lax.bitcast_convert_type(x, jnp.int32)
    return jnp.where(b < 0, b ^ jnp.int32(0x7FFFFFFF), b)


def _key2f(k):
    return lax.bitcast_convert_type(jnp.where(k < 0, k ^ jnp.int32(0x7FFFFFFF), k), jnp.float32)


_COUNT_SLAB = 4 * SUBLANES
_SCORE_GROUP = 4
_COUNT_GROUP = 4
_SEARCH_BISECT_EVERY = 8
_SEARCH_MAX_PROBES = 40 * _SEARCH_BISECT_EVERY


def _dsa_kernel(iq_ref, qb_ref, iw_ref, ik_ref, kb_ref, vbt_ref, tri_ref, o_ref,
                sc_ref, wt_ref, s_ref, acc_ref, *, tq, tk, q_off, s_len, topk):
    p0 = q_off + pl.program_id(1) * tq
    last = _last_block(p0, tq, tk, s_len)
    n_blk = last + 1
    row = lambda v: jnp.full((1, tq), v, jnp.float32)
    colsum = lambda x: jnp.sum(x, axis=0, keepdims=True)

    wt_ref[...] = iw_ref[...].T[:SUBLANES, :]

    def score_block(kb):
        start = pl.multiple_of(kb * tk, tk)
        ikb = ik_ref[pl.ds(start, tk), :]
        acc = jnp.zeros((tk, tq), jnp.float32)
        for h in range(N_IDX_HEADS):
            x = lax.dot_general(ikb, iq_ref[:, h * LANES:(h + 1) * LANES], _NT, preferred_element_type=jnp.float32)
            acc = acc + wt_ref[h:h + 1, :] * jnp.maximum(x, 0.0)
        return acc

    def stats(carry, hi_fill, lo_fill):
        mx, mn, n_pos, n_nonneg = carry
        return (jnp.maximum(mx, jnp.max(hi_fill, axis=0, keepdims=True)),
                jnp.minimum(mn, jnp.min(lo_fill, axis=0, keepdims=True)),
                n_pos + colsum(jnp.where(hi_fill >= FLT_MIN, 1.0, 0.0)),
                n_nonneg + colsum(jnp.where(hi_fill >= 0.0, 1.0, 0.0)))

    def p1_body(kb, carry):
        acc = score_block(kb)
        sc_ref[kb] = acc
        return stats(carry, acc, acc)

    carry = _fori_grouped(last, p1_body, (row(-jnp.inf), row(jnp.inf), row(0.0), row(0.0)), _SCORE_GROUP)
    acc = score_block(last)
    adm = _admissible_t(p0, last, tq, tk, s_len)
    masked = jnp.where(adm, acc, -jnp.inf)
    sc_ref[last] = masked
    mx, mn, n_pos, n_nonneg = stats(carry, masked, jnp.where(adm, acc, jnp.inf))

    qpos = p0 + lax.broadcasted_iota(jnp.int32, (1, tq), 1)
    n_adm = jnp.minimum(((qpos >> _CHUNK_SHIFT) + 1) * CHUNK, s_len).astype(jnp.float32)
    k_f = jnp.float32(topk)
    log_k = jnp.log(k_f)
    phi = lambda c: jnp.log(jnp.maximum(c, 0.25)) - log_k

    few = n_adm <= k_f
    pos = n_pos >= k_f
    at_zero = jnp.logical_and(~pos, n_nonneg >= k_f)
    above_max = _key2f(_f2key(jnp.maximum(mx, LOWEST)) + 1)
    lo = jnp.where(pos, FLT_MIN, LOWEST)
    c_lo = jnp.where(pos, n_pos, n_adm)
    hi = jnp.where(pos | at_zero, above_max, 0.0)
    c_hi = jnp.where(pos | at_zero, 0.0, n_nonneg)
    lo_val = jnp.where(pos, 0.0, jnp.maximum(mn, LOWEST))
    tau = jnp.where(few, LOWEST, jnp.where(at_zero, 0.0, lo))
    r = jnp.where(at_zero & ~few, k_f - n_pos, 0.0)
    done = jnp.where(few | at_zero, 1.0, 0.0)

    def count_ge(g):
        def body(kb, cnt):
            hit = jnp.where(sc_ref[kb] >= g, 1.0, 0.0)
            return cnt + jnp.sum(hit.reshape(tk // _COUNT_SLAB, _COUNT_SLAB, tq), axis=0)

        return colsum(_fori_grouped(n_blk, body, jnp.zeros((_COUNT_SLAB, tq), jnp.float32), _COUNT_GROUP))

    def search_cond(st):
        return jnp.logical_and(st[1] > 0, st[0] < _SEARCH_MAX_PROBES)

    def search_body(st):
        it, _, lo, hi, c_lo, c_hi, f_lo, f_hi, lo_val, side, tau, r, done = st
        lok, hik = _f2key(lo), _f2key(hi)
        midk = (lok >> 1) + (hik >> 1) + (lok & hik & 1)
        frac = f_lo / jnp.maximum(f_lo - f_hi, 1e-9)
        g_int = lo_val + (hi - lo_val) * frac
        bisect = (it % _SEARCH_BISECT_EVERY) == (_SEARCH_BISECT_EVERY - 1)
        g = jnp.where(bisect, _key2f(midk), g_int)
        gk = jnp.minimum(jnp.maximum(_f2key(g), lok + 1), jnp.maximum(hik - 1, lok + 1))
        g = _key2f(gk)
        c = count_ge(g)
        f_g = phi(c)
        live = done == 0.0
        up = live & (c >= k_f)
        dn = live & (c < k_f)
        f_hi = jnp.where(up & (side > 0.0), f_hi * 0.5, f_hi)
        f_lo = jnp.where(dn & (side < 0.0), f_lo * 0.5, f_lo)
        lo, c_lo, f_lo, lo_val = (jnp.where(up, g, lo), jnp.where(up, c, c_lo), jnp.where(up, f_g, f_lo),
                                  jnp.where(up, g, lo_val))
        hi, c_hi, f_hi = jnp.where(dn, g, hi), jnp.where(dn, c, c_hi), jnp.where(dn, f_g, f_hi)
        side = jnp.where(up, 1.0, jnp.where(dn, -1.0, side))
        adjacent = (_f2key(lo) + 1) >= _f2key(hi)
        fin = live & ((c_lo == k_f) | adjacent)
        tau = jnp.where(fin, lo, tau)
        r = jnp.where(fin & (c_lo > k_f), k_f - c_hi, r)
        done = jnp.where(fin, 1.0, done)
        pending = jnp.sum(1.0 - done).astype(jnp.int32)
        return (it + 1, pending, lo, hi, c_lo, c_hi, f_lo, f_hi, lo_val, side, tau, r, done)

    pending0 = jnp.sum(1.0 - done).astype(jnp.int32)
    st = lax.while_loop(search_cond, search_body,
                        (jnp.int32(0), pending0, lo, hi, c_lo, c_hi, phi(c_lo), phi(c_hi), lo_val, row(0.0), tau, r,
                         done))
    tau, r = st[10], st[11]

    @pl.when(jnp.sum(r) > 0.0)
    def _():
        tied = r > 0.0
        budget = jnp.where(tied, r, jnp.inf)

        def body(kb, seen):
            blk = sc_ref[kb]
            eq = jnp.logical_and(blk == tau, tied)
            eq_f = jnp.where(eq, 1.0, 0.0)
            rank = jnp.dot(tri_ref[...], eq_f.astype(MXU_DTYPE), preferred_element_type=jnp.float32) + seen
            sc_ref[kb] = jnp.where(jnp.logical_and(eq, rank > budget), -jnp.inf, blk)
            return seen + colsum(eq_f)

        lax.fori_loop(0, n_blk, body, row(0.0))

    heads_per_kv = N_HEADS_B // N_KV_B

    def qk(kb):
        start = pl.multiple_of(kb * tk, tk)
        kblk = kb_ref[pl.ds(start, tk), :]
        return lambda h: lax.dot_general(kblk, qb_ref[:, h * LANES:(h + 1) * LANES], _NT,
                                         preferred_element_type=jnp.float32)

    def sel_bias(kb):
        return jnp.where(sc_ref[kb] >= tau, 0.0, NEG_BIG)

    def values(kb):
        vt_all = vbt_ref[kb]
        return [vt_all[(h // heads_per_kv) * HEAD_DIM_B:(h // heads_per_kv + 1) * HEAD_DIM_B, :]
                for h in range(N_HEADS_B)]

    def p3_body(kb, state):
        return _attn_step(state, values(kb), qk(kb + 1), sel_bias(kb + 1), s_ref, acc_ref)

    state = _attn_step((_softmax_init(N_HEADS_B, tq, acc_ref), None), None, qk(0), sel_bias(0), s_ref, acc_ref)
    state = lax.fori_loop(0, last, p3_body, state)
    (_, l), _ = _attn_step(state, values(last), None, None, s_ref, acc_ref)
    _softmax_finish(l, acc_ref, o_ref)


def _dsa_call(iq, qb, iw, ik, kb, vbt, tri, *, tq, tk, q_off, s_len):
    b, lq, _ = iq.shape
    s_pad = ik.shape[1]
    n_kb = s_pad // tk
    topk = min(TOPK_MAX, s_len // 4)
    kern = functools.partial(_dsa_kernel, tq=tq, tk=tk, q_off=q_off, s_len=s_len, topk=topk)
    qspec = lambda w: pl.BlockSpec((None, tq, w), lambda bi, qi: (bi, qi, 0))
    once = dict(pipeline_mode=pl.Buffered(1))
    kspec = pl.BlockSpec((None, s_pad, LANES), lambda bi, qi: (bi, 0, 0), **once)
    return pl.pallas_call(
        kern, grid=(b, lq // tq),
        in_specs=[qspec(N_IDX_HEADS * LANES), qspec(N_HEADS_B * LANES), qspec(LANES), kspec, kspec,
                  pl.BlockSpec((None, n_kb, LANES, tk), lambda bi, qi: (bi, 0, 0, 0), **once),
                  _const_spec(tri.shape)],
        out_specs=qspec(N_HEADS_B * HEAD_DIM_B),
        out_shape=jax.ShapeDtypeStruct((b, lq, N_HEADS_B * HEAD_DIM_B), MXU_DTYPE),
        scratch_shapes=[pltpu.VMEM((n_kb, tk, tq), jnp.float32), pltpu.VMEM((SUBLANES, tq), jnp.float32),
                        pltpu.VMEM((N_HEADS_B, tk, tq), jnp.float32),
                        pltpu.VMEM((N_HEADS_B, HEAD_DIM_B, tq), jnp.float32)],
        compiler_params=_cparams(2), name="dsa",
    )(iq, qb, iw, ik, kb, vbt, tri)


_FF_CHUNK = 256


def _out_kernel(x_ref, oa_ref, ob_ref, woa_ref, wob_ref, gf_ref, wg_ref, wu_ref, wd_ref, gfin_ref, y_ref):
    o = (jnp.dot(oa_ref[...], woa_ref[...], preferred_element_type=jnp.float32)
         + jnp.dot(ob_ref[...], wob_ref[...], preferred_element_type=jnp.float32))
    x1 = x_ref[...] + o
    h2 = _rms(x1, gf_ref[...]).astype(MXU_DTYPE)
    ffn = jnp.zeros_like(x1)
    for j in range(D_FF // _FF_CHUNK):
        sl = slice(j * _FF_CHUNK, (j + 1) * _FF_CHUNK)
        gate = jnp.dot(h2, wg_ref[:, sl], preferred_element_type=jnp.float32)
        up = jnp.dot(h2, wu_ref[:, sl], preferred_element_type=jnp.float32)
        act = (gate * jax.nn.sigmoid(gate)) * up
        ffn = ffn + jnp.dot(act.astype(MXU_DTYPE), wd_ref[sl, :], preferred_element_type=jnp.float32)
    y_ref[...] = _rms(x1 + ffn, gfin_ref[...])


def _out_call(x2d, oa, ob, woa, wob, g_ffn, wg, wu, wd, g_fin, tm):
    t_tok = x2d.shape[0]
    row = lambda w: pl.BlockSpec((tm, w), lambda i: (i, 0))
    half = N_HEADS_A * V_DIM_A
    return pl.pallas_call(
        _out_kernel, grid=(t_tok // tm,),
        in_specs=[row(D_MODEL), row(half), row(half), _const_spec(woa.shape), _const_spec(wob.shape),
                  _const_spec((1, D_MODEL)), _const_spec(wg.shape), _const_spec(wu.shape), _const_spec(wd.shape),
                  _const_spec((1, D_MODEL))],
        out_specs=row(D_MODEL), out_shape=jax.ShapeDtypeStruct((t_tok, D_MODEL), jnp.float32),
        compiler_params=_cparams(1), name="out",
    )(x2d, oa, ob, woa, wob, g_ffn, wg, wu, wd, g_fin)


def _prep_weights(w_in, w_q_b, w_kv_b, w_o, w_gate, w_up, w_down):
    off = np.concatenate([[0], np.cumsum(IN_SIZES)])
    o_qlat, o_kvlat, o_kr, o_qb, o_kb, o_vb, o_iq, o_ik, o_iw = off[:9]
    zeros = lambda n: jnp.zeros((D_MODEL, n), w_in.dtype)
    cols = [w_in[:, o_qlat:o_qlat + Q_LORA], w_in[:, o_kvlat:o_kvlat + KV_LORA]]
    for h in range(N_HEADS_B):
        head = w_in[:, o_qb + h * HEAD_DIM_B:o_qb + (h + 1) * HEAD_DIM_B]
        cols += [head, zeros(HEAD_DIM_B)] if h < N_HEADS_B // N_KV_B else [zeros(HEAD_DIM_B), head]
    for h in range(N_IDX_HEADS):
        cols += [w_in[:, o_iq + h * IDX_DIM:o_iq + (h + 1) * IDX_DIM], zeros(LANES - IDX_DIM)]
    cols += [w_in[:, o_kb:o_kb + LANES], w_in[:, o_vb:o_vb + LANES],
             w_in[:, o_ik:o_ik + IDX_DIM], zeros(LANES - IDX_DIM),
             zeros(NOPE_DIM), w_in[:, o_kr:o_kr + ROPE_DIM], zeros(LANES - NOPE_DIM - ROPE_DIM),
             w_in[:, o_iw:o_iw + N_IDX_HEADS], zeros(LANES - N_IDX_HEADS)]
    w_in_p = jnp.concatenate(cols, axis=1).astype(MXU_DTYPE)
    assert w_in_p.shape[1] == _C_END

    qd = NOPE_DIM + ROPE_DIM
    w_qb_p = jnp.pad(w_q_b.reshape(Q_LORA, N_HEADS_A, qd), ((0, 0), (0, 0), (0, LANES - qd)))
    w_qb_p = w_qb_p.reshape(Q_LORA, N_HEADS_A * LANES).astype(MXU_DTYPE)

    kv = w_kv_b.reshape(KV_LORA, N_HEADS_A, NOPE_DIM + V_DIM_A)
    wk = jnp.pad(kv[:, :, :NOPE_DIM], ((0, 0), (0, 0), (0, LANES - NOPE_DIM)))
    wk = wk.reshape(KV_LORA, N_HEADS_A * LANES).astype(MXU_DTYPE)
    wvt = kv[:, :, NOPE_DIM:].reshape(KV_LORA, N_HEADS_A * V_DIM_A).T.astype(MXU_DTYPE)

    half = N_HEADS_A * V_DIM_A
    return (w_in_p, w_qb_p, wk, wvt, w_o[:half].astype(MXU_DTYPE), w_o[half:].astype(MXU_DTYPE),
            w_gate.astype(MXU_DTYPE), w_up.astype(MXU_DTYPE), w_down.astype(MXU_DTYPE))


def _rope_tables(pos):
    posf = pos.astype(jnp.float32)[:, None]
    n = pos.shape[0]

    def cs(d):
        inv = jnp.power(ROPE_THETA, -jnp.arange(0, d, 2, dtype=jnp.float32) / d)
        ang = posf * inv[None, :]
        return jnp.cos(ang), jnp.sin(ang)

    one, zero = (lambda w: jnp.ones((n, w), jnp.float32)), (lambda w: jnp.zeros((n, w), jnp.float32))
    c, s = cs(ROPE_DIM)
    cq = jnp.concatenate([one(NOPE_DIM), c, c, one(LANES - NOPE_DIM - ROPE_DIM)], axis=1)
    sq = jnp.concatenate([zero(NOPE_DIM), -s, s, zero(LANES - NOPE_DIM - ROPE_DIM)], axis=1)
    c, s = cs(ROT_DIM_B)
    c64 = jnp.tile(jnp.concatenate([c, c, one(HEAD_DIM_B - ROT_DIM_B)], axis=1), (1, LANES // HEAD_DIM_B))
    s64 = jnp.tile(jnp.concatenate([-s, s, zero(HEAD_DIM_B - ROT_DIM_B)], axis=1), (1, LANES // HEAD_DIM_B))
    return cq, sq, c64, s64


def _pick_tile(n, pref):
    t = min(pref, n)
    while n % t:
        t //= 2
    return t


def _round_up(n, m):
    return -(-n // m) * m


def _layer(x, pos0, past, weights, norms, final_norm):
    b, l, _ = x.shape
    g_attn, g_q, g_kv, g_ffn = norms
    w_in_p, w_qb_p, wk, wvt, woa, wob, wg, wu, wd = weights
    t_tok = b * l
    x2d = x.reshape(t_tok, D_MODEL)

    tabs = _rope_tables(pos0 + jnp.arange(l))
    if l >= TM // 2:
        tm = _pick_tile(l, TM)
        n_tab_tiles = l // tm
    else:
        tm = _pick_tile(t_tok, TM)
        tabs = tuple(jnp.tile(t, (tm // l, 1)) for t in tabs)
        n_tab_tiles = 1
    (ckv, kr128, kb, vb, ik128, iw, q_mla, q_b, iq, kb16, vb16, ik16) = _proj_call(
        x2d, tabs, n_tab_tiles, tm, g_attn, w_in_p, g_q, w_qb_p, g_kv)

    new = (ckv.reshape(b, l, KV_LORA), kr128.reshape(b, l, LANES)[..., NOPE_DIM:NOPE_DIM + ROPE_DIM],
           kb.reshape(b, l, N_KV_B, HEAD_DIM_B), vb.reshape(b, l, N_KV_B, HEAD_DIM_B),
           ik128.reshape(b, l, LANES)[..., :IDX_DIM])

    per_row = lambda a: a.reshape(b, l, a.shape[-1])
    p_len = 0 if past is None else past[0].shape[1]
    s_len = p_len + l
    s_pad = _round_up(s_len, max(TK_MLA, TK_DSA))
    if past is None:
        old = [None] * 5
    else:
        p_lat, p_kr, p_k, p_v, p_ik = past
        old = [p_lat,
               jnp.pad(p_kr, ((0, 0), (0, 0), (NOPE_DIM, LANES - NOPE_DIM - ROPE_DIM))),
               p_k.reshape(b, p_len, LANES).astype(MXU_DTYPE), p_v.reshape(b, p_len, LANES).astype(MXU_DTYPE),
               jnp.pad(p_ik, ((0, 0), (0, 0), (0, LANES - IDX_DIM))).astype(MXU_DTYPE)]

    def full_seq(prev, cur):
        parts = ([] if prev is None else [prev]) + [per_row(cur)]
        if s_pad > s_len:
            parts.append(jnp.zeros((b, s_pad - s_len, cur.shape[-1]), cur.dtype))
        return parts[0] if len(parts) == 1 else jnp.concatenate(parts, axis=1)

    c_all, kr_all, kb_all, vb_all, ik_all = (full_seq(o, c) for o, c in zip(old, (ckv, kr128, kb16, vb16, ik16)))
    vbt_all = vb_all.reshape(b, s_pad // TK_DSA, TK_DSA, LANES).transpose(0, 1, 3, 2)

    k_mla, vt_mla = _kvup_call(c_all, kr_all, wk, wvt, TK_MLA)

    lq = _round_up(l, LANES)
    padq = lambda a: per_row(a) if lq == l else jnp.pad(per_row(a), ((0, 0), (0, lq - l), (0, 0)))
    o_a = _mla_call(padq(q_mla), k_mla, vt_mla, tq=_pick_tile(lq, TQ_MLA), tk=TK_MLA, q_off=pos0, s_len=s_len)
    tri = jnp.tril(jnp.ones((TK_DSA, TK_DSA), MXU_DTYPE))
    o_b = _dsa_call(padq(iq), padq(q_b), padq(iw), ik_all, kb_all, vbt_all, tri,
                    tq=_pick_tile(lq, TQ_DSA), tk=TK_DSA, q_off=pos0, s_len=s_len)
    o_a, o_b = (o[:, :l].reshape(t_tok, -1) for o in (o_a, o_b))

    y = _out_call(x2d, o_a, o_b, woa, wob, g_ffn, wg, wu, wd, final_norm, tm)
    return y.reshape(b, l, D_MODEL), new


def kernel(x_prompt, x_sample, cache_mla_latent, cache_mla_krope, cache_dsa_k, cache_dsa_v, cache_idx_k, attn_norm, w_in, q_a_norm, w_q_b, kv_a_norm, w_kv_b, w_o, ffn_norm, w_gate, w_up, w_down, final_norm):
    depth = w_in.shape[0]
    assert depth == 1, "the fused final norm assumes a single layer"
    past_len = cache_mla_latent.shape[2]
    li = 0
    weights = _prep_weights(w_in[li], w_q_b[li], w_kv_b[li], w_o[li], w_gate[li], w_up[li], w_down[li])
    norms = (attn_norm[li][None, :], q_a_norm[li][None, :], kv_a_norm[li][None, :], ffn_norm[li][None, :])
    fin = final_norm[None, :]
    past = (cache_mla_latent[li], cache_mla_krope[li], cache_dsa_k[li], cache_dsa_v[li], cache_idx_k[li])
    y_p, new_p = _layer(x_prompt, 0, None, weights, norms, fin)
    y_s, new_s = _layer(x_sample, past_len, past, weights, norms, fin)
    return (y_p, y_s) + tuple(a[None] for a in new_p) + tuple(a[None] for a in new_s)
```

```python
import functools

import numpy as np
import jax
import jax.numpy as jnp
from jax import lax
from jax.experimental import pallas as pl
from jax.experimental.pallas import tpu as pltpu

D_MODEL = 1024
CHUNK = 64
ROPE_THETA = 500000.0
EPS = 1e-6
N_HEADS_A = 8
NOPE_DIM = 64
ROPE_DIM = 32
V_DIM_A = 64
Q_LORA = 384
KV_LORA = 256
N_HEADS_B = 8
N_KV_B = 2
HEAD_DIM_B = 64
ROT_DIM_B = 16
N_IDX_HEADS = 8
IDX_DIM = 64
ROT_DIM_IDX = 16
TOPK_MAX = 256
D_FF = 2816
IN_SIZES = (Q_LORA, KV_LORA, ROPE_DIM, N_HEADS_B * HEAD_DIM_B, N_KV_B * HEAD_DIM_B, N_KV_B * HEAD_DIM_B,
            N_IDX_HEADS * IDX_DIM, IDX_DIM, N_IDX_HEADS)

LANES = 128
SUBLANES = 8
MXU_DTYPE = jnp.bfloat16
VMEM_LIMIT = 56 * 1024 * 1024
NEG_BIG = -1e30
LOWEST = -3.0e38
FLT_MIN = 1.1754943508222875e-38
LOG2E = 1.4426950408889634

TM = 512
TQ_MLA = 512
TK_MLA = 512
TQ_DSA = 256
TK_DSA = 512
MLA_HEADS_PER_STEP = 4

_C_QLAT = 0
_C_KVLAT = _C_QLAT + Q_LORA
_C_QB = _C_KVLAT + KV_LORA
_C_IQ = _C_QB + N_HEADS_B * LANES
_C_KB = _C_IQ + N_IDX_HEADS * LANES
_C_VB = _C_KB + LANES
_C_IK = _C_VB + LANES
_C_KR = _C_IK + LANES
_C_IW = _C_KR + LANES
_C_END = _C_IW + LANES


def _cparams(n_axes):
    return pltpu.CompilerParams(dimension_semantics=("arbitrary",) * n_axes, vmem_limit_bytes=VMEM_LIMIT)


def _const_spec(shape):
    return pl.BlockSpec(shape, lambda *_: (0,) * len(shape), pipeline_mode=pl.Buffered(1))


def _rms(x, g):
    return (x * lax.rsqrt(jnp.mean(x * x, axis=-1, keepdims=True) + EPS)) * g


def _rope_tile(x, cos, sin, half, first_half):
    partner = jnp.where(first_half, pltpu.roll(x, LANES - half, 1), pltpu.roll(x, half, 1))
    return x * cos + partner * sin


def _proj_kernel(x_ref, cq_ref, sq_ref, c64_ref, s64_ref, gat_ref, win_ref, gq_ref, wqb_ref, gkv_ref,
                 ckv_ref, kr_ref, kb_ref, vb_ref, ik_ref, iw_ref, qm_ref, qb_ref, iq_ref,
                 kb16_ref, vb16_ref, ik16_ref):
    x = x_ref[...]
    hb = _rms(x, gat_ref[...]).astype(MXU_DTYPE)
    y = jnp.dot(hb, win_ref[...], preferred_element_type=jnp.float32)

    lane = lax.broadcasted_iota(jnp.int32, (1, LANES), 1)
    first_q = lane < NOPE_DIM + ROPE_DIM // 2
    first_64 = (lane & (HEAD_DIM_B - 1)) < ROT_DIM_B // 2
    cq, sq, c64, s64 = cq_ref[...], sq_ref[...], c64_ref[...], s64_ref[...]

    qn = _rms(y[:, _C_QLAT:_C_QLAT + Q_LORA], gq_ref[...]).astype(MXU_DTYPE)
    qa = jnp.dot(qn, wqb_ref[...], preferred_element_type=jnp.float32)
    scale_a = (NOPE_DIM + ROPE_DIM) ** -0.5 * LOG2E
    for h in range(N_HEADS_A):
        t = _rope_tile(qa[:, h * LANES:(h + 1) * LANES], cq, sq, ROPE_DIM // 2, first_q)
        qm_ref[:, h * LANES:(h + 1) * LANES] = (t * scale_a).astype(qm_ref.dtype)

    ckv_ref[...] = _rms(y[:, _C_KVLAT:_C_KVLAT + KV_LORA], gkv_ref[...])
    kr_ref[...] = _rope_tile(y[:, _C_KR:_C_KR + LANES], cq, sq, ROPE_DIM // 2, first_q)

    scale_b = HEAD_DIM_B ** -0.5 * LOG2E
    for h in range(N_HEADS_B):
        t = _rope_tile(y[:, _C_QB + h * LANES:_C_QB + (h + 1) * LANES], c64, s64, ROT_DIM_B // 2, first_64)
        qb_ref[:, h * LANES:(h + 1) * LANES] = (t * scale_b).astype(qb_ref.dtype)
    for h in range(N_IDX_HEADS):
        t = _rope_tile(y[:, _C_IQ + h * LANES:_C_IQ + (h + 1) * LANES], c64, s64, ROT_DIM_IDX // 2, first_64)
        iq_ref[:, h * LANES:(h + 1) * LANES] = t.astype(iq_ref.dtype)

    kb = _rope_tile(y[:, _C_KB:_C_KB + LANES], c64, s64, ROT_DIM_B // 2, first_64)
    kb_ref[...] = kb
    kb16_ref[...] = kb.astype(kb16_ref.dtype)
    vb = y[:, _C_VB:_C_VB + LANES]
    vb_ref[...] = vb
    vb16_ref[...] = vb.astype(vb16_ref.dtype)
    ik = _rope_tile(y[:, _C_IK:_C_IK + LANES], c64, s64, ROT_DIM_IDX // 2, first_64)
    ik_ref[...] = ik
    ik16_ref[...] = ik.astype(ik16_ref.dtype)
    iw_ref[...] = y[:, _C_IW:_C_IW + LANES] * ((N_IDX_HEADS * IDX_DIM) ** -0.5)


def _proj_call(x2d, tabs, n_tab_tiles, tm, g_attn, w_in_p, g_q, w_qb_p, g_kv):
    t_tok = x2d.shape[0]
    grid = (t_tok // tm,)
    row = lambda w: pl.BlockSpec((tm, w), lambda i: (i, 0))
    tab = pl.BlockSpec((tm, LANES), lambda i: (i % n_tab_tiles, 0))
    f32, b16 = jnp.float32, MXU_DTYPE
    out_shapes = [
        jax.ShapeDtypeStruct((t_tok, KV_LORA), f32),
        jax.ShapeDtypeStruct((t_tok, LANES), f32),
        jax.ShapeDtypeStruct((t_tok, LANES), f32),
        jax.ShapeDtypeStruct((t_tok, LANES), f32),
        jax.ShapeDtypeStruct((t_tok, LANES), f32),
        jax.ShapeDtypeStruct((t_tok, LANES), f32),
        jax.ShapeDtypeStruct((t_tok, N_HEADS_A * LANES), b16),
        jax.ShapeDtypeStruct((t_tok, N_HEADS_B * LANES), b16),
        jax.ShapeDtypeStruct((t_tok, N_IDX_HEADS * LANES), b16),
        jax.ShapeDtypeStruct((t_tok, LANES), b16),
        jax.ShapeDtypeStruct((t_tok, LANES), b16),
        jax.ShapeDtypeStruct((t_tok, LANES), b16),
    ]
    out_specs = [row(s.shape[1]) for s in out_shapes]
    in_specs = [row(D_MODEL), tab, tab, tab, tab, _const_spec((1, D_MODEL)), _const_spec(w_in_p.shape),
                _const_spec((1, Q_LORA)), _const_spec(w_qb_p.shape), _const_spec((1, KV_LORA))]
    return pl.pallas_call(
        _proj_kernel, grid=grid, in_specs=in_specs, out_specs=out_specs, out_shape=out_shapes,
        compiler_params=_cparams(1), name="proj",
    )(x2d, *tabs, g_attn, w_in_p, g_q, w_qb_p, g_kv)


_NT = (((1,), (1,)), ((), ()))


def _kvup_kernel(c_ref, kr_ref, wk_ref, wvt_ref, k_ref, vt_ref):
    cb = c_ref[...].astype(MXU_DTYPE)
    kn = jnp.dot(cb, wk_ref[...], preferred_element_type=jnp.float32)
    kr = kr_ref[...]
    for h in range(N_HEADS_A):
        k_ref[:, h * LANES:(h + 1) * LANES] = (kn[:, h * LANES:(h + 1) * LANES] + kr).astype(k_ref.dtype)
    vt_ref[...] = lax.dot_general(wvt_ref[...], cb, _NT, preferred_element_type=jnp.float32).astype(vt_ref.dtype)


def _kvup_call(c3d, kr3d, wk, wvt, tk):
    b, s_pad, _ = c3d.shape
    row = lambda w: pl.BlockSpec((None, tk, w), lambda bi, i: (bi, i, 0))
    vdim = N_HEADS_A * V_DIM_A
    return pl.pallas_call(
        _kvup_kernel, grid=(b, s_pad // tk),
        in_specs=[row(KV_LORA), row(LANES), _const_spec(wk.shape), _const_spec(wvt.shape)],
        out_specs=[row(N_HEADS_A * LANES), pl.BlockSpec((None, None, vdim, tk), lambda bi, i: (bi, i, 0, 0))],
        out_shape=[jax.ShapeDtypeStruct((b, s_pad, N_HEADS_A * LANES), MXU_DTYPE),
                   jax.ShapeDtypeStruct((b, s_pad // tk, vdim, tk), MXU_DTYPE)],
        compiler_params=_cparams(2), name="kvup",
    )(c3d, kr3d, wk, wvt)


_CHUNK_SHIFT = CHUNK.bit_length() - 1


def _admissible_t(p0, kb, tq, tk, s_len):
    kpos = kb * tk + lax.broadcasted_iota(jnp.int32, (tk, 1), 0)
    qpos = p0 + lax.broadcasted_iota(jnp.int32, (1, tq), 1)
    return ((qpos >> _CHUNK_SHIFT) >= (kpos >> _CHUNK_SHIFT)) & (kpos < s_len)


def _last_block(p0, tq, tk, s_len):
    return jnp.minimum(p0 + tq - 1, s_len - 1) >> (tk.bit_length() - 1)


def _fori_grouped(n, body, carry, group):
    shift = group.bit_length() - 1

    def several(j, c):
        for i in range(group):
            c = body(group * j + i, c)
        return c

    carry = lax.fori_loop(0, n >> shift, several, carry)
    return lax.fori_loop((n >> shift) << shift, n, body, carry)


def _softmax_init(n_heads, tq, acc_ref):
    acc_ref[...] = jnp.zeros(acc_ref.shape, jnp.float32)
    return jnp.full((n_heads, tq), NEG_BIG, jnp.float32), jnp.zeros((n_heads, tq), jnp.float32)


def _attn_step(state, values, next_scores, bias, s_ref, acc_ref):
    (m_prev, l_prev), maxima = state
    n_heads = s_ref.shape[0]
    sums, next_max = [], []
    if values is not None:
        m_new = jnp.maximum(m_prev, maxima)
        alpha = jnp.exp2(m_prev - m_new)
    for h in range(n_heads):
        if values is not None:
            p = jnp.exp2(s_ref[h] - m_new[h:h + 1, :])
            sums.append(jnp.sum(p, axis=0, keepdims=True))
            acc_ref[h] = alpha[h:h + 1, :] * acc_ref[h] + jnp.dot(values[h], p.astype(values[h].dtype),
                                                                   preferred_element_type=jnp.float32)
        if next_scores is not None:
            s = next_scores(h)
            if bias is not None:
                s = s + bias
            s_ref[h] = s
            next_max.append(jnp.max(s, axis=0, keepdims=True))
    carry = (m_new, alpha * l_prev + jnp.concatenate(sums, axis=0)) if values is not None else (m_prev, l_prev)
    return carry, (jnp.concatenate(next_max, axis=0) if next_scores is not None else maxima)


def _softmax_finish(l, acc_ref, o_ref):
    inv = 1.0 / l
    o_t = jnp.concatenate([acc_ref[h] * inv[h:h + 1, :] for h in range(l.shape[0])], axis=0)
    o_ref[...] = o_t.T.astype(o_ref.dtype)


def _mla_kernel(q_ref, k_ref, vt_ref, o_ref, s_ref, acc_ref, *, tq, tk, q_off, s_len):
    p0 = q_off + pl.program_id(2) * tq
    last = _last_block(p0, tq, tk, s_len)
    heads = range(MLA_HEADS_PER_STEP)

    def qk(kb):
        start = pl.multiple_of(kb * tk, tk)
        return lambda h: lax.dot_general(k_ref[pl.ds(start, tk), h * LANES:(h + 1) * LANES],
                                         q_ref[:, h * LANES:(h + 1) * LANES], _NT,
                                         preferred_element_type=jnp.float32)

    def mask_bias(kb):
        return jnp.where(_admissible_t(p0, kb, tq, tk, s_len), 0.0, NEG_BIG)

    def values(kb):
        vt_all = vt_ref[kb]
        return [vt_all[h * V_DIM_A:(h + 1) * V_DIM_A, :] for h in heads]

    def body(kb, state):
        return _attn_step(state, values(kb), qk(kb + 1), None, s_ref, acc_ref)

    def tail(state):
        return _attn_step(state, values(last - 1), qk(last), mask_bias(last), s_ref, acc_ref)

    state = _attn_step((_softmax_init(MLA_HEADS_PER_STEP, tq, acc_ref), None), None, qk(0), mask_bias(0), s_ref,
                       acc_ref)
    state = lax.fori_loop(0, jnp.maximum(last - 1, 0), body, state)
    state = lax.cond(last > 0, tail, lambda st: st, state)
    (_, l), _ = _attn_step(state, values(last), None, None, s_ref, acc_ref)
    _softmax_finish(l, acc_ref, o_ref)


def _mla_call(q, k, vt, *, tq, tk, q_off, s_len):
    b, lq, _ = q.shape
    s_pad = k.shape[1]
    hg = MLA_HEADS_PER_STEP
    kern = functools.partial(_mla_kernel, tq=tq, tk=tk, q_off=q_off, s_len=s_len)
    once = dict(pipeline_mode=pl.Buffered(1))
    return pl.pallas_call(
        kern, grid=(b, N_HEADS_A // hg, lq // tq),
        in_specs=[pl.BlockSpec((None, tq, hg * LANES), lambda bi, g, qi: (bi, qi, g)),
                  pl.BlockSpec((None, s_pad, hg * LANES), lambda bi, g, qi: (bi, 0, g), **once),
                  pl.BlockSpec((None, s_pad // tk, hg * V_DIM_A, tk), lambda bi, g, qi: (bi, 0, g, 0), **once)],
        out_specs=pl.BlockSpec((None, tq, hg * V_DIM_A), lambda bi, g, qi: (bi, qi, g)),
        out_shape=jax.ShapeDtypeStruct((b, lq, N_HEADS_A * V_DIM_A), MXU_DTYPE),
        scratch_shapes=[pltpu.VMEM((hg, tk, tq), jnp.float32), pltpu.VMEM((hg, V_DIM_A, tq), jnp.float32)],
        compiler_params=_cparams(3), name="mla",
    )(q, k, vt)


def _f2key(x):
    b = lax.bitcast_convert_type(x, jnp.int32)
    return jnp.where(b < 0, b ^ jnp.int32(0x7FFFFFFF), b)


def _key2f(k):
    return lax.bitcast_convert_type(jnp.where(k < 0, k ^ jnp.int32(0x7FFFFFFF), k), jnp.float32)


_COUNT_SLAB = 4 * SUBLANES
_SCORE_GROUP = 4
_COUNT_GROUP = 4
_SEARCH_BISECT_EVERY = 8
_SEARCH_MAX_PROBES = 40 * _SEARCH_BISECT_EVERY


def _dsa_kernel(iq_ref, qb_ref, iw_ref, ik_ref, kb_ref, vbt_ref, tri_ref, o_ref,
                sc_ref, wt_ref, s_ref, acc_ref, *, tq, tk, q_off, s_len, topk):
    p0 = q_off + pl.program_id(1) * tq
    last = _last_block(p0, tq, tk, s_len)
    n_blk = last + 1
    row = lambda v: jnp.full((1, tq), v, jnp.float32)
    colsum = lambda x: jnp.sum(x, axis=0, keepdims=True)

    wt_ref[...] = iw_ref[...].T[:SUBLANES, :]

    def score_block(kb):
        start = pl.multiple_of(kb * tk, tk)
        ikb = ik_ref[pl.ds(start, tk), :]
        acc = jnp.zeros((tk, tq), jnp.float32)
        for h in range(N_IDX_HEADS):
            x = lax.dot_general(ikb, iq_ref[:, h * LANES:(h + 1) * LANES], _NT, preferred_element_type=jnp.float32)
            acc = acc + wt_ref[h:h + 1, :] * jnp.maximum(x, 0.0)
        return acc

    def stats(carry, hi_fill, lo_fill):
        mx, mn, n_pos, n_nonneg = carry
        return (jnp.maximum(mx, jnp.max(hi_fill, axis=0, keepdims=True)),
                jnp.minimum(mn, jnp.min(lo_fill, axis=0, keepdims=True)),
                n_pos + colsum(jnp.where(hi_fill >= FLT_MIN, 1.0, 0.0)),
                n_nonneg + colsum(jnp.where(hi_fill >= 0.0, 1.0, 0.0)))

    def p1_body(kb, carry):
        acc = score_block(kb)
        sc_ref[kb] = acc
        return stats(carry, acc, acc)

    carry = _fori_grouped(last, p1_body, (row(-jnp.inf), row(jnp.inf), row(0.0), row(0.0)), _SCORE_GROUP)
    acc = score_block(last)
    adm = _admissible_t(p0, last, tq, tk, s_len)
    masked = jnp.where(adm, acc, -jnp.inf)
    sc_ref[last] = masked
    mx, mn, n_pos, n_nonneg = stats(carry, masked, jnp.where(adm, acc, jnp.inf))

    qpos = p0 + lax.broadcasted_iota(jnp.int32, (1, tq), 1)
    n_adm = jnp.minimum(((qpos >> _CHUNK_SHIFT) + 1) * CHUNK, s_len).astype(jnp.float32)
    k_f = jnp.float32(topk)
    log_k = jnp.log(k_f)
    phi = lambda c: jnp.log(jnp.maximum(c, 0.25)) - log_k

    few = n_adm <= k_f
    pos = n_pos >= k_f
    at_zero = jnp.logical_and(~pos, n_nonneg >= k_f)
    above_max = _key2f(_f2key(jnp.maximum(mx, LOWEST)) + 1)
    lo = jnp.where(pos, FLT_MIN, LOWEST)
    c_lo = jnp.where(pos, n_pos, n_adm)
    hi = jnp.where(pos | at_zero, above_max, 0.0)
    c_hi = jnp.where(pos | at_zero, 0.0, n_nonneg)
    lo_val = jnp.where(pos, 0.0, jnp.maximum(mn, LOWEST))
    tau = jnp.where(few, LOWEST, jnp.where(at_zero, 0.0, lo))
    r = jnp.where(at_zero & ~few, k_f - n_pos, 0.0)
    done = jnp.where(few | at_zero, 1.0, 0.0)

    def count_ge(g):
        def body(kb, cnt):
            hit = jnp.where(sc_ref[kb] >= g, 1.0, 0.0)
            return cnt + jnp.sum(hit.reshape(tk // _COUNT_SLAB, _COUNT_SLAB, tq), axis=0)

        return colsum(_fori_grouped(n_blk, body, jnp.zeros((_COUNT_SLAB, tq), jnp.float32), _COUNT_GROUP))

    def search_cond(st):
        return jnp.logical_and(st[1] > 0, st[0] < _SEARCH_MAX_PROBES)

    def search_body(st):
        it, _, lo, hi, c_lo, c_hi, f_lo, f_hi, lo_val, side, tau, r, done = st
        lok, hik = _f2key(lo), _f2key(hi)
        midk = (lok >> 1) + (hik >> 1) + (lok & hik & 1)
        frac = f_lo / jnp.maximum(f_lo - f_hi, 1e-9)
        g_int = lo_val + (hi - lo_val) * frac
        bisect = (it % _SEARCH_BISECT_EVERY) == (_SEARCH_BISECT_EVERY - 1)
        g = jnp.where(bisect, _key2f(midk), g_int)
        gk = jnp.minimum(jnp.maximum(_f2key(g), lok + 1), jnp.maximum(hik - 1, lok + 1))
        g = _key2f(gk)
        c = count_ge(g)
        f_g = phi(c)
        live = done == 0.0
        up = live & (c >= k_f)
        dn = live & (c < k_f)
        f_hi = jnp.where(up & (side > 0.0), f_hi * 0.5, f_hi)
        f_lo = jnp.where(dn & (side < 0.0), f_lo * 0.5, f_lo)
        lo, c_lo, f_lo, lo_val = (jnp.where(up, g, lo), jnp.where(up, c, c_lo), jnp.where(up, f_g, f_lo),
                                  jnp.where(up, g, lo_val))
        hi, c_hi, f_hi = jnp.where(dn, g, hi), jnp.where(dn, c, c_hi), jnp.where(dn, f_g, f_hi)
        side = jnp.where(up, 1.0, jnp.where(dn, -1.0, side))
        adjacent = (_f2key(lo) + 1) >= _f2key(hi)
        fin = live & ((c_lo == k_f) | adjacent)
        tau = jnp.where(fin, lo, tau)
        r = jnp.where(fin & (c_lo > k_f), k_f - c_hi, r)
        done = jnp.where(fin, 1.0, done)
        pending = jnp.sum(1.0 - done).astype(jnp.int32)
        return (it + 1, pending, lo, hi, c_lo, c_hi, f_lo, f_hi, lo_val, side, tau, r, done)

    pending0 = jnp.sum(1.0 - done).astype(jnp.int32)
    st = lax.while_loop(search_cond, search_body,
                        (jnp.int32(0), pending0, lo, hi, c_lo, c_hi, phi(c_lo), phi(c_hi), lo_val, row(0.0), tau, r,
                         done))
    tau, r = st[10], st[11]

    @pl.when(jnp.sum(r) > 0.0)
    def _():
        tied = r > 0.0
        budget = jnp.where(tied, r, jnp.inf)

        def body(kb, seen):
            blk = sc_ref[kb]
            eq = jnp.logical_and(blk == tau, tied)
            eq_f = jnp.where(eq, 1.0, 0.0)
            rank = jnp.dot(tri_ref[...], eq_f.astype(MXU_DTYPE), preferred_element_type=jnp.float32) + seen
            sc_ref[kb] = jnp.where(jnp.logical_and(eq, rank > budget), -jnp.inf, blk)
            return seen + colsum(eq_f)

        _fori_grouped(n_blk, body, row(0.0), _COUNT_GROUP)

    heads_per_kv = N_HEADS_B // N_KV_B

    def qk(kb):
        start = pl.multiple_of(kb * tk, tk)
        kblk = kb_ref[pl.ds(start, tk), :]
        return lambda h: lax.dot_general(kblk, qb_ref[:, h * LANES:(h + 1) * LANES], _NT,
                                         preferred_element_type=jnp.float32)

    def sel_bias(kb):
        return jnp.where(sc_ref[kb] >= tau, 0.0, NEG_BIG)

    def values(kb):
        vt_all = vbt_ref[kb]
        return [vt_all[(h // heads_per_kv) * HEAD_DIM_B:(h // heads_per_kv + 1) * HEAD_DIM_B, :]
                for h in range(N_HEADS_B)]

    def p3_body(kb, state):
        return _attn_step(state, values(kb), qk(kb + 1), sel_bias(kb + 1), s_ref, acc_ref)

    state = _attn_step((_softmax_init(N_HEADS_B, tq, acc_ref), None), None, qk(0), sel_bias(0), s_ref, acc_ref)
    state = lax.fori_loop(0, last, p3_body, state)
    (_, l), _ = _attn_step(state, values(last), None, None, s_ref, acc_ref)
    _softmax_finish(l, acc_ref, o_ref)


def _dsa_call(iq, qb, iw, ik, kb, vbt, tri, *, tq, tk, q_off, s_len):
    b, lq, _ = iq.shape
    s_pad = ik.shape[1]
    n_kb = s_pad // tk
    topk = min(TOPK_MAX, s_len // 4)
    kern = functools.partial(_dsa_kernel, tq=tq, tk=tk, q_off=q_off, s_len=s_len, topk=topk)
    qspec = lambda w: pl.BlockSpec((None, tq, w), lambda bi, qi: (bi, qi, 0))
    once = dict(pipeline_mode=pl.Buffered(1))
    kspec = pl.BlockSpec((None, s_pad, LANES), lambda bi, qi: (bi, 0, 0), **once)
    return pl.pallas_call(
        kern, grid=(b, lq // tq),
        in_specs=[qspec(N_IDX_HEADS * LANES), qspec(N_HEADS_B * LANES), qspec(LANES), kspec, kspec,
                  pl.BlockSpec((None, n_kb, LANES, tk), lambda bi, qi: (bi, 0, 0, 0), **once),
                  _const_spec(tri.shape)],
        out_specs=qspec(N_HEADS_B * HEAD_DIM_B),
        out_shape=jax.ShapeDtypeStruct((b, lq, N_HEADS_B * HEAD_DIM_B), MXU_DTYPE),
        scratch_shapes=[pltpu.VMEM((n_kb, tk, tq), jnp.float32), pltpu.VMEM((SUBLANES, tq), jnp.float32),
                        pltpu.VMEM((N_HEADS_B, tk, tq), jnp.float32),
                        pltpu.VMEM((N_HEADS_B, HEAD_DIM_B, tq), jnp.float32)],
        compiler_params=_cparams(2), name="dsa",
    )(iq, qb, iw, ik, kb, vbt, tri)


_FF_CHUNK = 256


def _out_kernel(x_ref, oa_ref, ob_ref, woa_ref, wob_ref, gf_ref, wg_ref, wu_ref, wd_ref, gfin_ref, y_ref):
    o = (jnp.dot(oa_ref[...], woa_ref[...], preferred_element_type=jnp.float32)
         + jnp.dot(ob_ref[...], wob_ref[...], preferred_element_type=jnp.float32))
    x1 = x_ref[...] + o
    h2 = _rms(x1, gf_ref[...]).astype(MXU_DTYPE)
    ffn = jnp.zeros_like(x1)
    for j in range(D_FF // _FF_CHUNK):
        sl = slice(j * _FF_CHUNK, (j + 1) * _FF_CHUNK)
        gate = jnp.dot(h2, wg_ref[:, sl], preferred_element_type=jnp.float32)
        up = jnp.dot(h2, wu_ref[:, sl], preferred_element_type=jnp.float32)
        act = (gate * jax.nn.sigmoid(gate)) * up
        ffn = ffn + jnp.dot(act.astype(MXU_DTYPE), wd_ref[sl, :], preferred_element_type=jnp.float32)
    y_ref[...] = _rms(x1 + ffn, gfin_ref[...])


def _out_call(x2d, oa, ob, woa, wob, g_ffn, wg, wu, wd, g_fin, tm):
    t_tok = x2d.shape[0]
    row = lambda w: pl.BlockSpec((tm, w), lambda i: (i, 0))
    half = N_HEADS_A * V_DIM_A
    return pl.pallas_call(
        _out_kernel, grid=(t_tok // tm,),
        in_specs=[row(D_MODEL), row(half), row(half), _const_spec(woa.shape), _const_spec(wob.shape),
                  _const_spec((1, D_MODEL)), _const_spec(wg.shape), _const_spec(wu.shape), _const_spec(wd.shape),
                  _const_spec((1, D_MODEL))],
        out_specs=row(D_MODEL), out_shape=jax.ShapeDtypeStruct((t_tok, D_MODEL), jnp.float32),
        compiler_params=_cparams(1), name="out",
    )(x2d, oa, ob, woa, wob, g_ffn, wg, wu, wd, g_fin)


def _prep_weights(w_in, w_q_b, w_kv_b, w_o, w_gate, w_up, w_down):
    off = np.concatenate([[0], np.cumsum(IN_SIZES)])
    o_qlat, o_kvlat, o_kr, o_qb, o_kb, o_vb, o_iq, o_ik, o_iw = off[:9]
    zeros = lambda n: jnp.zeros((D_MODEL, n), w_in.dtype)
    cols = [w_in[:, o_qlat:o_qlat + Q_LORA], w_in[:, o_kvlat:o_kvlat + KV_LORA]]
    for h in range(N_HEADS_B):
        head = w_in[:, o_qb + h * HEAD_DIM_B:o_qb + (h + 1) * HEAD_DIM_B]
        cols += [head, zeros(HEAD_DIM_B)] if h < N_HEADS_B // N_KV_B else [zeros(HEAD_DIM_B), head]
    for h in range(N_IDX_HEADS):
        cols += [w_in[:, o_iq + h * IDX_DIM:o_iq + (h + 1) * IDX_DIM], zeros(LANES - IDX_DIM)]
    cols += [w_in[:, o_kb:o_kb + LANES], w_in[:, o_vb:o_vb + LANES],
             w_in[:, o_ik:o_ik + IDX_DIM], zeros(LANES - IDX_DIM),
             zeros(NOPE_DIM), w_in[:, o_kr:o_kr + ROPE_DIM], zeros(LANES - NOPE_DIM - ROPE_DIM),
             w_in[:, o_iw:o_iw + N_IDX_HEADS], zeros(LANES - N_IDX_HEADS)]
    w_in_p = jnp.concatenate(cols, axis=1).astype(MXU_DTYPE)
    assert w_in_p.shape[1] == _C_END

    qd = NOPE_DIM + ROPE_DIM
    w_qb_p = jnp.pad(w_q_b.reshape(Q_LORA, N_HEADS_A, qd), ((0, 0), (0, 0), (0, LANES - qd)))
    w_qb_p = w_qb_p.reshape(Q_LORA, N_HEADS_A * LANES).astype(MXU_DTYPE)

    kv = w_kv_b.reshape(KV_LORA, N_HEADS_A, NOPE_DIM + V_DIM_A)
    wk = jnp.pad(kv[:, :, :NOPE_DIM], ((0, 0), (0, 0), (0, LANES - NOPE_DIM)))
    wk = wk.reshape(KV_LORA, N_HEADS_A * LANES).astype(MXU_DTYPE)
    wvt = kv[:, :, NOPE_DIM:].reshape(KV_LORA, N_HEADS_A * V_DIM_A).T.astype(MXU_DTYPE)

    half = N_HEADS_A * V_DIM_A
    return (w_in_p, w_qb_p, wk, wvt, w_o[:half].astype(MXU_DTYPE), w_o[half:].astype(MXU_DTYPE),
            w_gate.astype(MXU_DTYPE), w_up.astype(MXU_DTYPE), w_down.astype(MXU_DTYPE))


def _rope_tables(pos):
    posf = pos.astype(jnp.float32)[:, None]
    n = pos.shape[0]

    def cs(d):
        inv = jnp.power(ROPE_THETA, -jnp.arange(0, d, 2, dtype=jnp.float32) / d)
        ang = posf * inv[None, :]
        return jnp.cos(ang), jnp.sin(ang)

    one, zero = (lambda w: jnp.ones((n, w), jnp.float32)), (lambda w: jnp.zeros((n, w), jnp.float32))
    c, s = cs(ROPE_DIM)
    cq = jnp.concatenate([one(NOPE_DIM), c, c, one(LANES - NOPE_DIM - ROPE_DIM)], axis=1)
    sq = jnp.concatenate([zero(NOPE_DIM), -s, s, zero(LANES - NOPE_DIM - ROPE_DIM)], axis=1)
    c, s = cs(ROT_DIM_B)
    c64 = jnp.tile(jnp.concatenate([c, c, one(HEAD_DIM_B - ROT_DIM_B)], axis=1), (1, LANES // HEAD_DIM_B))
    s64 = jnp.tile(jnp.concatenate([-s, s, zero(HEAD_DIM_B - ROT_DIM_B)], axis=1), (1, LANES // HEAD_DIM_B))
    return cq, sq, c64, s64


def _pick_tile(n, pref):
    t = min(pref, n)
    while n % t:
        t //= 2
    return t


def _round_up(n, m):
    return -(-n // m) * m


def _layer(x, pos0, past, weights, norms, final_norm):
    b, l, _ = x.shape
    g_attn, g_q, g_kv, g_ffn = norms
    w_in_p, w_qb_p, wk, wvt, woa, wob, wg, wu, wd = weights
    t_tok = b * l
    x2d = x.reshape(t_tok, D_MODEL)

    tabs = _rope_tables(pos0 + jnp.arange(l))
    if l >= TM // 2:
        tm = _pick_tile(l, TM)
        n_tab_tiles = l // tm
    else:
        tm = _pick_tile(t_tok, TM)
        tabs = tuple(jnp.tile(t, (tm // l, 1)) for t in tabs)
        n_tab_tiles = 1
    (ckv, kr128, kb, vb, ik128, iw, q_mla, q_b, iq, kb16, vb16, ik16) = _proj_call(
        x2d, tabs, n_tab_tiles, tm, g_attn, w_in_p, g_q, w_qb_p, g_kv)

    new = (ckv.reshape(b, l, KV_LORA), kr128.reshape(b, l, LANES)[..., NOPE_DIM:NOPE_DIM + ROPE_DIM],
           kb.reshape(b, l, N_KV_B, HEAD_DIM_B), vb.reshape(b, l, N_KV_B, HEAD_DIM_B),
           ik128.reshape(b, l, LANES)[..., :IDX_DIM])

    per_row = lambda a: a.reshape(b, l, a.shape[-1])
    p_len = 0 if past is None else past[0].shape[1]
    s_len = p_len + l
    lq = _round_up(l, LANES)
    tq_mla, tq_dsa = _pick_tile(lq, TQ_MLA), _pick_tile(lq, TQ_DSA)
    tk_mla, tk_dsa = (2 * TK_MLA, 2 * TK_DSA) if lq < TQ_DSA else (TK_MLA, TK_DSA)
    s_pad = _round_up(s_len, max(tk_mla, tk_dsa))
    if past is None:
        old = [None] * 5
    else:
        p_lat, p_kr, p_k, p_v, p_ik = past
        old = [p_lat,
               jnp.pad(p_kr, ((0, 0), (0, 0), (NOPE_DIM, LANES - NOPE_DIM - ROPE_DIM))),
               p_k.reshape(b, p_len, LANES).astype(MXU_DTYPE), p_v.reshape(b, p_len, LANES).astype(MXU_DTYPE),
               jnp.pad(p_ik, ((0, 0), (0, 0), (0, LANES - IDX_DIM))).astype(MXU_DTYPE)]

    def full_seq(prev, cur):
        parts = ([] if prev is None else [prev]) + [per_row(cur)]
        if s_pad > s_len:
            parts.append(jnp.zeros((b, s_pad - s_len, cur.shape[-1]), cur.dtype))
        return parts[0] if len(parts) == 1 else jnp.concatenate(parts, axis=1)

    c_all, kr_all, kb_all, vb_all, ik_all = (full_seq(o, c) for o, c in zip(old, (ckv, kr128, kb16, vb16, ik16)))
    vbt_all = vb_all.reshape(b, s_pad // tk_dsa, tk_dsa, LANES).transpose(0, 1, 3, 2)

    k_mla, vt_mla = _kvup_call(c_all, kr_all, wk, wvt, tk_mla)

    padq = lambda a: per_row(a) if lq == l else jnp.pad(per_row(a), ((0, 0), (0, lq - l), (0, 0)))
    o_a = _mla_call(padq(q_mla), k_mla, vt_mla, tq=tq_mla, tk=tk_mla, q_off=pos0, s_len=s_len)
    tri = jnp.tril(jnp.ones((tk_dsa, tk_dsa), MXU_DTYPE))
    o_b = _dsa_call(padq(iq), padq(q_b), padq(iw), ik_all, kb_all, vbt_all, tri,
                    tq=tq_dsa, tk=tk_dsa, q_off=pos0, s_len=s_len)
    o_a, o_b = (o[:, :l].reshape(t_tok, -1) for o in (o_a, o_b))

    y = _out_call(x2d, o_a, o_b, woa, wob, g_ffn, wg, wu, wd, final_norm, tm)
    return y.reshape(b, l, D_MODEL), new


def kernel(x_prompt, x_sample, cache_mla_latent, cache_mla_krope, cache_dsa_k, cache_dsa_v, cache_idx_k, attn_norm, w_in, q_a_norm, w_q_b, kv_a_norm, w_kv_b, w_o, ffn_norm, w_gate, w_up, w_down, final_norm):
    depth = w_in.shape[0]
    assert depth == 1, "the fused final norm assumes a single layer"
    past_len = cache_mla_latent.shape[2]
    li = 0
    weights = _prep_weights(w_in[li], w_q_b[li], w_kv_b[li], w_o[li], w_gate[li], w_up[li], w_down[li])
    norms = (attn_norm[li][None, :], q_a_norm[li][None, :], kv_a_norm[li][None, :], ffn_norm[li][None, :])
    fin = final_norm[None, :]
    past = (cache_mla_latent[li], cache_mla_krope[li], cache_dsa_k[li], cache_dsa_v[li], cache_idx_k[li])
    y_p, new_p = _layer(x_prompt, 0, None, weights, norms, fin)
    y_s, new_s = _layer(x_sample, past_len, past, weights, norms, fin)
    return (y_p, y_s) + tuple(a[None] for a in new_p) + tuple(a[None] for a in new_s)
```

```python
import functools

import numpy as np
import jax
import jax.numpy as jnp
from jax import lax
from jax.experimental import pallas as pl
from jax.experimental.pallas import tpu as pltpu

D_MODEL = 1024
CHUNK = 64
ROPE_THETA = 500000.0
EPS = 1e-6
N_HEADS_A = 8
NOPE_DIM = 64
ROPE_DIM = 32
V_DIM_A = 64
Q_LORA = 384
KV_LORA = 256
N_HEADS_B = 8
N_KV_B = 2
HEAD_DIM_B = 64
ROT_DIM_B = 16
N_IDX_HEADS = 8
IDX_DIM = 64
ROT_DIM_IDX = 16
TOPK_MAX = 256
D_FF = 2816
IN_SIZES = (Q_LORA, KV_LORA, ROPE_DIM, N_HEADS_B * HEAD_DIM_B, N_KV_B * HEAD_DIM_B, N_KV_B * HEAD_DIM_B,
            N_IDX_HEADS * IDX_DIM, IDX_DIM, N_IDX_HEADS)

LANES = 128
SUBLANES = 8
MXU_DTYPE = jnp.bfloat16
VMEM_LIMIT = 56 * 1024 * 1024
NEG_BIG = -1e30
LOWEST = -3.0e38
FLT_MIN = 1.1754943508222875e-38
LOG2E = 1.4426950408889634

TM = 512
TQ_MLA = 512
TK_MLA = 512
TQ_DSA = 256
TK_DSA = 512
MLA_HEADS_PER_STEP = 4

_C_QLAT = 0
_C_KVLAT = _C_QLAT + Q_LORA
_C_QB = _C_KVLAT + KV_LORA
_C_IQ = _C_QB + N_HEADS_B * LANES
_C_KB = _C_IQ + N_IDX_HEADS * LANES
_C_VB = _C_KB + LANES
_C_IK = _C_VB + LANES
_C_KR = _C_IK + LANES
_C_IW = _C_KR + LANES
_C_END = _C_IW + LANES


def _cparams(n_axes):
    return pltpu.CompilerParams(dimension_semantics=("arbitrary",) * n_axes, vmem_limit_bytes=VMEM_LIMIT)


def _const_spec(shape):
    return pl.BlockSpec(shape, lambda *_: (0,) * len(shape), pipeline_mode=pl.Buffered(1))


def _rms(x, g):
    return (x * lax.rsqrt(jnp.mean(x * x, axis=-1, keepdims=True) + EPS)) * g


def _rope_tile(x, cos, sin, half, first_half):
    partner = jnp.where(first_half, pltpu.roll(x, LANES - half, 1), pltpu.roll(x, half, 1))
    return x * cos + partner * sin


def _proj_kernel(x_ref, cq_ref, sq_ref, c64_ref, s64_ref, gat_ref, win_ref, gq_ref, wqb_ref, gkv_ref,
                 ckv_ref, kr_ref, kb_ref, vb_ref, ik_ref, iw_ref, qm_ref, qb_ref, iq_ref,
                 kb16_ref, vb16_ref, ik16_ref):
    x = x_ref[...]
    hb = _rms(x, gat_ref[...]).astype(MXU_DTYPE)
    y = jnp.dot(hb, win_ref[...], preferred_element_type=jnp.float32)

    lane = lax.broadcasted_iota(jnp.int32, (1, LANES), 1)
    first_q = lane < NOPE_DIM + ROPE_DIM // 2
    first_64 = (lane & (HEAD_DIM_B - 1)) < ROT_DIM_B // 2
    cq, sq, c64, s64 = cq_ref[...], sq_ref[...], c64_ref[...], s64_ref[...]

    qn = _rms(y[:, _C_QLAT:_C_QLAT + Q_LORA], gq_ref[...]).astype(MXU_DTYPE)
    qa = jnp.dot(qn, wqb_ref[...], preferred_element_type=jnp.float32)
    scale_a = (NOPE_DIM + ROPE_DIM) ** -0.5 * LOG2E
    for h in range(N_HEADS_A):
        t = _rope_tile(qa[:, h * LANES:(h + 1) * LANES], cq, sq, ROPE_DIM // 2, first_q)
        qm_ref[:, h * LANES:(h + 1) * LANES] = (t * scale_a).astype(qm_ref.dtype)

    ckv_ref[...] = _rms(y[:, _C_KVLAT:_C_KVLAT + KV_LORA], gkv_ref[...])
    kr_ref[...] = _rope_tile(y[:, _C_KR:_C_KR + LANES], cq, sq, ROPE_DIM // 2, first_q)

    scale_b = HEAD_DIM_B ** -0.5 * LOG2E
    for h in range(N_HEADS_B):
        t = _rope_tile(y[:, _C_QB + h * LANES:_C_QB + (h + 1) * LANES], c64, s64, ROT_DIM_B // 2, first_64)
        qb_ref[:, h * LANES:(h + 1) * LANES] = (t * scale_b).astype(qb_ref.dtype)
    for h in range(N_IDX_HEADS):
        t = _rope_tile(y[:, _C_IQ + h * LANES:_C_IQ + (h + 1) * LANES], c64, s64, ROT_DIM_IDX // 2, first_64)
        iq_ref[:, h * LANES:(h + 1) * LANES] = t.astype(iq_ref.dtype)

    kb = _rope_tile(y[:, _C_KB:_C_KB + LANES], c64, s64, ROT_DIM_B // 2, first_64)
    kb_ref[...] = kb
    kb16_ref[...] = kb.astype(kb16_ref.dtype)
    vb = y[:, _C_VB:_C_VB + LANES]
    vb_ref[...] = vb
    vb16_ref[...] = vb.astype(vb16_ref.dtype)
    ik = _rope_tile(y[:, _C_IK:_C_IK + LANES], c64, s64, ROT_DIM_IDX // 2, first_64)
    ik_ref[...] = ik
    ik16_ref[...] = ik.astype(ik16_ref.dtype)
    iw_ref[...] = y[:, _C_IW:_C_IW + LANES] * ((N_IDX_HEADS * IDX_DIM) ** -0.5)


def _proj_call(x2d, tabs, n_tab_tiles, tm, g_attn, w_in_p, g_q, w_qb_p, g_kv):
    t_tok = x2d.shape[0]
    grid = (t_tok // tm,)
    row = lambda w: pl.BlockSpec((tm, w), lambda i: (i, 0))
    tab = pl.BlockSpec((tm, LANES), lambda i: (i % n_tab_tiles, 0))
    f32, b16 = jnp.float32, MXU_DTYPE
    out_shapes = [
        jax.ShapeDtypeStruct((t_tok, KV_LORA), f32),
        jax.ShapeDtypeStruct((t_tok, LANES), f32),
        jax.ShapeDtypeStruct((t_tok, LANES), f32),
        jax.ShapeDtypeStruct((t_tok, LANES), f32),
        jax.ShapeDtypeStruct((t_tok, LANES), f32),
        jax.ShapeDtypeStruct((t_tok, LANES), f32),
        jax.ShapeDtypeStruct((t_tok, N_HEADS_A * LANES), b16),
        jax.ShapeDtypeStruct((t_tok, N_HEADS_B * LANES), b16),
        jax.ShapeDtypeStruct((t_tok, N_IDX_HEADS * LANES), b16),
        jax.ShapeDtypeStruct((t_tok, LANES), b16),
        jax.ShapeDtypeStruct((t_tok, LANES), b16),
        jax.ShapeDtypeStruct((t_tok, LANES), b16),
    ]
    out_specs = [row(s.shape[1]) for s in out_shapes]
    in_specs = [row(D_MODEL), tab, tab, tab, tab, _const_spec((1, D_MODEL)), _const_spec(w_in_p.shape),
                _const_spec((1, Q_LORA)), _const_spec(w_qb_p.shape), _const_spec((1, KV_LORA))]
    return pl.pallas_call(
        _proj_kernel, grid=grid, in_specs=in_specs, out_specs=out_specs, out_shape=out_shapes,
        compiler_params=_cparams(1), name="proj",
    )(x2d, *tabs, g_attn, w_in_p, g_q, w_qb_p, g_kv)


_NT = (((1,), (1,)), ((), ()))


def _kvup_kernel(c_ref, kr_ref, wk_ref, wvt_ref, k_ref, vt_ref):
    cb = c_ref[...].astype(MXU_DTYPE)
    kn = jnp.dot(cb, wk_ref[...], preferred_element_type=jnp.float32)
    kr = kr_ref[...]
    for h in range(N_HEADS_A):
        k_ref[:, h * LANES:(h + 1) * LANES] = (kn[:, h * LANES:(h + 1) * LANES] + kr).astype(k_ref.dtype)
    vt_ref[...] = lax.dot_general(wvt_ref[...], cb, _NT, preferred_element_type=jnp.float32).astype(vt_ref.dtype)


def _kvup_call(c3d, kr3d, wk, wvt, tk):
    b, s_pad, _ = c3d.shape
    row = lambda w: pl.BlockSpec((None, tk, w), lambda bi, i: (bi, i, 0))
    vdim = N_HEADS_A * V_DIM_A
    return pl.pallas_call(
        _kvup_kernel, grid=(b, s_pad // tk),
        in_specs=[row(KV_LORA), row(LANES), _const_spec(wk.shape), _const_spec(wvt.shape)],
        out_specs=[row(N_HEADS_A * LANES), pl.BlockSpec((None, None, vdim, tk), lambda bi, i: (bi, i, 0, 0))],
        out_shape=[jax.ShapeDtypeStruct((b, s_pad, N_HEADS_A * LANES), MXU_DTYPE),
                   jax.ShapeDtypeStruct((b, s_pad // tk, vdim, tk), MXU_DTYPE)],
        compiler_params=_cparams(2), name="kvup",
    )(c3d, kr3d, wk, wvt)


_CHUNK_SHIFT = CHUNK.bit_length() - 1


def _admissible_t(p0, kb, tq, tk, s_len):
    kpos = kb * tk + lax.broadcasted_iota(jnp.int32, (tk, 1), 0)
    qpos = p0 + lax.broadcasted_iota(jnp.int32, (1, tq), 1)
    return ((qpos >> _CHUNK_SHIFT) >= (kpos >> _CHUNK_SHIFT)) & (kpos < s_len)


def _check_tiles(tq, tk, q_off, lq):
    assert tk & (tk - 1) == 0 and tk % tq == 0 and tk % CHUNK == 0, (tq, tk)
    assert q_off % tq == 0 or (lq == tq and q_off % tk + tq <= tk), (tq, tk, q_off, lq)


def _last_block(p0, tq, tk, s_len):
    return jnp.minimum(p0 + tq - 1, s_len - 1) >> (tk.bit_length() - 1)


def _fori_grouped(n, body, carry, group):
    shift = group.bit_length() - 1

    def several(j, c):
        for i in range(group):
            c = body(group * j + i, c)
        return c

    carry = lax.fori_loop(0, n >> shift, several, carry)
    return lax.fori_loop((n >> shift) << shift, n, body, carry)


def _softmax_init(n_heads, tq, acc_ref):
    acc_ref[...] = jnp.zeros(acc_ref.shape, jnp.float32)
    return jnp.full((n_heads, tq), NEG_BIG, jnp.float32), jnp.zeros((n_heads, tq), jnp.float32)


def _attn_step(state, values, next_scores, bias, s_ref, acc_ref):
    (m_prev, l_prev), maxima = state
    n_heads = s_ref.shape[0]
    sums, next_max = [], []
    if values is not None:
        m_new = jnp.maximum(m_prev, maxima)
        alpha = jnp.exp2(m_prev - m_new)
    for h in range(n_heads):
        if values is not None:
            p = jnp.exp2(s_ref[h] - m_new[h:h + 1, :])
            sums.append(jnp.sum(p, axis=0, keepdims=True))
            acc_ref[h] = alpha[h:h + 1, :] * acc_ref[h] + jnp.dot(values[h], p.astype(values[h].dtype),
                                                                   preferred_element_type=jnp.float32)
        if next_scores is not None:
            s = next_scores(h)
            if bias is not None:
                s = s + bias
            s_ref[h] = s
            next_max.append(jnp.max(s, axis=0, keepdims=True))
    carry = (m_new, alpha * l_prev + jnp.concatenate(sums, axis=0)) if values is not None else (m_prev, l_prev)
    return carry, (jnp.concatenate(next_max, axis=0) if next_scores is not None else maxima)


def _softmax_finish(l, acc_ref, o_ref):
    inv = 1.0 / l
    o_t = jnp.concatenate([acc_ref[h] * inv[h:h + 1, :] for h in range(l.shape[0])], axis=0)
    o_ref[...] = o_t.T.astype(o_ref.dtype)


def _mla_kernel(q_ref, k_ref, vt_ref, o_ref, s_ref, acc_ref, *, tq, tk, q_off, s_len):
    p0 = q_off + pl.program_id(2) * tq
    last = _last_block(p0, tq, tk, s_len)
    heads = range(MLA_HEADS_PER_STEP)

    def qk(kb):
        start = pl.multiple_of(kb * tk, tk)
        return lambda h: lax.dot_general(k_ref[pl.ds(start, tk), h * LANES:(h + 1) * LANES],
                                         q_ref[:, h * LANES:(h + 1) * LANES], _NT,
                                         preferred_element_type=jnp.float32)

    def mask_bias(kb):
        return jnp.where(_admissible_t(p0, kb, tq, tk, s_len), 0.0, NEG_BIG)

    def values(kb):
        vt_all = vt_ref[kb]
        return [vt_all[h * V_DIM_A:(h + 1) * V_DIM_A, :] for h in heads]

    def body(kb, state):
        return _attn_step(state, values(kb), qk(kb + 1), None, s_ref, acc_ref)

    def tail(state):
        return _attn_step(state, values(last - 1), qk(last), mask_bias(last), s_ref, acc_ref)

    state = _attn_step((_softmax_init(MLA_HEADS_PER_STEP, tq, acc_ref), None), None, qk(0), mask_bias(0), s_ref,
                       acc_ref)
    state = lax.fori_loop(0, jnp.maximum(last - 1, 0), body, state)
    state = lax.cond(last > 0, tail, lambda st: st, state)
    (_, l), _ = _attn_step(state, values(last), None, None, s_ref, acc_ref)
    _softmax_finish(l, acc_ref, o_ref)


def _mla_call(q, k, vt, *, tq, tk, q_off, s_len):
    b, lq, _ = q.shape
    _check_tiles(tq, tk, q_off, lq)
    s_pad = k.shape[1]
    hg = MLA_HEADS_PER_STEP
    kern = functools.partial(_mla_kernel, tq=tq, tk=tk, q_off=q_off, s_len=s_len)
    once = dict(pipeline_mode=pl.Buffered(1))
    return pl.pallas_call(
        kern, grid=(b, N_HEADS_A // hg, lq // tq),
        in_specs=[pl.BlockSpec((None, tq, hg * LANES), lambda bi, g, qi: (bi, qi, g)),
                  pl.BlockSpec((None, s_pad, hg * LANES), lambda bi, g, qi: (bi, 0, g), **once),
                  pl.BlockSpec((None, s_pad // tk, hg * V_DIM_A, tk), lambda bi, g, qi: (bi, 0, g, 0), **once)],
        out_specs=pl.BlockSpec((None, tq, hg * V_DIM_A), lambda bi, g, qi: (bi, qi, g)),
        out_shape=jax.ShapeDtypeStruct((b, lq, N_HEADS_A * V_DIM_A), MXU_DTYPE),
        scratch_shapes=[pltpu.VMEM((hg, tk, tq), jnp.float32), pltpu.VMEM((hg, V_DIM_A, tq), jnp.float32)],
        compiler_params=_cparams(3), name="mla",
    )(q, k, vt)


def _f2key(x):
    b = lax.bitcast_convert_type(x, jnp.int32)
    return jnp.where(b < 0, b ^ jnp.int32(0x7FFFFFFF), b)


def _key2f(k):
    return lax.bitcast_convert_type(jnp.where(k < 0, k ^ jnp.int32(0x7FFFFFFF), k), jnp.float32)


_COUNT_SLAB = 4 * SUBLANES
_SCORE_GROUP = 4
_COUNT_GROUP = 4
_SEARCH_BISECT_EVERY = 8
_SEARCH_MAX_PROBES = 40 * _SEARCH_BISECT_EVERY


def _dsa_kernel(iq_ref, qb_ref, iw_ref, ik_ref, kb_ref, vbt_ref, tri_ref, o_ref,
                sc_ref, wt_ref, s_ref, acc_ref, *, tq, tk, q_off, s_len, topk):
    p0 = q_off + pl.program_id(1) * tq
    last = _last_block(p0, tq, tk, s_len)
    n_blk = last + 1
    row = lambda v: jnp.full((1, tq), v, jnp.float32)
    colsum = lambda x: jnp.sum(x, axis=0, keepdims=True)

    wt_ref[...] = iw_ref[...].T[:SUBLANES, :]

    def score_block(kb):
        start = pl.multiple_of(kb * tk, tk)
        ikb = ik_ref[pl.ds(start, tk), :]
        acc = jnp.zeros((tk, tq), jnp.float32)
        for h in range(N_IDX_HEADS):
            x = lax.dot_general(ikb, iq_ref[:, h * LANES:(h + 1) * LANES], _NT, preferred_element_type=jnp.float32)
            acc = acc + wt_ref[h:h + 1, :] * jnp.maximum(x, 0.0)
        return acc

    def stats(carry, hi_fill, lo_fill):
        mx, mn, n_pos, n_nonneg = carry
        return (jnp.maximum(mx, jnp.max(hi_fill, axis=0, keepdims=True)),
                jnp.minimum(mn, jnp.min(lo_fill, axis=0, keepdims=True)),
                n_pos + colsum(jnp.where(hi_fill >= FLT_MIN, 1.0, 0.0)),
                n_nonneg + colsum(jnp.where(hi_fill >= 0.0, 1.0, 0.0)))

    def p1_body(kb, carry):
        acc = score_block(kb)
        sc_ref[kb] = acc
        return stats(carry, acc, acc)

    carry = _fori_grouped(last, p1_body, (row(-jnp.inf), row(jnp.inf), row(0.0), row(0.0)), _SCORE_GROUP)
    acc = score_block(last)
    adm = _admissible_t(p0, last, tq, tk, s_len)
    masked = jnp.where(adm, acc, -jnp.inf)
    sc_ref[last] = masked
    mx, mn, n_pos, n_nonneg = stats(carry, masked, jnp.where(adm, acc, jnp.inf))

    qpos = p0 + lax.broadcasted_iota(jnp.int32, (1, tq), 1)
    n_adm = jnp.minimum(((qpos >> _CHUNK_SHIFT) + 1) * CHUNK, s_len).astype(jnp.float32)
    k_f = jnp.float32(topk)
    log_k = jnp.log(k_f)
    phi = lambda c: jnp.log(jnp.maximum(c, 0.25)) - log_k

    few = n_adm <= k_f
    pos = n_pos >= k_f
    at_zero = jnp.logical_and(~pos, n_nonneg >= k_f)
    above_max = _key2f(_f2key(jnp.maximum(mx, LOWEST)) + 1)
    lo = jnp.where(pos, FLT_MIN, LOWEST)
    c_lo = jnp.where(pos, n_pos, n_adm)
    hi = jnp.where(pos | at_zero, above_max, 0.0)
    c_hi = jnp.where(pos | at_zero, 0.0, n_nonneg)
    lo_val = jnp.where(pos, 0.0, jnp.maximum(mn, LOWEST))
    tau = jnp.where(few, LOWEST, jnp.where(at_zero, 0.0, lo))
    r = jnp.where(at_zero & ~few, k_f - n_pos, 0.0)
    done = jnp.where(few | at_zero, 1.0, 0.0)

    def count_ge(g):
        def body(kb, cnt):
            hit = jnp.where(sc_ref[kb] >= g, 1.0, 0.0)
            return cnt + jnp.sum(hit.reshape(tk // _COUNT_SLAB, _COUNT_SLAB, tq), axis=0)

        return colsum(_fori_grouped(n_blk, body, jnp.zeros((_COUNT_SLAB, tq), jnp.float32), _COUNT_GROUP))

    def search_cond(st):
        return jnp.logical_and(st[1] > 0, st[0] < _SEARCH_MAX_PROBES)

    def next_probe(it, lo, hi, f_lo, f_hi, lo_val):
        lok, hik = _f2key(lo), _f2key(hi)
        midk = (lok >> 1) + (hik >> 1) + (lok & hik & 1)
        frac = f_lo / jnp.maximum(f_lo - f_hi, 1e-9)
        g_int = lo_val + (hi - lo_val) * frac
        bisect = (it % _SEARCH_BISECT_EVERY) == (_SEARCH_BISECT_EVERY - 1)
        g = jnp.where(bisect, _key2f(midk), g_int)
        gk = jnp.minimum(jnp.maximum(_f2key(g), lok + 1), jnp.maximum(hik - 1, lok + 1))
        return _key2f(gk)

    def search_body(st):
        it, _, g, lo, hi, c_lo, c_hi, f_lo, f_hi, lo_val, side, tau, r, done = st
        c = count_ge(g)
        f_g = phi(c)
        live = done == 0.0
        up = live & (c >= k_f)
        dn = live & (c < k_f)
        f_hi = jnp.where(up & (side > 0.0), f_hi * 0.5, f_hi)
        f_lo = jnp.where(dn & (side < 0.0), f_lo * 0.5, f_lo)
        lo, c_lo, f_lo, lo_val = (jnp.where(up, g, lo), jnp.where(up, c, c_lo), jnp.where(up, f_g, f_lo),
                                  jnp.where(up, g, lo_val))
        hi, c_hi, f_hi = jnp.where(dn, g, hi), jnp.where(dn, c, c_hi), jnp.where(dn, f_g, f_hi)
        side = jnp.where(up, 1.0, jnp.where(dn, -1.0, side))
        adjacent = (_f2key(lo) + 1) >= _f2key(hi)
        fin = live & ((c_lo == k_f) | adjacent)
        tau = jnp.where(fin, lo, tau)
        r = jnp.where(fin & (c_lo > k_f), k_f - c_hi, r)
        done = jnp.where(fin, 1.0, done)
        pending = jnp.sum(1.0 - done).astype(jnp.int32)
        g = next_probe(it + 1, lo, hi, f_lo, f_hi, lo_val)
        return (it + 1, pending, g, lo, hi, c_lo, c_hi, f_lo, f_hi, lo_val, side, tau, r, done)

    pending0 = jnp.sum(1.0 - done).astype(jnp.int32)
    f_lo0, f_hi0 = phi(c_lo), phi(c_hi)
    st = lax.while_loop(search_cond, search_body,
                        (jnp.int32(0), pending0, next_probe(0, lo, hi, f_lo0, f_hi0, lo_val), lo, hi, c_lo, c_hi,
                         f_lo0, f_hi0, lo_val, row(0.0), tau, r, done))
    tau, r = st[11], st[12]

    @pl.when(jnp.sum(r) > 0.0)
    def _():
        tied = r > 0.0
        budget = jnp.where(tied, r, jnp.inf)

        def body(kb, seen):
            blk = sc_ref[kb]
            eq = jnp.logical_and(blk == tau, tied)
            eq_f = jnp.where(eq, 1.0, 0.0)
            rank = jnp.dot(tri_ref[...], eq_f.astype(MXU_DTYPE), preferred_element_type=jnp.float32) + seen
            sc_ref[kb] = jnp.where(jnp.logical_and(eq, rank > budget), -jnp.inf, blk)
            return seen + colsum(eq_f)

        _fori_grouped(n_blk, body, row(0.0), _COUNT_GROUP)

    heads_per_kv = N_HEADS_B // N_KV_B

    def qk(kb):
        start = pl.multiple_of(kb * tk, tk)
        kblk = kb_ref[pl.ds(start, tk), :]
        return lambda h: lax.dot_general(kblk, qb_ref[:, h * LANES:(h + 1) * LANES], _NT,
                                         preferred_element_type=jnp.float32)

    def sel_bias(kb):
        return jnp.where(sc_ref[kb] >= tau, 0.0, NEG_BIG)

    def values(kb):
        vt_all = vbt_ref[kb]
        return [vt_all[(h // heads_per_kv) * HEAD_DIM_B:(h // heads_per_kv + 1) * HEAD_DIM_B, :]
                for h in range(N_HEADS_B)]

    def p3_body(kb, state):
        return _attn_step(state, values(kb), qk(kb + 1), sel_bias(kb + 1), s_ref, acc_ref)

    state = _attn_step((_softmax_init(N_HEADS_B, tq, acc_ref), None), None, qk(0), sel_bias(0), s_ref, acc_ref)
    state = lax.fori_loop(0, last, p3_body, state)
    (_, l), _ = _attn_step(state, values(last), None, None, s_ref, acc_ref)
    _softmax_finish(l, acc_ref, o_ref)


def _dsa_call(iq, qb, iw, ik, kb, vbt, tri, *, tq, tk, q_off, s_len):
    b, lq, _ = iq.shape
    _check_tiles(tq, tk, q_off, lq)
    s_pad = ik.shape[1]
    n_kb = s_pad // tk
    topk = min(TOPK_MAX, s_len // 4)
    kern = functools.partial(_dsa_kernel, tq=tq, tk=tk, q_off=q_off, s_len=s_len, topk=topk)
    qspec = lambda w: pl.BlockSpec((None, tq, w), lambda bi, qi: (bi, qi, 0))
    once = dict(pipeline_mode=pl.Buffered(1))
    kspec = pl.BlockSpec((None, s_pad, LANES), lambda bi, qi: (bi, 0, 0), **once)
    return pl.pallas_call(
        kern, grid=(b, lq // tq),
        in_specs=[qspec(N_IDX_HEADS * LANES), qspec(N_HEADS_B * LANES), qspec(LANES), kspec, kspec,
                  pl.BlockSpec((None, n_kb, LANES, tk), lambda bi, qi: (bi, 0, 0, 0), **once),
                  _const_spec(tri.shape)],
        out_specs=qspec(N_HEADS_B * HEAD_DIM_B),
        out_shape=jax.ShapeDtypeStruct((b, lq, N_HEADS_B * HEAD_DIM_B), MXU_DTYPE),
        scratch_shapes=[pltpu.VMEM((n_kb, tk, tq), jnp.float32), pltpu.VMEM((SUBLANES, tq), jnp.float32),
                        pltpu.VMEM((N_HEADS_B, tk, tq), jnp.float32),
                        pltpu.VMEM((N_HEADS_B, HEAD_DIM_B, tq), jnp.float32)],
        compiler_params=_cparams(2), name="dsa",
    )(iq, qb, iw, ik, kb, vbt, tri)


_FF_CHUNK = 256


def _out_kernel(x_ref, oa_ref, ob_ref, woa_ref, wob_ref, gf_ref, wg_ref, wu_ref, wd_ref, gfin_ref, y_ref):
    o = (jnp.dot(oa_ref[...], woa_ref[...], preferred_element_type=jnp.float32)
         + jnp.dot(ob_ref[...], wob_ref[...], preferred_element_type=jnp.float32))
    x1 = x_ref[...] + o
    h2 = _rms(x1, gf_ref[...]).astype(MXU_DTYPE)
    ffn = jnp.zeros_like(x1)
    for j in range(D_FF // _FF_CHUNK):
        sl = slice(j * _FF_CHUNK, (j + 1) * _FF_CHUNK)
        gate = jnp.dot(h2, wg_ref[:, sl], preferred_element_type=jnp.float32)
        up = jnp.dot(h2, wu_ref[:, sl], preferred_element_type=jnp.float32)
        act = (gate * jax.nn.sigmoid(gate)) * up
        ffn = ffn + jnp.dot(act.astype(MXU_DTYPE), wd_ref[sl, :], preferred_element_type=jnp.float32)
    y_ref[...] = _rms(x1 + ffn, gfin_ref[...])


def _out_call(x2d, oa, ob, woa, wob, g_ffn, wg, wu, wd, g_fin, tm):
    t_tok = x2d.shape[0]
    row = lambda w: pl.BlockSpec((tm, w), lambda i: (i, 0))
    half = N_HEADS_A * V_DIM_A
    return pl.pallas_call(
        _out_kernel, grid=(t_tok // tm,),
        in_specs=[row(D_MODEL), row(half), row(half), _const_spec(woa.shape), _const_spec(wob.shape),
                  _const_spec((1, D_MODEL)), _const_spec(wg.shape), _const_spec(wu.shape), _const_spec(wd.shape),
                  _const_spec((1, D_MODEL))],
        out_specs=row(D_MODEL), out_shape=jax.ShapeDtypeStruct((t_tok, D_MODEL), jnp.float32),
        compiler_params=_cparams(1), name="out",
    )(x2d, oa, ob, woa, wob, g_ffn, wg, wu, wd, g_fin)


def _prep_weights(w_in, w_q_b, w_kv_b, w_o, w_gate, w_up, w_down):
    off = np.concatenate([[0], np.cumsum(IN_SIZES)])
    o_qlat, o_kvlat, o_kr, o_qb, o_kb, o_vb, o_iq, o_ik, o_iw = off[:9]
    zeros = lambda n: jnp.zeros((D_MODEL, n), w_in.dtype)
    cols = [w_in[:, o_qlat:o_qlat + Q_LORA], w_in[:, o_kvlat:o_kvlat + KV_LORA]]
    for h in range(N_HEADS_B):
        head = w_in[:, o_qb + h * HEAD_DIM_B:o_qb + (h + 1) * HEAD_DIM_B]
        cols += [head, zeros(HEAD_DIM_B)] if h < N_HEADS_B // N_KV_B else [zeros(HEAD_DIM_B), head]
    for h in range(N_IDX_HEADS):
        cols += [w_in[:, o_iq + h * IDX_DIM:o_iq + (h + 1) * IDX_DIM], zeros(LANES - IDX_DIM)]
    cols += [w_in[:, o_kb:o_kb + LANES], w_in[:, o_vb:o_vb + LANES],
             w_in[:, o_ik:o_ik + IDX_DIM], zeros(LANES - IDX_DIM),
             zeros(NOPE_DIM), w_in[:, o_kr:o_kr + ROPE_DIM], zeros(LANES - NOPE_DIM - ROPE_DIM),
             w_in[:, o_iw:o_iw + N_IDX_HEADS], zeros(LANES - N_IDX_HEADS)]
    w_in_p = jnp.concatenate(cols, axis=1).astype(MXU_DTYPE)
    assert w_in_p.shape[1] == _C_END

    qd = NOPE_DIM + ROPE_DIM
    w_qb_p = jnp.pad(w_q_b.reshape(Q_LORA, N_HEADS_A, qd), ((0, 0), (0, 0), (0, LANES - qd)))
    w_qb_p = w_qb_p.reshape(Q_LORA, N_HEADS_A * LANES).astype(MXU_DTYPE)

    kv = w_kv_b.reshape(KV_LORA, N_HEADS_A, NOPE_DIM + V_DIM_A)
    wk = jnp.pad(kv[:, :, :NOPE_DIM], ((0, 0), (0, 0), (0, LANES - NOPE_DIM)))
    wk = wk.reshape(KV_LORA, N_HEADS_A * LANES).astype(MXU_DTYPE)
    wvt = kv[:, :, NOPE_DIM:].reshape(KV_LORA, N_HEADS_A * V_DIM_A).T.astype(MXU_DTYPE)

    half = N_HEADS_A * V_DIM_A
    return (w_in_p, w_qb_p, wk, wvt, w_o[:half].astype(MXU_DTYPE), w_o[half:].astype(MXU_DTYPE),
            w_gate.astype(MXU_DTYPE), w_up.astype(MXU_DTYPE), w_down.astype(MXU_DTYPE))


def _rope_tables(pos):
    posf = pos.astype(jnp.float32)[:, None]
    n = pos.shape[0]

    def cs(d):
        inv = jnp.power(ROPE_THETA, -jnp.arange(0, d, 2, dtype=jnp.float32) / d)
        ang = posf * inv[None, :]
        return jnp.cos(ang), jnp.sin(ang)

    one, zero = (lambda w: jnp.ones((n, w), jnp.float32)), (lambda w: jnp.zeros((n, w), jnp.float32))
    c, s = cs(ROPE_DIM)
    cq = jnp.concatenate([one(NOPE_DIM), c, c, one(LANES - NOPE_DIM - ROPE_DIM)], axis=1)
    sq = jnp.concatenate([zero(NOPE_DIM), -s, s, zero(LANES - NOPE_DIM - ROPE_DIM)], axis=1)
    c, s = cs(ROT_DIM_B)
    c64 = jnp.tile(jnp.concatenate([c, c, one(HEAD_DIM_B - ROT_DIM_B)], axis=1), (1, LANES // HEAD_DIM_B))
    s64 = jnp.tile(jnp.concatenate([-s, s, zero(HEAD_DIM_B - ROT_DIM_B)], axis=1), (1, LANES // HEAD_DIM_B))
    return cq, sq, c64, s64


def _pick_tile(n, pref):
    t = min(pref, n)
    while n % t:
        t //= 2
    return t


def _round_up(n, m):
    return -(-n // m) * m


def _layer(x, pos0, past, weights, norms, final_norm):
    b, l, _ = x.shape
    g_attn, g_q, g_kv, g_ffn = norms
    w_in_p, w_qb_p, wk, wvt, woa, wob, wg, wu, wd = weights
    t_tok = b * l
    x2d = x.reshape(t_tok, D_MODEL)

    tabs = _rope_tables(pos0 + jnp.arange(l))
    if l >= TM // 2:
        tm = _pick_tile(l, TM)
        n_tab_tiles = l // tm
    else:
        tm = _pick_tile(t_tok, TM)
        tabs = tuple(jnp.tile(t, (tm // l, 1)) for t in tabs)
        n_tab_tiles = 1
    (ckv, kr128, kb, vb, ik128, iw, q_mla, q_b, iq, kb16, vb16, ik16) = _proj_call(
        x2d, tabs, n_tab_tiles, tm, g_attn, w_in_p, g_q, w_qb_p, g_kv)

    new = (ckv.reshape(b, l, KV_LORA), kr128.reshape(b, l, LANES)[..., NOPE_DIM:NOPE_DIM + ROPE_DIM],
           kb.reshape(b, l, N_KV_B, HEAD_DIM_B), vb.reshape(b, l, N_KV_B, HEAD_DIM_B),
           ik128.reshape(b, l, LANES)[..., :IDX_DIM])

    per_row = lambda a: a.reshape(b, l, a.shape[-1])
    p_len = 0 if past is None else past[0].shape[1]
    s_len = p_len + l
    lq = _round_up(l, LANES)
    tq_mla, tq_dsa = _pick_tile(lq, TQ_MLA), _pick_tile(lq, TQ_DSA)
    tk_mla, tk_dsa = (2 * TK_MLA, 2 * TK_DSA) if lq < TQ_DSA else (TK_MLA, TK_DSA)
    s_pad = _round_up(s_len, max(tk_mla, tk_dsa))
    if past is None:
        old = [None] * 5
    else:
        p_lat, p_kr, p_k, p_v, p_ik = past
        old = [p_lat,
               jnp.pad(p_kr, ((0, 0), (0, 0), (NOPE_DIM, LANES - NOPE_DIM - ROPE_DIM))),
               p_k.reshape(b, p_len, LANES).astype(MXU_DTYPE), p_v.reshape(b, p_len, LANES).astype(MXU_DTYPE),
               jnp.pad(p_ik, ((0, 0), (0, 0), (0, LANES - IDX_DIM))).astype(MXU_DTYPE)]

    def full_seq(prev, cur):
        parts = ([] if prev is None else [prev]) + [per_row(cur)]
        if s_pad > s_len:
            parts.append(jnp.zeros((b, s_pad - s_len, cur.shape[-1]), cur.dtype))
        return parts[0] if len(parts) == 1 else jnp.concatenate(parts, axis=1)

    c_all, kr_all, kb_all, vb_all, ik_all = (full_seq(o, c) for o, c in zip(old, (ckv, kr128, kb16, vb16, ik16)))
    vbt_all = vb_all.reshape(b, s_pad // tk_dsa, tk_dsa, LANES).transpose(0, 1, 3, 2)

    k_mla, vt_mla = _kvup_call(c_all, kr_all, wk, wvt, tk_mla)

    padq = lambda a: per_row(a) if lq == l else jnp.pad(per_row(a), ((0, 0), (0, lq - l), (0, 0)))
    o_a = _mla_call(padq(q_mla), k_mla, vt_mla, tq=tq_mla, tk=tk_mla, q_off=pos0, s_len=s_len)
    tri = jnp.tril(jnp.ones((tk_dsa, tk_dsa), MXU_DTYPE))
    o_b = _dsa_call(padq(iq), padq(q_b), padq(iw), ik_all, kb_all, vbt_all, tri,
                    tq=tq_dsa, tk=tk_dsa, q_off=pos0, s_len=s_len)
    o_a, o_b = (o[:, :l].reshape(t_tok, -1) for o in (o_a, o_b))

    y = _out_call(x2d, o_a, o_b, woa, wob, g_ffn, wg, wu, wd, final_norm, tm)
    return y.reshape(b, l, D_MODEL), new


def kernel(x_prompt, x_sample, cache_mla_latent, cache_mla_krope, cache_dsa_k, cache_dsa_v, cache_idx_k, attn_norm, w_in, q_a_norm, w_q_b, kv_a_norm, w_kv_b, w_o, ffn_norm, w_gate, w_up, w_down, final_norm):
    depth = w_in.shape[0]
    assert depth == 1, "the fused final norm assumes a single layer"
    past_len = cache_mla_latent.shape[2]
    li = 0
    weights = _prep_weights(w_in[li], w_q_b[li], w_kv_b[li], w_o[li], w_gate[li], w_up[li], w_down[li])
    norms = (attn_norm[li][None, :], q_a_norm[li][None, :], kv_a_norm[li][None, :], ffn_norm[li][None, :])
    fin = final_norm[None, :]
    past = (cache_mla_latent[li], cache_mla_krope[li], cache_dsa_k[li], cache_dsa_v[li], cache_idx_k[li])
    y_p, new_p = _layer(x_prompt, 0, None, weights, norms, fin)
    y_s, new_s = _layer(x_sample, past_len, past, weights, norms, fin)
    return (y_p, y_s) + tuple(a[None] for a in new_p) + tuple(a[None] for a in new_s)
```

```python
import functools

import numpy as np
import jax
import jax.numpy as jnp
from jax import lax
from jax.experimental import pallas as pl
from jax.experimental.pallas import tpu as pltpu

D_MODEL = 1024
CHUNK = 64
ROPE_THETA = 500000.0
EPS = 1e-6
N_HEADS_A = 8
NOPE_DIM = 64
ROPE_DIM = 32
V_DIM_A = 64
Q_LORA = 384
KV_LORA = 256
N_HEADS_B = 8
N_KV_B = 2
HEAD_DIM_B = 64
ROT_DIM_B = 16
N_IDX_HEADS = 8
IDX_DIM = 64
ROT_DIM_IDX = 16
TOPK_MAX = 256
D_FF = 2816
IN_SIZES = (Q_LORA, KV_LORA, ROPE_DIM, N_HEADS_B * HEAD_DIM_B, N_KV_B * HEAD_DIM_B, N_KV_B * HEAD_DIM_B,
            N_IDX_HEADS * IDX_DIM, IDX_DIM, N_IDX_HEADS)

LANES = 128
SUBLANES = 8
MXU_DTYPE = jnp.bfloat16
VMEM_LIMIT = 56 * 1024 * 1024
NEG_BIG = -1e30
LOWEST = -3.0e38
FLT_MIN = 1.1754943508222875e-38
LOG2E = 1.4426950408889634

TM = 512
TQ_MLA = 512
TK_MLA = 512
TQ_DSA = 256
TK_DSA = 512
MLA_HEADS_PER_STEP = 4

_C_QLAT = 0
_C_KVLAT = _C_QLAT + Q_LORA
_C_QB = _C_KVLAT + KV_LORA
_C_IQ = _C_QB + N_HEADS_B * LANES
_C_KB = _C_IQ + N_IDX_HEADS * LANES
_C_VB = _C_KB + LANES
_C_IK = _C_VB + LANES
_C_KR = _C_IK + LANES
_C_IW = _C_KR + LANES
_C_END = _C_IW + LANES


def _cparams(n_axes):
    return pltpu.CompilerParams(dimension_semantics=("arbitrary",) * n_axes, vmem_limit_bytes=VMEM_LIMIT)


def _const_spec(shape):
    return pl.BlockSpec(shape, lambda *_: (0,) * len(shape), pipeline_mode=pl.Buffered(1))


def _rms(x, g):
    return (x * lax.rsqrt(jnp.mean(x * x, axis=-1, keepdims=True) + EPS)) * g


def _rope_tile(x, cos, sin, half, first_half):
    partner = jnp.where(first_half, pltpu.roll(x, LANES - half, 1), pltpu.roll(x, half, 1))
    return x * cos + partner * sin


def _proj_kernel(x_ref, cq_ref, sq_ref, c64_ref, s64_ref, gat_ref, win_ref, gq_ref, wqb_ref, gkv_ref,
                 ckv_ref, kr_ref, kb_ref, vb_ref, ik_ref, iw_ref, qm_ref, qb_ref, iq_ref,
                 kb16_ref, vb16_ref, ik16_ref):
    x = x_ref[...]
    hb = _rms(x, gat_ref[...]).astype(MXU_DTYPE)
    y = jnp.dot(hb, win_ref[...], preferred_element_type=jnp.float32)

    lane = lax.broadcasted_iota(jnp.int32, (1, LANES), 1)
    first_q = lane < NOPE_DIM + ROPE_DIM // 2
    first_64 = (lane & (HEAD_DIM_B - 1)) < ROT_DIM_B // 2
    cq, sq, c64, s64 = cq_ref[...], sq_ref[...], c64_ref[...], s64_ref[...]

    qn = _rms(y[:, _C_QLAT:_C_QLAT + Q_LORA], gq_ref[...]).astype(MXU_DTYPE)
    qa = jnp.dot(qn, wqb_ref[...], preferred_element_type=jnp.float32)
    scale_a = (NOPE_DIM + ROPE_DIM) ** -0.5 * LOG2E
    for h in range(N_HEADS_A):
        t = _rope_tile(qa[:, h * LANES:(h + 1) * LANES], cq, sq, ROPE_DIM // 2, first_q)
        qm_ref[:, h * LANES:(h + 1) * LANES] = (t * scale_a).astype(qm_ref.dtype)

    ckv_ref[...] = _rms(y[:, _C_KVLAT:_C_KVLAT + KV_LORA], gkv_ref[...])
    kr_ref[...] = _rope_tile(y[:, _C_KR:_C_KR + LANES], cq, sq, ROPE_DIM // 2, first_q)

    scale_b = HEAD_DIM_B ** -0.5 * LOG2E
    for h in range(N_HEADS_B):
        t = _rope_tile(y[:, _C_QB + h * LANES:_C_QB + (h + 1) * LANES], c64, s64, ROT_DIM_B // 2, first_64)
        qb_ref[:, h * LANES:(h + 1) * LANES] = (t * scale_b).astype(qb_ref.dtype)
    for h in range(N_IDX_HEADS):
        t = _rope_tile(y[:, _C_IQ + h * LANES:_C_IQ + (h + 1) * LANES], c64, s64, ROT_DIM_IDX // 2, first_64)
        iq_ref[:, h * LANES:(h + 1) * LANES] = t.astype(iq_ref.dtype)

    kb = _rope_tile(y[:, _C_KB:_C_KB + LANES], c64, s64, ROT_DIM_B // 2, first_64)
    kb_ref[...] = kb
    kb16_ref[...] = kb.astype(kb16_ref.dtype)
    vb = y[:, _C_VB:_C_VB + LANES]
    vb_ref[...] = vb
    vb16_ref[...] = vb.astype(vb16_ref.dtype)
    ik = _rope_tile(y[:, _C_IK:_C_IK + LANES], c64, s64, ROT_DIM_IDX // 2, first_64)
    ik_ref[...] = ik
    ik16_ref[...] = ik.astype(ik16_ref.dtype)
    iw_ref[...] = y[:, _C_IW:_C_IW + LANES] * ((N_IDX_HEADS * IDX_DIM) ** -0.5)


def _proj_call(x2d, tabs, n_tab_tiles, tm, g_attn, w_in_p, g_q, w_qb_p, g_kv):
    t_tok = x2d.shape[0]
    grid = (t_tok // tm,)
    row = lambda w: pl.BlockSpec((tm, w), lambda i: (i, 0))
    tab = pl.BlockSpec((tm, LANES), lambda i: (i % n_tab_tiles, 0))
    f32, b16 = jnp.float32, MXU_DTYPE
    out_shapes = [
        jax.ShapeDtypeStruct((t_tok, KV_LORA), f32),
        jax.ShapeDtypeStruct((t_tok, LANES), f32),
        jax.ShapeDtypeStruct((t_tok, LANES), f32),
        jax.ShapeDtypeStruct((t_tok, LANES), f32),
        jax.ShapeDtypeStruct((t_tok, LANES), f32),
        jax.ShapeDtypeStruct((t_tok, LANES), f32),
        jax.ShapeDtypeStruct((t_tok, N_HEADS_A * LANES), b16),
        jax.ShapeDtypeStruct((t_tok, N_HEADS_B * LANES), b16),
        jax.ShapeDtypeStruct((t_tok, N_IDX_HEADS * LANES), b16),
        jax.ShapeDtypeStruct((t_tok, LANES), b16),
        jax.ShapeDtypeStruct((t_tok, LANES), b16),
        jax.ShapeDtypeStruct((t_tok, LANES), b16),
    ]
    out_specs = [row(s.shape[1]) for s in out_shapes]
    in_specs = [row(D_MODEL), tab, tab, tab, tab, _const_spec((1, D_MODEL)), _const_spec(w_in_p.shape),
                _const_spec((1, Q_LORA)), _const_spec(w_qb_p.shape), _const_spec((1, KV_LORA))]
    return pl.pallas_call(
        _proj_kernel, grid=grid, in_specs=in_specs, out_specs=out_specs, out_shape=out_shapes,
        compiler_params=_cparams(1), name="proj",
    )(x2d, *tabs, g_attn, w_in_p, g_q, w_qb_p, g_kv)


_NT = (((1,), (1,)), ((), ()))


def _kvup_kernel(c_ref, kr_ref, wk_ref, wvt_ref, k_ref, vt_ref):
    cb = c_ref[...].astype(MXU_DTYPE)
    kn = jnp.dot(cb, wk_ref[...], preferred_element_type=jnp.float32)
    kr = kr_ref[...]
    for h in range(N_HEADS_A):
        k_ref[h] = (kn[:, h * LANES:(h + 1) * LANES] + kr).astype(k_ref.dtype)
    vt_ref[...] = lax.dot_general(wvt_ref[...], cb, _NT, preferred_element_type=jnp.float32).astype(vt_ref.dtype)


def _kvup_call(c3d, kr3d, wk, wvt, tk):
    b, s_pad, _ = c3d.shape
    row = lambda w: pl.BlockSpec((None, tk, w), lambda bi, i: (bi, i, 0))
    vdim = N_HEADS_A * V_DIM_A
    return pl.pallas_call(
        _kvup_kernel, grid=(b, s_pad // tk),
        in_specs=[row(KV_LORA), row(LANES), _const_spec(wk.shape), _const_spec(wvt.shape)],
        out_specs=[pl.BlockSpec((None, N_HEADS_A, tk, LANES), lambda bi, i: (bi, 0, i, 0)),
                   pl.BlockSpec((None, None, vdim, tk), lambda bi, i: (bi, i, 0, 0))],
        out_shape=[jax.ShapeDtypeStruct((b, N_HEADS_A, s_pad, LANES), MXU_DTYPE),
                   jax.ShapeDtypeStruct((b, s_pad // tk, vdim, tk), MXU_DTYPE)],
        compiler_params=_cparams(2), name="kvup",
    )(c3d, kr3d, wk, wvt)


_CHUNK_SHIFT = CHUNK.bit_length() - 1


def _admissible_t(p0, kb, tq, tk, s_len):
    kpos = kb * tk + lax.broadcasted_iota(jnp.int32, (tk, 1), 0)
    qpos = p0 + lax.broadcasted_iota(jnp.int32, (1, tq), 1)
    return ((qpos >> _CHUNK_SHIFT) >= (kpos >> _CHUNK_SHIFT)) & (kpos < s_len)


def _check_tiles(tq, tk, q_off, lq):
    assert tk & (tk - 1) == 0 and tk % tq == 0 and tk % CHUNK == 0, (tq, tk)
    assert q_off % tq == 0 or (lq == tq and q_off % tk + tq <= tk), (tq, tk, q_off, lq)


def _last_block(p0, tq, tk, s_len):
    return jnp.minimum(p0 + tq - 1, s_len - 1) >> (tk.bit_length() - 1)


def _fori_grouped(n, body, carry, group):
    shift = group.bit_length() - 1

    def several(j, c):
        for i in range(group):
            c = body(group * j + i, c)
        return c

    carry = lax.fori_loop(0, n >> shift, several, carry)
    return lax.fori_loop((n >> shift) << shift, n, body, carry)


def _softmax_init(n_heads, tq, acc_ref):
    acc_ref[...] = jnp.zeros(acc_ref.shape, jnp.float32)
    return jnp.full((n_heads, tq), NEG_BIG, jnp.float32), jnp.zeros((n_heads, tq), jnp.float32)


def _attn_step(state, values, next_scores, bias, s_ref, acc_ref):
    (m_prev, l_prev), maxima = state
    n_heads = s_ref.shape[0]
    sums, next_max = [], []
    if values is not None:
        m_new = jnp.maximum(m_prev, maxima)
        alpha = jnp.exp2(m_prev - m_new)
    for h in range(n_heads):
        if values is not None:
            p = jnp.exp2(s_ref[h] - m_new[h:h + 1, :])
            sums.append(jnp.sum(p, axis=0, keepdims=True))
            acc_ref[h] = alpha[h:h + 1, :] * acc_ref[h] + jnp.dot(values[h], p.astype(values[h].dtype),
                                                                   preferred_element_type=jnp.float32)
        if next_scores is not None:
            s = next_scores(h)
            if bias is not None:
                s = s + bias
            s_ref[h] = s
            next_max.append(jnp.max(s, axis=0, keepdims=True))
    carry = (m_new, alpha * l_prev + jnp.concatenate(sums, axis=0)) if values is not None else (m_prev, l_prev)
    return carry, (jnp.concatenate(next_max, axis=0) if next_scores is not None else maxima)


def _softmax_finish(l, acc_ref, o_ref):
    inv = 1.0 / l
    o_t = jnp.concatenate([acc_ref[h] * inv[h:h + 1, :] for h in range(l.shape[0])], axis=0)
    o_ref[...] = o_t.T.astype(o_ref.dtype)


def _mla_kernel(q_ref, k_ref, vt_ref, o_ref, s_ref, acc_ref, *, tq, tk, q_off, s_len):
    p0 = q_off + pl.program_id(2) * tq
    last = _last_block(p0, tq, tk, s_len)
    heads = range(MLA_HEADS_PER_STEP)

    def qk(kb):
        start = pl.multiple_of(kb * tk, tk)
        return lambda h: lax.dot_general(k_ref[h, pl.ds(start, tk), :],
                                         q_ref[:, h * LANES:(h + 1) * LANES], _NT,
                                         preferred_element_type=jnp.float32)

    def mask_bias(kb):
        return jnp.where(_admissible_t(p0, kb, tq, tk, s_len), 0.0, NEG_BIG)

    def values(kb):
        vt_all = vt_ref[kb]
        return [vt_all[h * V_DIM_A:(h + 1) * V_DIM_A, :] for h in heads]

    def body(kb, state):
        return _attn_step(state, values(kb), qk(kb + 1), None, s_ref, acc_ref)

    def tail(state):
        return _attn_step(state, values(last - 1), qk(last), mask_bias(last), s_ref, acc_ref)

    state = _attn_step((_softmax_init(MLA_HEADS_PER_STEP, tq, acc_ref), None), None, qk(0), mask_bias(0), s_ref,
                       acc_ref)
    state = lax.fori_loop(0, jnp.maximum(last - 1, 0), body, state)
    state = lax.cond(last > 0, tail, lambda st: st, state)
    (_, l), _ = _attn_step(state, values(last), None, None, s_ref, acc_ref)
    _softmax_finish(l, acc_ref, o_ref)


def _mla_call(q, k, vt, *, tq, tk, q_off, s_len):
    b, lq, _ = q.shape
    _check_tiles(tq, tk, q_off, lq)
    s_pad = k.shape[2]
    hg = MLA_HEADS_PER_STEP
    kern = functools.partial(_mla_kernel, tq=tq, tk=tk, q_off=q_off, s_len=s_len)
    once = dict(pipeline_mode=pl.Buffered(1))
    return pl.pallas_call(
        kern, grid=(b, N_HEADS_A // hg, lq // tq),
        in_specs=[pl.BlockSpec((None, tq, hg * LANES), lambda bi, g, qi: (bi, qi, g)),
                  pl.BlockSpec((None, hg, s_pad, LANES), lambda bi, g, qi: (bi, g, 0, 0), **once),
                  pl.BlockSpec((None, s_pad // tk, hg * V_DIM_A, tk), lambda bi, g, qi: (bi, 0, g, 0), **once)],
        out_specs=pl.BlockSpec((None, tq, hg * V_DIM_A), lambda bi, g, qi: (bi, qi, g)),
        out_shape=jax.ShapeDtypeStruct((b, lq, N_HEADS_A * V_DIM_A), MXU_DTYPE),
        scratch_shapes=[pltpu.VMEM((hg, tk, tq), jnp.float32), pltpu.VMEM((hg, V_DIM_A, tq), jnp.float32)],
        compiler_params=_cparams(3), name="mla",
    )(q, k, vt)


def _f2key(x):
    b = lax.bitcast_convert_type(x, jnp.int32)
    return jnp.where(b < 0, b ^ jnp.int32(0x7FFFFFFF), b)


def _key2f(k):
    return lax.bitcast_convert_type(jnp.where(k < 0, k ^ jnp.int32(0x7FFFFFFF), k), jnp.float32)


_COUNT_SLAB = 4 * SUBLANES
_SCORE_GROUP = 4
_COUNT_GROUP = 4
_SEARCH_BISECT_EVERY = 8
_SEARCH_MAX_PROBES = 40 * _SEARCH_BISECT_EVERY


def _dsa_kernel(iq_ref, qb_ref, iw_ref, ik_ref, kb_ref, vbt_ref, tri_ref, o_ref,
                sc_ref, wt_ref, s_ref, acc_ref, *, tq, tk, q_off, s_len, topk):
    p0 = q_off + pl.program_id(1) * tq
    last = _last_block(p0, tq, tk, s_len)
    n_blk = last + 1
    row = lambda v: jnp.full((1, tq), v, jnp.float32)
    colsum = lambda x: jnp.sum(x, axis=0, keepdims=True)

    wt_ref[...] = iw_ref[...].T[:SUBLANES, :]

    def score_block(kb):
        start = pl.multiple_of(kb * tk, tk)
        ikb = ik_ref[pl.ds(start, tk), :]
        acc = jnp.zeros((tk, tq), jnp.float32)
        for h in range(N_IDX_HEADS):
            x = lax.dot_general(ikb, iq_ref[:, h * LANES:(h + 1) * LANES], _NT, preferred_element_type=jnp.float32)
            acc = acc + wt_ref[h:h + 1, :] * jnp.maximum(x, 0.0)
        return acc

    def stats(carry, hi_fill, lo_fill):
        mx, mn, n_pos, n_nonneg = carry
        return (jnp.maximum(mx, jnp.max(hi_fill, axis=0, keepdims=True)),
                jnp.minimum(mn, jnp.min(lo_fill, axis=0, keepdims=True)),
                n_pos + colsum(jnp.where(hi_fill >= FLT_MIN, 1.0, 0.0)),
                n_nonneg + colsum(jnp.where(hi_fill >= 0.0, 1.0, 0.0)))

    def p1_body(kb, carry):
        acc = score_block(kb)
        sc_ref[kb] = acc
        return stats(carry, acc, acc)

    carry = _fori_grouped(last, p1_body, (row(-jnp.inf), row(jnp.inf), row(0.0), row(0.0)), _SCORE_GROUP)
    acc = score_block(last)
    adm = _admissible_t(p0, last, tq, tk, s_len)
    masked = jnp.where(adm, acc, -jnp.inf)
    sc_ref[last] = masked
    mx, mn, n_pos, n_nonneg = stats(carry, masked, jnp.where(adm, acc, jnp.inf))

    qpos = p0 + lax.broadcasted_iota(jnp.int32, (1, tq), 1)
    n_adm = jnp.minimum(((qpos >> _CHUNK_SHIFT) + 1) * CHUNK, s_len).astype(jnp.float32)
    k_f = jnp.float32(topk)
    log_k = jnp.log(k_f)
    phi = lambda c: jnp.log(jnp.maximum(c, 0.25)) - log_k

    few = n_adm <= k_f
    pos = n_pos >= k_f
    at_zero = jnp.logical_and(~pos, n_nonneg >= k_f)
    above_max = _key2f(_f2key(jnp.maximum(mx, LOWEST)) + 1)
    lo = jnp.where(pos, FLT_MIN, LOWEST)
    c_lo = jnp.where(pos, n_pos, n_adm)
    hi = jnp.where(pos | at_zero, above_max, 0.0)
    c_hi = jnp.where(pos | at_zero, 0.0, n_nonneg)
    lo_val = jnp.where(pos, 0.0, jnp.maximum(mn, LOWEST))
    tau = jnp.where(few, LOWEST, jnp.where(at_zero, 0.0, lo))
    r = jnp.where(at_zero & ~few, k_f - n_pos, 0.0)
    done = jnp.where(few | at_zero, 1.0, 0.0)

    def count_ge(g):
        def body(kb, cnt):
            hit = jnp.where(sc_ref[kb] >= g, 1.0, 0.0)
            return cnt + jnp.sum(hit.reshape(tk // _COUNT_SLAB, _COUNT_SLAB, tq), axis=0)

        return colsum(_fori_grouped(n_blk, body, jnp.zeros((_COUNT_SLAB, tq), jnp.float32), _COUNT_GROUP))

    def search_cond(st):
        return jnp.logical_and(st[1] > 0, st[0] < _SEARCH_MAX_PROBES)

    def next_probe(it, lo, hi, f_lo, f_hi, lo_val):
        lok, hik = _f2key(lo), _f2key(hi)
        midk = (lok >> 1) + (hik >> 1) + (lok & hik & 1)
        frac = f_lo / jnp.maximum(f_lo - f_hi, 1e-9)
        g_int = lo_val + (hi - lo_val) * frac
        bisect = (it % _SEARCH_BISECT_EVERY) == (_SEARCH_BISECT_EVERY - 1)
        g = jnp.where(bisect, _key2f(midk), g_int)
        gk = jnp.minimum(jnp.maximum(_f2key(g), lok + 1), jnp.maximum(hik - 1, lok + 1))
        return _key2f(gk)

    def search_body(st):
        it, _, g, lo, hi, c_lo, c_hi, f_lo, f_hi, lo_val, side, tau, r, done = st
        c = count_ge(g)
        f_g = phi(c)
        live = done == 0.0
        up = live & (c >= k_f)
        dn = live & (c < k_f)
        f_hi = jnp.where(up & (side > 0.0), f_hi * 0.5, f_hi)
        f_lo = jnp.where(dn & (side < 0.0), f_lo * 0.5, f_lo)
        lo, c_lo, f_lo, lo_val = (jnp.where(up, g, lo), jnp.where(up, c, c_lo), jnp.where(up, f_g, f_lo),
                                  jnp.where(up, g, lo_val))
        hi, c_hi, f_hi = jnp.where(dn, g, hi), jnp.where(dn, c, c_hi), jnp.where(dn, f_g, f_hi)
        side = jnp.where(up, 1.0, jnp.where(dn, -1.0, side))
        adjacent = (_f2key(lo) + 1) >= _f2key(hi)
        fin = live & ((c_lo == k_f) | adjacent)
        tau = jnp.where(fin, lo, tau)
        r = jnp.where(fin & (c_lo > k_f), k_f - c_hi, r)
        done = jnp.where(fin, 1.0, done)
        pending = jnp.sum(1.0 - done).astype(jnp.int32)
        g = next_probe(it + 1, lo, hi, f_lo, f_hi, lo_val)
        return (it + 1, pending, g, lo, hi, c_lo, c_hi, f_lo, f_hi, lo_val, side, tau, r, done)

    pending0 = jnp.sum(1.0 - done).astype(jnp.int32)
    f_lo0, f_hi0 = phi(c_lo), phi(c_hi)
    st = lax.while_loop(search_cond, search_body,
                        (jnp.int32(0), pending0, next_probe(0, lo, hi, f_lo0, f_hi0, lo_val), lo, hi, c_lo, c_hi,
                         f_lo0, f_hi0, lo_val, row(0.0), tau, r, done))
    tau, r = st[11], st[12]

    @pl.when(jnp.sum(r) > 0.0)
    def _():
        tied = r > 0.0
        budget = jnp.where(tied, r, jnp.inf)

        def body(kb, seen):
            blk = sc_ref[kb]
            eq = jnp.logical_and(blk == tau, tied)
            eq_f = jnp.where(eq, 1.0, 0.0)
            rank = jnp.dot(tri_ref[...], eq_f.astype(MXU_DTYPE), preferred_element_type=jnp.float32) + seen
            sc_ref[kb] = jnp.where(jnp.logical_and(eq, rank > budget), -jnp.inf, blk)
            return seen + colsum(eq_f)

        _fori_grouped(n_blk, body, row(0.0), _COUNT_GROUP)

    heads_per_kv = N_HEADS_B // N_KV_B

    def qk(kb):
        start = pl.multiple_of(kb * tk, tk)
        kblk = kb_ref[pl.ds(start, tk), :]
        return lambda h: lax.dot_general(kblk, qb_ref[:, h * LANES:(h + 1) * LANES], _NT,
                                         preferred_element_type=jnp.float32)

    def sel_bias(kb):
        return jnp.where(sc_ref[kb] >= tau, 0.0, NEG_BIG)

    def values(kb):
        vt_all = vbt_ref[kb]
        return [vt_all[(h // heads_per_kv) * HEAD_DIM_B:(h // heads_per_kv + 1) * HEAD_DIM_B, :]
                for h in range(N_HEADS_B)]

    def p3_body(kb, state):
        return _attn_step(state, values(kb), qk(kb + 1), sel_bias(kb + 1), s_ref, acc_ref)

    state = _attn_step((_softmax_init(N_HEADS_B, tq, acc_ref), None), None, qk(0), sel_bias(0), s_ref, acc_ref)
    state = lax.fori_loop(0, last, p3_body, state)
    (_, l), _ = _attn_step(state, values(last), None, None, s_ref, acc_ref)
    _softmax_finish(l, acc_ref, o_ref)


def _dsa_call(iq, qb, iw, ik, kb, vbt, tri, *, tq, tk, q_off, s_len):
    b, lq, _ = iq.shape
    _check_tiles(tq, tk, q_off, lq)
    s_pad = ik.shape[1]
    n_kb = s_pad // tk
    topk = min(TOPK_MAX, s_len // 4)
    kern = functools.partial(_dsa_kernel, tq=tq, tk=tk, q_off=q_off, s_len=s_len, topk=topk)
    qspec = lambda w: pl.BlockSpec((None, tq, w), lambda bi, qi: (bi, qi, 0))
    once = dict(pipeline_mode=pl.Buffered(1))
    kspec = pl.BlockSpec((None, s_pad, LANES), lambda bi, qi: (bi, 0, 0), **once)
    return pl.pallas_call(
        kern, grid=(b, lq // tq),
        in_specs=[qspec(N_IDX_HEADS * LANES), qspec(N_HEADS_B * LANES), qspec(LANES), kspec, kspec,
                  pl.BlockSpec((None, n_kb, LANES, tk), lambda bi, qi: (bi, 0, 0, 0), **once),
                  _const_spec(tri.shape)],
        out_specs=qspec(N_HEADS_B * HEAD_DIM_B),
        out_shape=jax.ShapeDtypeStruct((b, lq, N_HEADS_B * HEAD_DIM_B), MXU_DTYPE),
        scratch_shapes=[pltpu.VMEM((n_kb, tk, tq), jnp.float32), pltpu.VMEM((SUBLANES, tq), jnp.float32),
                        pltpu.VMEM((N_HEADS_B, tk, tq), jnp.float32),
                        pltpu.VMEM((N_HEADS_B, HEAD_DIM_B, tq), jnp.float32)],
        compiler_params=_cparams(2), name="dsa",
    )(iq, qb, iw, ik, kb, vbt, tri)


_FF_CHUNK = 256


def _out_kernel(x_ref, oa_ref, ob_ref, woa_ref, wob_ref, gf_ref, wg_ref, wu_ref, wd_ref, gfin_ref, y_ref):
    o = (jnp.dot(oa_ref[...], woa_ref[...], preferred_element_type=jnp.float32)
         + jnp.dot(ob_ref[...], wob_ref[...], preferred_element_type=jnp.float32))
    x1 = x_ref[...] + o
    h2 = _rms(x1, gf_ref[...]).astype(MXU_DTYPE)
    ffn = jnp.zeros_like(x1)
    for j in range(D_FF // _FF_CHUNK):
        sl = slice(j * _FF_CHUNK, (j + 1) * _FF_CHUNK)
        gate = jnp.dot(h2, wg_ref[:, sl], preferred_element_type=jnp.float32)
        up = jnp.dot(h2, wu_ref[:, sl], preferred_element_type=jnp.float32)
        act = (gate * jax.nn.sigmoid(gate)) * up
        ffn = ffn + jnp.dot(act.astype(MXU_DTYPE), wd_ref[sl, :], preferred_element_type=jnp.float32)
    y_ref[...] = _rms(x1 + ffn, gfin_ref[...])


def _out_call(x2d, oa, ob, woa, wob, g_ffn, wg, wu, wd, g_fin, tm):
    t_tok = x2d.shape[0]
    row = lambda w: pl.BlockSpec((tm, w), lambda i: (i, 0))
    half = N_HEADS_A * V_DIM_A
    return pl.pallas_call(
        _out_kernel, grid=(t_tok // tm,),
        in_specs=[row(D_MODEL), row(half), row(half), _const_spec(woa.shape), _const_spec(wob.shape),
                  _const_spec((1, D_MODEL)), _const_spec(wg.shape), _const_spec(wu.shape), _const_spec(wd.shape),
                  _const_spec((1, D_MODEL))],
        out_specs=row(D_MODEL), out_shape=jax.ShapeDtypeStruct((t_tok, D_MODEL), jnp.float32),
        compiler_params=_cparams(1), name="out",
    )(x2d, oa, ob, woa, wob, g_ffn, wg, wu, wd, g_fin)


def _prep_weights(w_in, w_q_b, w_kv_b, w_o, w_gate, w_up, w_down):
    off = np.concatenate([[0], np.cumsum(IN_SIZES)])
    o_qlat, o_kvlat, o_kr, o_qb, o_kb, o_vb, o_iq, o_ik, o_iw = off[:9]
    zeros = lambda n: jnp.zeros((D_MODEL, n), w_in.dtype)
    cols = [w_in[:, o_qlat:o_qlat + Q_LORA], w_in[:, o_kvlat:o_kvlat + KV_LORA]]
    for h in range(N_HEADS_B):
        head = w_in[:, o_qb + h * HEAD_DIM_B:o_qb + (h + 1) * HEAD_DIM_B]
        cols += [head, zeros(HEAD_DIM_B)] if h < N_HEADS_B // N_KV_B else [zeros(HEAD_DIM_B), head]
    for h in range(N_IDX_HEADS):
        cols += [w_in[:, o_iq + h * IDX_DIM:o_iq + (h + 1) * IDX_DIM], zeros(LANES - IDX_DIM)]
    cols += [w_in[:, o_kb:o_kb + LANES], w_in[:, o_vb:o_vb + LANES],
             w_in[:, o_ik:o_ik + IDX_DIM], zeros(LANES - IDX_DIM),
             zeros(NOPE_DIM), w_in[:, o_kr:o_kr + ROPE_DIM], zeros(LANES - NOPE_DIM - ROPE_DIM),
             w_in[:, o_iw:o_iw + N_IDX_HEADS], zeros(LANES - N_IDX_HEADS)]
    w_in_p = jnp.concatenate(cols, axis=1).astype(MXU_DTYPE)
    assert w_in_p.shape[1] == _C_END

    qd = NOPE_DIM + ROPE_DIM
    w_qb_p = jnp.pad(w_q_b.reshape(Q_LORA, N_HEADS_A, qd), ((0, 0), (0, 0), (0, LANES - qd)))
    w_qb_p = w_qb_p.reshape(Q_LORA, N_HEADS_A * LANES).astype(MXU_DTYPE)

    kv = w_kv_b.reshape(KV_LORA, N_HEADS_A, NOPE_DIM + V_DIM_A)
    wk = jnp.pad(kv[:, :, :NOPE_DIM], ((0, 0), (0, 0), (0, LANES - NOPE_DIM)))
    wk = wk.reshape(KV_LORA, N_HEADS_A * LANES).astype(MXU_DTYPE)
    wvt = kv[:, :, NOPE_DIM:].reshape(KV_LORA, N_HEADS_A * V_DIM_A).T.astype(MXU_DTYPE)

    half = N_HEADS_A * V_DIM_A
    return (w_in_p, w_qb_p, wk, wvt, w_o[:half].astype(MXU_DTYPE), w_o[half:].astype(MXU_DTYPE),
            w_gate.astype(MXU_DTYPE), w_up.astype(MXU_DTYPE), w_down.astype(MXU_DTYPE))


def _rope_tables(pos):
    posf = pos.astype(jnp.float32)[:, None]
    n = pos.shape[0]

    def cs(d):
        inv = jnp.power(ROPE_THETA, -jnp.arange(0, d, 2, dtype=jnp.float32) / d)
        ang = posf * inv[None, :]
        return jnp.cos(ang), jnp.sin(ang)

    one, zero = (lambda w: jnp.ones((n, w), jnp.float32)), (lambda w: jnp.zeros((n, w), jnp.float32))
    c, s = cs(ROPE_DIM)
    cq = jnp.concatenate([one(NOPE_DIM), c, c, one(LANES - NOPE_DIM - ROPE_DIM)], axis=1)
    sq = jnp.concatenate([zero(NOPE_DIM), -s, s, zero(LANES - NOPE_DIM - ROPE_DIM)], axis=1)
    c, s = cs(ROT_DIM_B)
    c64 = jnp.tile(jnp.concatenate([c, c, one(HEAD_DIM_B - ROT_DIM_B)], axis=1), (1, LANES // HEAD_DIM_B))
    s64 = jnp.tile(jnp.concatenate([-s, s, zero(HEAD_DIM_B - ROT_DIM_B)], axis=1), (1, LANES // HEAD_DIM_B))
    return cq, sq, c64, s64


def _pick_tile(n, pref):
    t = min(pref, n)
    while n % t:
        t //= 2
    return t


def _round_up(n, m):
    return -(-n // m) * m


def _layer(x, pos0, past, weights, norms, final_norm):
    b, l, _ = x.shape
    g_attn, g_q, g_kv, g_ffn = norms
    w_in_p, w_qb_p, wk, wvt, woa, wob, wg, wu, wd = weights
    t_tok = b * l
    x2d = x.reshape(t_tok, D_MODEL)

    tabs = _rope_tables(pos0 + jnp.arange(l))
    if l >= TM // 2:
        tm = _pick_tile(l, TM)
        n_tab_tiles = l // tm
    else:
        tm = _pick_tile(t_tok, TM)
        tabs = tuple(jnp.tile(t, (tm // l, 1)) for t in tabs)
        n_tab_tiles = 1
    (ckv, kr128, kb, vb, ik128, iw, q_mla, q_b, iq, kb16, vb16, ik16) = _proj_call(
        x2d, tabs, n_tab_tiles, tm, g_attn, w_in_p, g_q, w_qb_p, g_kv)

    new = (ckv.reshape(b, l, KV_LORA), kr128.reshape(b, l, LANES)[..., NOPE_DIM:NOPE_DIM + ROPE_DIM],
           kb.reshape(b, l, N_KV_B, HEAD_DIM_B), vb.reshape(b, l, N_KV_B, HEAD_DIM_B),
           ik128.reshape(b, l, LANES)[..., :IDX_DIM])

    per_row = lambda a: a.reshape(b, l, a.shape[-1])
    p_len = 0 if past is None else past[0].shape[1]
    s_len = p_len + l
    lq = _round_up(l, LANES)
    tq_mla, tq_dsa = _pick_tile(lq, TQ_MLA), _pick_tile(lq, TQ_DSA)
    tk_mla, tk_dsa = (2 * TK_MLA, 2 * TK_DSA) if lq < TQ_DSA else (TK_MLA, TK_DSA)
    s_pad = _round_up(s_len, max(tk_mla, tk_dsa))
    if past is None:
        old = [None] * 5
    else:
        p_lat, p_kr, p_k, p_v, p_ik = past
        old = [p_lat,
               jnp.pad(p_kr, ((0, 0), (0, 0), (NOPE_DIM, LANES - NOPE_DIM - ROPE_DIM))),
               p_k.reshape(b, p_len, LANES).astype(MXU_DTYPE), p_v.reshape(b, p_len, LANES).astype(MXU_DTYPE),
               jnp.pad(p_ik, ((0, 0), (0, 0), (0, LANES - IDX_DIM))).astype(MXU_DTYPE)]

    def full_seq(prev, cur):
        parts = ([] if prev is None else [prev]) + [per_row(cur)]
        if s_pad > s_len:
            parts.append(jnp.zeros((b, s_pad - s_len, cur.shape[-1]), cur.dtype))
        return parts[0] if len(parts) == 1 else jnp.concatenate(parts, axis=1)

    c_all, kr_all, kb_all, vb_all, ik_all = (full_seq(o, c) for o, c in zip(old, (ckv, kr128, kb16, vb16, ik16)))
    vbt_all = vb_all.reshape(b, s_pad // tk_dsa, tk_dsa, LANES).transpose(0, 1, 3, 2)

    k_mla, vt_mla = _kvup_call(c_all, kr_all, wk, wvt, tk_mla)

    padq = lambda a: per_row(a) if lq == l else jnp.pad(per_row(a), ((0, 0), (0, lq - l), (0, 0)))
    o_a = _mla_call(padq(q_mla), k_mla, vt_mla, tq=tq_mla, tk=tk_mla, q_off=pos0, s_len=s_len)
    tri = jnp.tril(jnp.ones((tk_dsa, tk_dsa), MXU_DTYPE))
    o_b = _dsa_call(padq(iq), padq(q_b), padq(iw), ik_all, kb_all, vbt_all, tri,
                    tq=tq_dsa, tk=tk_dsa, q_off=pos0, s_len=s_len)
    o_a, o_b = (o[:, :l].reshape(t_tok, -1) for o in (o_a, o_b))

    y = _out_call(x2d, o_a, o_b, woa, wob, g_ffn, wg, wu, wd, final_norm, tm)
    return y.reshape(b, l, D_MODEL), new


def kernel(x_prompt, x_sample, cache_mla_latent, cache_mla_krope, cache_dsa_k, cache_dsa_v, cache_idx_k, attn_norm, w_in, q_a_norm, w_q_b, kv_a_norm, w_kv_b, w_o, ffn_norm, w_gate, w_up, w_down, final_norm):
    depth = w_in.shape[0]
    assert depth == 1, "the fused final norm assumes a single layer"
    past_len = cache_mla_latent.shape[2]
    li = 0
    weights = _prep_weights(w_in[li], w_q_b[li], w_kv_b[li], w_o[li], w_gate[li], w_up[li], w_down[li])
    norms = (attn_norm[li][None, :], q_a_norm[li][None, :], kv_a_norm[li][None, :], ffn_norm[li][None, :])
    fin = final_norm[None, :]
    past = (cache_mla_latent[li], cache_mla_krope[li], cache_dsa_k[li], cache_dsa_v[li], cache_idx_k[li])
    y_p, new_p = _layer(x_prompt, 0, None, weights, norms, fin)
    y_s, new_s = _layer(x_sample, past_len, past, weights, norms, fin)
    return (y_p, y_s) + tuple(a[None] for a in new_p) + tuple(a[None] for a in new_s)
```

```python
import functools

import numpy as np
import jax
import jax.numpy as jnp
from jax import lax
from jax.experimental import pallas as pl
from jax.experimental.pallas import tpu as pltpu

D_MODEL = 1024
CHUNK = 64
ROPE_THETA = 500000.0
EPS = 1e-6
N_HEADS_A = 8
NOPE_DIM = 64
ROPE_DIM = 32
V_DIM_A = 64
Q_LORA = 384
KV_LORA = 256
N_HEADS_B = 8
N_KV_B = 2
HEAD_DIM_B = 64
ROT_DIM_B = 16
N_IDX_HEADS = 8
IDX_DIM = 64
ROT_DIM_IDX = 16
TOPK_MAX = 256
D_FF = 2816
IN_SIZES = (Q_LORA, KV_LORA, ROPE_DIM, N_HEADS_B * HEAD_DIM_B, N_KV_B * HEAD_DIM_B, N_KV_B * HEAD_DIM_B,
            N_IDX_HEADS * IDX_DIM, IDX_DIM, N_IDX_HEADS)

LANES = 128
SUBLANES = 8
MXU_DTYPE = jnp.bfloat16
VMEM_LIMIT = 56 * 1024 * 1024
NEG_BIG = -1e30
LOWEST = -3.0e38
FLT_MIN = 1.1754943508222875e-38
LOG2E = 1.4426950408889634

TM = 512
TQ_MLA = 512
TK_MLA = 512
TQ_DSA = 256
TK_DSA = 512
MLA_HEADS_PER_STEP = 4

_C_QLAT = 0
_C_KVLAT = _C_QLAT + Q_LORA
_C_QB = _C_KVLAT + KV_LORA
_C_IQ = _C_QB + N_HEADS_B * LANES
_C_KB = _C_IQ + N_IDX_HEADS * LANES
_C_VB = _C_KB + LANES
_C_IK = _C_VB + LANES
_C_KR = _C_IK + LANES
_C_IW = _C_KR + LANES
_C_END = _C_IW + LANES


def _cparams(n_axes):
    return pltpu.CompilerParams(dimension_semantics=("arbitrary",) * n_axes, vmem_limit_bytes=VMEM_LIMIT)


def _const_spec(shape):
    return pl.BlockSpec(shape, lambda *_: (0,) * len(shape), pipeline_mode=pl.Buffered(1))


def _rms(x, g):
    return (x * lax.rsqrt(jnp.mean(x * x, axis=-1, keepdims=True) + EPS)) * g


def _rope_tile(x, cos, sin, half, first_half):
    partner = jnp.where(first_half, pltpu.roll(x, LANES - half, 1), pltpu.roll(x, half, 1))
    return x * cos + partner * sin


def _proj_kernel(x_ref, cq_ref, sq_ref, c64_ref, s64_ref, gat_ref, win_ref, gq_ref, wqb_ref, gkv_ref,
                 ckv_ref, kr_ref, kb_ref, vb_ref, ik_ref, iw_ref, qm_ref, qb_ref, iq_ref,
                 kb16_ref, vb16_ref, ik16_ref):
    x = x_ref[...]
    hb = _rms(x, gat_ref[...]).astype(MXU_DTYPE)
    y = jnp.dot(hb, win_ref[...], preferred_element_type=jnp.float32)

    lane = lax.broadcasted_iota(jnp.int32, (1, LANES), 1)
    first_q = lane < NOPE_DIM + ROPE_DIM // 2
    first_64 = (lane & (HEAD_DIM_B - 1)) < ROT_DIM_B // 2
    cq, sq, c64, s64 = cq_ref[...], sq_ref[...], c64_ref[...], s64_ref[...]

    qn = _rms(y[:, _C_QLAT:_C_QLAT + Q_LORA], gq_ref[...]).astype(MXU_DTYPE)
    qa = jnp.dot(qn, wqb_ref[...], preferred_element_type=jnp.float32)
    scale_a = (NOPE_DIM + ROPE_DIM) ** -0.5 * LOG2E
    for h in range(N_HEADS_A):
        t = _rope_tile(qa[:, h * LANES:(h + 1) * LANES], cq, sq, ROPE_DIM // 2, first_q)
        qm_ref[:, h * LANES:(h + 1) * LANES] = (t * scale_a).astype(qm_ref.dtype)

    ckv_ref[...] = _rms(y[:, _C_KVLAT:_C_KVLAT + KV_LORA], gkv_ref[...])
    kr_ref[...] = _rope_tile(y[:, _C_KR:_C_KR + LANES], cq, sq, ROPE_DIM // 2, first_q)

    scale_b = HEAD_DIM_B ** -0.5 * LOG2E
    for h in range(N_HEADS_B):
        t = _rope_tile(y[:, _C_QB + h * LANES:_C_QB + (h + 1) * LANES], c64, s64, ROT_DIM_B // 2, first_64)
        qb_ref[:, h * LANES:(h + 1) * LANES] = (t * scale_b).astype(qb_ref.dtype)
    for h in range(N_IDX_HEADS):
        t = _rope_tile(y[:, _C_IQ + h * LANES:_C_IQ + (h + 1) * LANES], c64, s64, ROT_DIM_IDX // 2, first_64)
        iq_ref[:, h * LANES:(h + 1) * LANES] = t.astype(iq_ref.dtype)

    kb = _rope_tile(y[:, _C_KB:_C_KB + LANES], c64, s64, ROT_DIM_B // 2, first_64)
    kb_ref[...] = kb
    kb16_ref[...] = kb.astype(kb16_ref.dtype)
    vb = y[:, _C_VB:_C_VB + LANES]
    vb_ref[...] = vb
    vb16_ref[...] = vb.astype(vb16_ref.dtype)
    ik = _rope_tile(y[:, _C_IK:_C_IK + LANES], c64, s64, ROT_DIM_IDX // 2, first_64)
    ik_ref[...] = ik
    ik16_ref[...] = ik.astype(ik16_ref.dtype)
    iw_ref[...] = y[:, _C_IW:_C_IW + LANES] * ((N_IDX_HEADS * IDX_DIM) ** -0.5)


def _proj_call(x2d, tabs, n_tab_tiles, tm, g_attn, w_in_p, g_q, w_qb_p, g_kv):
    t_tok = x2d.shape[0]
    grid = (t_tok // tm,)
    row = lambda w: pl.BlockSpec((tm, w), lambda i: (i, 0))
    tab = pl.BlockSpec((tm, LANES), lambda i: (i % n_tab_tiles, 0))
    f32, b16 = jnp.float32, MXU_DTYPE
    out_shapes = [
        jax.ShapeDtypeStruct((t_tok, KV_LORA), f32),
        jax.ShapeDtypeStruct((t_tok, LANES), f32),
        jax.ShapeDtypeStruct((t_tok, LANES), f32),
        jax.ShapeDtypeStruct((t_tok, LANES), f32),
        jax.ShapeDtypeStruct((t_tok, LANES), f32),
        jax.ShapeDtypeStruct((t_tok, LANES), f32),
        jax.ShapeDtypeStruct((t_tok, N_HEADS_A * LANES), b16),
        jax.ShapeDtypeStruct((t_tok, N_HEADS_B * LANES), b16),
        jax.ShapeDtypeStruct((t_tok, N_IDX_HEADS * LANES), b16),
        jax.ShapeDtypeStruct((t_tok, LANES), b16),
        jax.ShapeDtypeStruct((t_tok, LANES), b16),
        jax.ShapeDtypeStruct((t_tok, LANES), b16),
    ]
    out_specs = [row(s.shape[1]) for s in out_shapes]
    in_specs = [row(D_MODEL), tab, tab, tab, tab, _const_spec((1, D_MODEL)), _const_spec(w_in_p.shape),
                _const_spec((1, Q_LORA)), _const_spec(w_qb_p.shape), _const_spec((1, KV_LORA))]
    return pl.pallas_call(
        _proj_kernel, grid=grid, in_specs=in_specs, out_specs=out_specs, out_shape=out_shapes,
        compiler_params=_cparams(1), name="proj",
    )(x2d, *tabs, g_attn, w_in_p, g_q, w_qb_p, g_kv)


_NT = (((1,), (1,)), ((), ()))


def _kvup_kernel(c_ref, kr_ref, wk_ref, wvt_ref, k_ref, vt_ref):
    cb = c_ref[...].astype(MXU_DTYPE)
    kn = jnp.dot(cb, wk_ref[...], preferred_element_type=jnp.float32)
    kr = kr_ref[...]
    for h in range(N_HEADS_A):
        k_ref[:, h * LANES:(h + 1) * LANES] = (kn[:, h * LANES:(h + 1) * LANES] + kr).astype(k_ref.dtype)
    vt_ref[...] = lax.dot_general(wvt_ref[...], cb, _NT, preferred_element_type=jnp.float32).astype(vt_ref.dtype)


def _kvup_call(c3d, kr3d, wk, wvt, tk):
    b, s_pad, _ = c3d.shape
    row = lambda w: pl.BlockSpec((None, tk, w), lambda bi, i: (bi, i, 0))
    vdim = N_HEADS_A * V_DIM_A
    return pl.pallas_call(
        _kvup_kernel, grid=(b, s_pad // tk),
        in_specs=[row(KV_LORA), row(LANES), _const_spec(wk.shape), _const_spec(wvt.shape)],
        out_specs=[row(N_HEADS_A * LANES), pl.BlockSpec((None, None, vdim, tk), lambda bi, i: (bi, i, 0, 0))],
        out_shape=[jax.ShapeDtypeStruct((b, s_pad, N_HEADS_A * LANES), MXU_DTYPE),
                   jax.ShapeDtypeStruct((b, s_pad // tk, vdim, tk), MXU_DTYPE)],
        compiler_params=_cparams(2), name="kvup",
    )(c3d, kr3d, wk, wvt)


_CHUNK_SHIFT = CHUNK.bit_length() - 1


def _admissible_t(p0, kb, tq, tk, s_len):
    kpos = kb * tk + lax.broadcasted_iota(jnp.int32, (tk, 1), 0)
    qpos = p0 + lax.broadcasted_iota(jnp.int32, (1, tq), 1)
    return ((qpos >> _CHUNK_SHIFT) >= (kpos >> _CHUNK_SHIFT)) & (kpos < s_len)


def _check_tiles(tq, tk, q_off, lq):
    assert tk & (tk - 1) == 0 and tk % tq == 0 and tk % CHUNK == 0, (tq, tk)
    assert q_off % tq == 0 or (lq == tq and q_off % tk + tq <= tk), (tq, tk, q_off, lq)


def _last_block(p0, tq, tk, s_len):
    return jnp.minimum(p0 + tq - 1, s_len - 1) >> (tk.bit_length() - 1)


def _fori_grouped(n, body, carry, group):
    shift = group.bit_length() - 1

    def several(j, c):
        for i in range(group):
            c = body(group * j + i, c)
        return c

    carry = lax.fori_loop(0, n >> shift, several, carry)
    return lax.fori_loop((n >> shift) << shift, n, body, carry)


def _softmax_init(n_heads, tq, acc_ref):
    acc_ref[...] = jnp.zeros(acc_ref.shape, jnp.float32)
    return jnp.full((n_heads, tq), NEG_BIG, jnp.float32), jnp.zeros((n_heads, tq), jnp.float32)


def _attn_step(state, values, next_scores, bias, s_ref, acc_ref):
    (m_prev, l_prev), maxima = state
    n_heads = s_ref.shape[0]
    sums, next_max = [], []
    if values is not None:
        m_new = jnp.maximum(m_prev, maxima)
        alpha = jnp.exp2(m_prev - m_new)
    for h in range(n_heads):
        if values is not None:
            p = jnp.exp2(s_ref[h] - m_new[h:h + 1, :])
            sums.append(jnp.sum(p, axis=0, keepdims=True))
            acc_ref[h] = alpha[h:h + 1, :] * acc_ref[h] + jnp.dot(values[h], p.astype(values[h].dtype),
                                                                   preferred_element_type=jnp.float32)
        if next_scores is not None:
            s = next_scores(h)
            if bias is not None:
                s = s + bias
            s_ref[h] = s
            next_max.append(jnp.max(s, axis=0, keepdims=True))
    carry = (m_new, alpha * l_prev + jnp.concatenate(sums, axis=0)) if values is not None else (m_prev, l_prev)
    return carry, (jnp.concatenate(next_max, axis=0) if next_scores is not None else maxima)


def _softmax_finish(l, acc_ref, o_ref):
    inv = 1.0 / l
    o_t = jnp.concatenate([acc_ref[h] * inv[h:h + 1, :] for h in range(l.shape[0])], axis=0)
    o_ref[...] = o_t.T.astype(o_ref.dtype)


def _mla_kernel(q_ref, k_ref, vt_ref, o_ref, s_ref, acc_ref, *, tq, tk, q_off, s_len):
    p0 = q_off + pl.program_id(2) * tq
    last = _last_block(p0, tq, tk, s_len)
    heads = range(MLA_HEADS_PER_STEP)

    def qk(kb):
        start = pl.multiple_of(kb * tk, tk)
        return lambda h: lax.dot_general(k_ref[pl.ds(start, tk), h * LANES:(h + 1) * LANES],
                                         q_ref[:, h * LANES:(h + 1) * LANES], _NT,
                                         preferred_element_type=jnp.float32)

    def mask_bias(kb):
        return jnp.where(_admissible_t(p0, kb, tq, tk, s_len), 0.0, NEG_BIG)

    def values(kb):
        vt_all = vt_ref[kb]
        return [vt_all[h * V_DIM_A:(h + 1) * V_DIM_A, :] for h in heads]

    def body(kb, state):
        return _attn_step(state, values(kb), qk(kb + 1), None, s_ref, acc_ref)

    def tail(state):
        return _attn_step(state, values(last - 1), qk(last), mask_bias(last), s_ref, acc_ref)

    state = _attn_step((_softmax_init(MLA_HEADS_PER_STEP, tq, acc_ref), None), None, qk(0), mask_bias(0), s_ref,
                       acc_ref)
    state = lax.fori_loop(0, jnp.maximum(last - 1, 0), body, state)
    state = lax.cond(last > 0, tail, lambda st: st, state)
    (_, l), _ = _attn_step(state, values(last), None, None, s_ref, acc_ref)
    _softmax_finish(l, acc_ref, o_ref)


def _mla_call(q, k, vt, *, tq, tk, q_off, s_len):
    b, lq, _ = q.shape
    _check_tiles(tq, tk, q_off, lq)
    s_pad = k.shape[1]
    hg = MLA_HEADS_PER_STEP
    kern = functools.partial(_mla_kernel, tq=tq, tk=tk, q_off=q_off, s_len=s_len)
    once = dict(pipeline_mode=pl.Buffered(1))
    return pl.pallas_call(
        kern, grid=(b, N_HEADS_A // hg, lq // tq),
        in_specs=[pl.BlockSpec((None, tq, hg * LANES), lambda bi, g, qi: (bi, qi, g)),
                  pl.BlockSpec((None, s_pad, hg * LANES), lambda bi, g, qi: (bi, 0, g), **once),
                  pl.BlockSpec((None, s_pad // tk, hg * V_DIM_A, tk), lambda bi, g, qi: (bi, 0, g, 0), **once)],
        out_specs=pl.BlockSpec((None, tq, hg * V_DIM_A), lambda bi, g, qi: (bi, qi, g)),
        out_shape=jax.ShapeDtypeStruct((b, lq, N_HEADS_A * V_DIM_A), MXU_DTYPE),
        scratch_shapes=[pltpu.VMEM((hg, tk, tq), jnp.float32), pltpu.VMEM((hg, V_DIM_A, tq), jnp.float32)],
        compiler_params=_cparams(3), name="mla",
    )(q, k, vt)


def _f2key(x):
    b = lax.bitcast_convert_type(x, jnp.int32)
    return jnp.where(b < 0, b ^ jnp.int32(0x7FFFFFFF), b)


def _key2f(k):
    return lax.bitcast_convert_type(jnp.where(k < 0, k ^ jnp.int32(0x7FFFFFFF), k), jnp.float32)


_COUNT_SLAB = 4 * SUBLANES
_SCORE_GROUP = 4
_COUNT_GROUP = 4
_SEARCH_BISECT_EVERY = 8
_SEARCH_MAX_PROBES = 40 * _SEARCH_BISECT_EVERY


def _dsa_kernel(iq_ref, qb_ref, iw_ref, ik_ref, kb_ref, vbt_ref, tri_ref, o_ref,
                sc_ref, wt_ref, s_ref, acc_ref, *, tq, tk, q_off, s_len, topk):
    p0 = q_off + pl.program_id(1) * tq
    last = _last_block(p0, tq, tk, s_len)
    n_blk = last + 1
    row = lambda v: jnp.full((1, tq), v, jnp.float32)
    colsum = lambda x: jnp.sum(x, axis=0, keepdims=True)

    wt_ref[...] = iw_ref[...].T[:SUBLANES, :]

    def score_block(kb):
        start = pl.multiple_of(kb * tk, tk)
        ikb = ik_ref[pl.ds(start, tk), :]
        acc = jnp.zeros((tk, tq), jnp.float32)
        for h in range(N_IDX_HEADS):
            x = lax.dot_general(ikb, iq_ref[:, h * LANES:(h + 1) * LANES], _NT, preferred_element_type=jnp.float32)
            acc = acc + wt_ref[h:h + 1, :] * jnp.maximum(x, 0.0)
        return acc

    def stats(carry, hi_fill, lo_fill):
        mx, mn, n_pos, n_nonneg = carry
        return (jnp.maximum(mx, jnp.max(hi_fill, axis=0, keepdims=True)),
                jnp.minimum(mn, jnp.min(lo_fill, axis=0, keepdims=True)),
                n_pos + colsum(jnp.where(hi_fill >= FLT_MIN, 1.0, 0.0)),
                n_nonneg + colsum(jnp.where(hi_fill >= 0.0, 1.0, 0.0)))

    def p1_body(kb, carry):
        acc = score_block(kb)
        sc_ref[kb] = acc
        return stats(carry, acc, acc)

    carry = _fori_grouped(last, p1_body, (row(-jnp.inf), row(jnp.inf), row(0.0), row(0.0)), _SCORE_GROUP)
    acc = score_block(last)
    adm = _admissible_t(p0, last, tq, tk, s_len)
    masked = jnp.where(adm, acc, -jnp.inf)
    sc_ref[last] = masked
    mx, mn, n_pos, n_nonneg = stats(carry, masked, jnp.where(adm, acc, jnp.inf))

    qpos = p0 + lax.broadcasted_iota(jnp.int32, (1, tq), 1)
    n_adm = jnp.minimum(((qpos >> _CHUNK_SHIFT) + 1) * CHUNK, s_len).astype(jnp.float32)
    k_f = jnp.float32(topk)
    log_k = jnp.log(k_f)
    phi = lambda c: jnp.log(jnp.maximum(c, 0.25)) - log_k

    few = n_adm <= k_f
    pos = n_pos >= k_f
    at_zero = jnp.logical_and(~pos, n_nonneg >= k_f)
    above_max = _key2f(_f2key(jnp.maximum(mx, LOWEST)) + 1)
    lo = jnp.where(pos, FLT_MIN, LOWEST)
    c_lo = jnp.where(pos, n_pos, n_adm)
    hi = jnp.where(pos | at_zero, above_max, 0.0)
    c_hi = jnp.where(pos | at_zero, 0.0, n_nonneg)
    lo_val = jnp.where(pos, 0.0, jnp.maximum(mn, LOWEST))
    tau = jnp.where(few, LOWEST, jnp.where(at_zero, 0.0, lo))
    r = jnp.where(at_zero & ~few, k_f - n_pos, 0.0)
    done = jnp.where(few | at_zero, 1.0, 0.0)

    def count_ge(g):
        def body(kb, cnt):
            hit = jnp.where(sc_ref[kb] >= g, 1.0, 0.0)
            return cnt + jnp.sum(hit.reshape(tk // _COUNT_SLAB, _COUNT_SLAB, tq), axis=0)

        return colsum(_fori_grouped(n_blk, body, jnp.zeros((_COUNT_SLAB, tq), jnp.float32), _COUNT_GROUP))

    def search_cond(st):
        return jnp.logical_and(st[1] > 0, st[0] < _SEARCH_MAX_PROBES)

    def next_probe(it, lo, hi, f_lo, f_hi, lo_val):
        lok, hik = _f2key(lo), _f2key(hi)
        midk = (lok >> 1) + (hik >> 1) + (lok & hik & 1)
        frac = f_lo / jnp.maximum(f_lo - f_hi, 1e-9)
        g_int = lo_val + (hi - lo_val) * frac
        bisect = (it % _SEARCH_BISECT_EVERY) == (_SEARCH_BISECT_EVERY - 1)
        g = jnp.where(bisect, _key2f(midk), g_int)
        gk = jnp.minimum(jnp.maximum(_f2key(g), lok + 1), jnp.maximum(hik - 1, lok + 1))
        return _key2f(gk)

    def search_body(st):
        it, _, g, lo, hi, c_lo, c_hi, f_lo, f_hi, lo_val, side, tau, r, done = st
        c = count_ge(g)
        f_g = phi(c)
        live = done == 0.0
        up = live & (c >= k_f)
        dn = live & (c < k_f)
        f_hi = jnp.where(up & (side > 0.0), f_hi * 0.5, f_hi)
        f_lo = jnp.where(dn & (side < 0.0), f_lo * 0.5, f_lo)
        lo, c_lo, f_lo, lo_val = (jnp.where(up, g, lo), jnp.where(up, c, c_lo), jnp.where(up, f_g, f_lo),
                                  jnp.where(up, g, lo_val))
        hi, c_hi, f_hi = jnp.where(dn, g, hi), jnp.where(dn, c, c_hi), jnp.where(dn, f_g, f_hi)
        side = jnp.where(up, 1.0, jnp.where(dn, -1.0, side))
        adjacent = (_f2key(lo) + 1) >= _f2key(hi)
        fin = live & ((c_lo == k_f) | adjacent)
        tau = jnp.where(fin, lo, tau)
        r = jnp.where(fin & (c_lo > k_f), k_f - c_hi, r)
        done = jnp.where(fin, 1.0, done)
        pending = jnp.sum(1.0 - done).astype(jnp.int32)
        g = next_probe(it + 1, lo, hi, f_lo, f_hi, lo_val)
        return (it + 1, pending, g, lo, hi, c_lo, c_hi, f_lo, f_hi, lo_val, side, tau, r, done)

    pending0 = jnp.sum(1.0 - done).astype(jnp.int32)
    f_lo0, f_hi0 = phi(c_lo), phi(c_hi)
    st = lax.while_loop(search_cond, search_body,
                        (jnp.int32(0), pending0, next_probe(0, lo, hi, f_lo0, f_hi0, lo_val), lo, hi, c_lo, c_hi,
                         f_lo0, f_hi0, lo_val, row(0.0), tau, r, done))
    tau, r = st[11], st[12]

    @pl.when(jnp.sum(r) > 0.0)
    def _():
        tied = r > 0.0
        budget = jnp.where(tied, r, jnp.inf)

        def body(kb, seen):
            blk = sc_ref[kb]
            eq = jnp.logical_and(blk == tau, tied)
            eq_f = jnp.where(eq, 1.0, 0.0)
            rank = jnp.dot(tri_ref[...], eq_f.astype(MXU_DTYPE), preferred_element_type=jnp.float32) + seen
            sc_ref[kb] = jnp.where(jnp.logical_and(eq, rank > budget), -jnp.inf, blk)
            return seen + colsum(eq_f)

        _fori_grouped(n_blk, body, row(0.0), _COUNT_GROUP)

    heads_per_kv = N_HEADS_B // N_KV_B

    def qk(kb):
        start = pl.multiple_of(kb * tk, tk)
        kblk = kb_ref[pl.ds(start, tk), :]
        return lambda h: lax.dot_general(kblk, qb_ref[:, h * LANES:(h + 1) * LANES], _NT,
                                         preferred_element_type=jnp.float32)

    def sel_bias(kb):
        return jnp.where(sc_ref[kb] >= tau, 0.0, NEG_BIG)

    def values(kb):
        vt_all = vbt_ref[kb]
        return [vt_all[(h // heads_per_kv) * HEAD_DIM_B:(h // heads_per_kv + 1) * HEAD_DIM_B, :]
                for h in range(N_HEADS_B)]

    def p3_body(kb, state):
        return _attn_step(state, values(kb), qk(kb + 1), sel_bias(kb + 1), s_ref, acc_ref)

    state = _attn_step((_softmax_init(N_HEADS_B, tq, acc_ref), None), None, qk(0), sel_bias(0), s_ref, acc_ref)
    state = lax.fori_loop(0, last, p3_body, state)
    (_, l), _ = _attn_step(state, values(last), None, None, s_ref, acc_ref)
    _softmax_finish(l, acc_ref, o_ref)


def _dsa_call(iq, qb, iw, ik, kb, vbt, tri, *, tq, tk, q_off, s_len):
    b, lq, _ = iq.shape
    _check_tiles(tq, tk, q_off, lq)
    s_pad = ik.shape[1]
    n_kb = s_pad // tk
    topk = min(TOPK_MAX, s_len // 4)
    kern = functools.partial(_dsa_kernel, tq=tq, tk=tk, q_off=q_off, s_len=s_len, topk=topk)
    qspec = lambda w: pl.BlockSpec((None, tq, w), lambda bi, qi: (bi, qi, 0))
    once = dict(pipeline_mode=pl.Buffered(1))
    kspec = pl.BlockSpec((None, s_pad, LANES), lambda bi, qi: (bi, 0, 0), **once)
    return pl.pallas_call(
        kern, grid=(b, lq // tq),
        in_specs=[qspec(N_IDX_HEADS * LANES), qspec(N_HEADS_B * LANES), qspec(LANES), kspec, kspec,
                  pl.BlockSpec((None, n_kb, LANES, tk), lambda bi, qi: (bi, 0, 0, 0), **once),
                  _const_spec(tri.shape)],
        out_specs=qspec(N_HEADS_B * HEAD_DIM_B),
        out_shape=jax.ShapeDtypeStruct((b, lq, N_HEADS_B * HEAD_DIM_B), MXU_DTYPE),
        scratch_shapes=[pltpu.VMEM((n_kb, tk, tq), jnp.float32), pltpu.VMEM((SUBLANES, tq), jnp.float32),
                        pltpu.VMEM((N_HEADS_B, tk, tq), jnp.float32),
                        pltpu.VMEM((N_HEADS_B, HEAD_DIM_B, tq), jnp.float32)],
        compiler_params=_cparams(2), name="dsa",
    )(iq, qb, iw, ik, kb, vbt, tri)


_VISITS_PER_STEP = 4
VMEM_LIMIT_DSA = 60 * 1024 * 1024


def _dsa_pipe_kernel(iq_ref, iw_ref, qb_ref, ik_ref, kb_ref, vbt_ref, tri_ref, o_ref,
                     sc_ref, wt_ref, s_ref, acc_ref, tau_ref, *, tq, tk, q_off, s_len, topk, n_tiles):
    t = pl.program_id(1)
    has_cur, has_prev = t < n_tiles, t > 0
    cur, prev = jnp.minimum(t, n_tiles - 1), jnp.maximum(t - 1, 0)
    slot_c, slot_p = cur & 1, prev & 1
    p0 = q_off + cur * tq
    last = _last_block(p0, tq, tk, s_len)
    n_blk = last + 1
    p0_p = q_off + prev * tq
    last_p = _last_block(p0_p, tq, tk, s_len)
    row = lambda v: jnp.full((1, tq), v, jnp.float32)
    colsum = lambda x: jnp.sum(x, axis=0, keepdims=True)
    k_f = jnp.float32(topk)
    log_k = jnp.log(k_f)
    phi = lambda c: jnp.log(jnp.maximum(c, 0.25)) - log_k

    def score_phase():
        wt_ref[...] = iw_ref[...].T[:SUBLANES, :]

        def score_block(kb):
            start = pl.multiple_of(kb * tk, tk)
            ikb = ik_ref[pl.ds(start, tk), :]
            acc = jnp.zeros((tk, tq), jnp.float32)
            for h in range(N_IDX_HEADS):
                x = lax.dot_general(ikb, iq_ref[:, h * LANES:(h + 1) * LANES], _NT,
                                    preferred_element_type=jnp.float32)
                acc = acc + wt_ref[h:h + 1, :] * jnp.maximum(x, 0.0)
            return acc

        def stats(carry, hi_fill, lo_fill):
            mx, mn, n_pos, n_nonneg = carry
            return (jnp.maximum(mx, jnp.max(hi_fill, axis=0, keepdims=True)),
                    jnp.minimum(mn, jnp.min(lo_fill, axis=0, keepdims=True)),
                    n_pos + colsum(jnp.where(hi_fill >= FLT_MIN, 1.0, 0.0)),
                    n_nonneg + colsum(jnp.where(hi_fill >= 0.0, 1.0, 0.0)))

        def body(kb, carry):
            acc = score_block(kb)
            sc_ref[slot_c, kb] = acc
            return stats(carry, acc, acc)

        carry = _fori_grouped(last, body, (row(-jnp.inf), row(jnp.inf), row(0.0), row(0.0)), _SCORE_GROUP)
        acc = score_block(last)
        adm = _admissible_t(p0, last, tq, tk, s_len)
        masked = jnp.where(adm, acc, -jnp.inf)
        sc_ref[slot_c, last] = masked
        return stats(carry, masked, jnp.where(adm, acc, jnp.inf))

    mx, mn, n_pos, n_nonneg = lax.cond(has_cur, score_phase, lambda: (row(0.0), row(0.0), row(0.0), row(0.0)))

    qpos = p0 + lax.broadcasted_iota(jnp.int32, (1, tq), 1)
    n_adm = jnp.minimum(((qpos >> _CHUNK_SHIFT) + 1) * CHUNK, s_len).astype(jnp.float32)
    few = n_adm <= k_f
    pos = n_pos >= k_f
    at_zero = jnp.logical_and(~pos, n_nonneg >= k_f)
    above_max = _key2f(_f2key(jnp.maximum(mx, LOWEST)) + 1)
    lo = jnp.where(pos, FLT_MIN, LOWEST)
    c_lo = jnp.where(pos, n_pos, n_adm)
    hi = jnp.where(pos | at_zero, above_max, 0.0)
    c_hi = jnp.where(pos | at_zero, 0.0, n_nonneg)
    lo_val = jnp.where(pos, 0.0, jnp.maximum(mn, LOWEST))
    tau = jnp.where(few, LOWEST, jnp.where(at_zero, 0.0, lo))
    r = jnp.where(at_zero & ~few & has_cur, k_f - n_pos, 0.0)
    done = jnp.where(few | at_zero | jnp.logical_not(has_cur), 1.0, 0.0)

    def next_probe(it, lo, hi, f_lo, f_hi, lo_val):
        lok, hik = _f2key(lo), _f2key(hi)
        midk = (lok >> 1) + (hik >> 1) + (lok & hik & 1)
        frac = f_lo / jnp.maximum(f_lo - f_hi, 1e-9)
        g_int = lo_val + (hi - lo_val) * frac
        bisect = (it % _SEARCH_BISECT_EVERY) == (_SEARCH_BISECT_EVERY - 1)
        g = jnp.where(bisect, _key2f(midk), g_int)
        gk = jnp.minimum(jnp.maximum(_f2key(g), lok + 1), jnp.maximum(hik - 1, lok + 1))
        return _key2f(gk)

    def search_update(st, c, enable):
        it, g, lo, hi, c_lo, c_hi, f_lo, f_hi, lo_val, side, tau, r, done = st
        f_g = phi(c)
        live = jnp.logical_and(done == 0.0, enable)
        up = live & (c >= k_f)
        dn = live & (c < k_f)
        f_hi = jnp.where(up & (side > 0.0), f_hi * 0.5, f_hi)
        f_lo = jnp.where(dn & (side < 0.0), f_lo * 0.5, f_lo)
        lo, c_lo, f_lo, lo_val = (jnp.where(up, g, lo), jnp.where(up, c, c_lo), jnp.where(up, f_g, f_lo),
                                  jnp.where(up, g, lo_val))
        hi, c_hi, f_hi = jnp.where(dn, g, hi), jnp.where(dn, c, c_hi), jnp.where(dn, f_g, f_hi)
        side = jnp.where(up, 1.0, jnp.where(dn, -1.0, side))
        adjacent = (_f2key(lo) + 1) >= _f2key(hi)
        fin = live & ((c_lo == k_f) | adjacent)
        tau = jnp.where(fin, lo, tau)
        r = jnp.where(fin & (c_lo > k_f), k_f - c_hi, r)
        done = jnp.where(fin, 1.0, done)
        it = it + enable.astype(jnp.int32)
        g = jnp.where(enable, next_probe(it, lo, hi, f_lo, f_hi, lo_val), g)
        return (it, g, lo, hi, c_lo, c_hi, f_lo, f_hi, lo_val, side, tau, r, done)

    def visit(kb, g, cnt):
        hit = jnp.where(sc_ref[slot_c, kb] >= g, 1.0, 0.0)
        return cnt + jnp.sum(hit.reshape(tk // _COUNT_SLAB, _COUNT_SLAB, tq), axis=0)

    f_lo0, f_hi0 = phi(c_lo), phi(c_hi)
    sst = (jnp.int32(0), next_probe(0, lo, hi, f_lo0, f_hi0, lo_val), lo, hi, c_lo, c_hi, f_lo0, f_hi0, lo_val,
           row(0.0), tau, r, done)
    zero_cnt = jnp.zeros((_COUNT_SLAB, tq), jnp.float32)

    heads_per_kv = N_HEADS_B // N_KV_B

    def qk(kb):
        start = pl.multiple_of(kb * tk, tk)
        kblk = kb_ref[pl.ds(start, tk), :]
        return lambda h: lax.dot_general(kblk, qb_ref[:, h * LANES:(h + 1) * LANES], _NT,
                                         preferred_element_type=jnp.float32)

    def sel_bias(kb):
        return jnp.where(sc_ref[slot_p, kb] >= tau_ref[slot_p], 0.0, NEG_BIG)

    def values(kb):
        vt_all = vbt_ref[kb]
        return [vt_all[(h // heads_per_kv) * HEAD_DIM_B:(h // heads_per_kv + 1) * HEAD_DIM_B, :]
                for h in range(N_HEADS_B)]

    init = _softmax_init(N_HEADS_B, tq, acc_ref)
    attn = lax.cond(has_prev,
                    lambda: _attn_step((init, None), None, qk(0), sel_bias(0), s_ref, acc_ref),
                    lambda: (init, jnp.zeros((N_HEADS_B, tq), jnp.float32)))

    def fused_step(kb, carry):
        attn, sst, cursor, cnt = carry
        attn = _attn_step(attn, values(kb), qk(kb + 1), sel_bias(kb + 1), s_ref, acc_ref)
        g = sst[1]
        for i in range(_VISITS_PER_STEP):
            blk = cursor + i
            cnt = visit(jnp.minimum(blk, last), jnp.where(blk < n_blk, g, jnp.inf), cnt)
        complete = (cursor + _VISITS_PER_STEP) >= n_blk
        sst = search_update(sst, colsum(cnt), complete)
        cnt = jnp.where(complete, 0.0, cnt)
        cursor = jnp.where(complete, 0, cursor + _VISITS_PER_STEP)
        return attn, sst, cursor, cnt

    attn, sst, _, _ = lax.fori_loop(0, jnp.where(has_prev, last_p, 0), fused_step,
                                    (attn, sst, jnp.int32(0), zero_cnt))

    @pl.when(has_prev)
    def _():
        (_, l), _ = _attn_step(attn, values(last_p), None, None, s_ref, acc_ref)
        _softmax_finish(l, acc_ref, o_ref)

    def count_ge(g):
        return colsum(_fori_grouped(n_blk, lambda kb, cnt: visit(kb, g, cnt), zero_cnt, _COUNT_GROUP))

    def pending_of(done):
        return jnp.sum(1.0 - done).astype(jnp.int32)

    def search_cond(c):
        return jnp.logical_and(c[0] > 0, c[1][0] < _SEARCH_MAX_PROBES)

    def search_body(c):
        sst = search_update(c[1], count_ge(c[1][1]), jnp.bool_(True))
        return pending_of(sst[12]), sst

    _, sst = lax.while_loop(search_cond, search_body, (pending_of(sst[12]), sst))
    tau, r = sst[10], sst[11]

    @pl.when(jnp.sum(r) > 0.0)
    def _():
        tied = r > 0.0
        budget = jnp.where(tied, r, jnp.inf)

        def body(kb, seen):
            blk = sc_ref[slot_c, kb]
            eq = jnp.logical_and(blk == tau, tied)
            eq_f = jnp.where(eq, 1.0, 0.0)
            rank = jnp.dot(tri_ref[...], eq_f.astype(MXU_DTYPE), preferred_element_type=jnp.float32) + seen
            sc_ref[slot_c, kb] = jnp.where(jnp.logical_and(eq, rank > budget), -jnp.inf, blk)
            return seen + colsum(eq_f)

        _fori_grouped(n_blk, body, row(0.0), _COUNT_GROUP)

    @pl.when(has_cur)
    def _():
        tau_ref[slot_c] = tau


def _dsa_pipe_call(iq, qb, iw, ik, kb, vbt, tri, *, tq, tk, q_off, s_len):
    b, lq, _ = iq.shape
    _check_tiles(tq, tk, q_off, lq)
    s_pad = ik.shape[1]
    n_kb = s_pad // tk
    n_tiles = lq // tq
    topk = min(TOPK_MAX, s_len // 4)
    kern = functools.partial(_dsa_pipe_kernel, tq=tq, tk=tk, q_off=q_off, s_len=s_len, topk=topk, n_tiles=n_tiles)
    cur_spec = lambda w: pl.BlockSpec((None, tq, w), lambda bi, t: (bi, jnp.minimum(t, n_tiles - 1), 0))
    prev_spec = lambda w: pl.BlockSpec((None, tq, w), lambda bi, t: (bi, jnp.maximum(t - 1, 0), 0))
    once = dict(pipeline_mode=pl.Buffered(1))
    kspec = pl.BlockSpec((None, s_pad, LANES), lambda bi, t: (bi, 0, 0), **once)
    return pl.pallas_call(
        kern, grid=(b, n_tiles + 1),
        in_specs=[cur_spec(N_IDX_HEADS * LANES), cur_spec(LANES), prev_spec(N_HEADS_B * LANES), kspec, kspec,
                  pl.BlockSpec((None, n_kb, LANES, tk), lambda bi, t: (bi, 0, 0, 0), **once),
                  _const_spec(tri.shape)],
        out_specs=prev_spec(N_HEADS_B * HEAD_DIM_B),
        out_shape=jax.ShapeDtypeStruct((b, lq, N_HEADS_B * HEAD_DIM_B), MXU_DTYPE),
        scratch_shapes=[pltpu.VMEM((2, n_kb, tk, tq), jnp.float32), pltpu.VMEM((SUBLANES, tq), jnp.float32),
                        pltpu.VMEM((N_HEADS_B, tk, tq), jnp.float32),
                        pltpu.VMEM((N_HEADS_B, HEAD_DIM_B, tq), jnp.float32),
                        pltpu.VMEM((2, 1, tq), jnp.float32)],
        compiler_params=pltpu.CompilerParams(dimension_semantics=("arbitrary", "arbitrary"),
                                             vmem_limit_bytes=VMEM_LIMIT_DSA),
        name="dsa",
    )(iq, iw, qb, ik, kb, vbt, tri)


_FF_CHUNK = 256


def _out_kernel(x_ref, oa_ref, ob_ref, woa_ref, wob_ref, gf_ref, wg_ref, wu_ref, wd_ref, gfin_ref, y_ref):
    o = (jnp.dot(oa_ref[...], woa_ref[...], preferred_element_type=jnp.float32)
         + jnp.dot(ob_ref[...], wob_ref[...], preferred_element_type=jnp.float32))
    x1 = x_ref[...] + o
    h2 = _rms(x1, gf_ref[...]).astype(MXU_DTYPE)
    ffn = jnp.zeros_like(x1)
    for j in range(D_FF // _FF_CHUNK):
        sl = slice(j * _FF_CHUNK, (j + 1) * _FF_CHUNK)
        gate = jnp.dot(h2, wg_ref[:, sl], preferred_element_type=jnp.float32)
        up = jnp.dot(h2, wu_ref[:, sl], preferred_element_type=jnp.float32)
        act = (gate * jax.nn.sigmoid(gate)) * up
        ffn = ffn + jnp.dot(act.astype(MXU_DTYPE), wd_ref[sl, :], preferred_element_type=jnp.float32)
    y_ref[...] = _rms(x1 + ffn, gfin_ref[...])


def _out_call(x2d, oa, ob, woa, wob, g_ffn, wg, wu, wd, g_fin, tm):
    t_tok = x2d.shape[0]
    row = lambda w: pl.BlockSpec((tm, w), lambda i: (i, 0))
    half = N_HEADS_A * V_DIM_A
    return pl.pallas_call(
        _out_kernel, grid=(t_tok // tm,),
        in_specs=[row(D_MODEL), row(half), row(half), _const_spec(woa.shape), _const_spec(wob.shape),
                  _const_spec((1, D_MODEL)), _const_spec(wg.shape), _const_spec(wu.shape), _const_spec(wd.shape),
                  _const_spec((1, D_MODEL))],
        out_specs=row(D_MODEL), out_shape=jax.ShapeDtypeStruct((t_tok, D_MODEL), jnp.float32),
        compiler_params=_cparams(1), name="out",
    )(x2d, oa, ob, woa, wob, g_ffn, wg, wu, wd, g_fin)


def _prep_weights(w_in, w_q_b, w_kv_b, w_o, w_gate, w_up, w_down):
    off = np.concatenate([[0], np.cumsum(IN_SIZES)])
    o_qlat, o_kvlat, o_kr, o_qb, o_kb, o_vb, o_iq, o_ik, o_iw = off[:9]
    zeros = lambda n: jnp.zeros((D_MODEL, n), w_in.dtype)
    cols = [w_in[:, o_qlat:o_qlat + Q_LORA], w_in[:, o_kvlat:o_kvlat + KV_LORA]]
    for h in range(N_HEADS_B):
        head = w_in[:, o_qb + h * HEAD_DIM_B:o_qb + (h + 1) * HEAD_DIM_B]
        cols += [head, zeros(HEAD_DIM_B)] if h < N_HEADS_B // N_KV_B else [zeros(HEAD_DIM_B), head]
    for h in range(N_IDX_HEADS):
        cols += [w_in[:, o_iq + h * IDX_DIM:o_iq + (h + 1) * IDX_DIM], zeros(LANES - IDX_DIM)]
    cols += [w_in[:, o_kb:o_kb + LANES], w_in[:, o_vb:o_vb + LANES],
             w_in[:, o_ik:o_ik + IDX_DIM], zeros(LANES - IDX_DIM),
             zeros(NOPE_DIM), w_in[:, o_kr:o_kr + ROPE_DIM], zeros(LANES - NOPE_DIM - ROPE_DIM),
             w_in[:, o_iw:o_iw + N_IDX_HEADS], zeros(LANES - N_IDX_HEADS)]
    w_in_p = jnp.concatenate(cols, axis=1).astype(MXU_DTYPE)
    assert w_in_p.shape[1] == _C_END

    qd = NOPE_DIM + ROPE_DIM
    w_qb_p = jnp.pad(w_q_b.reshape(Q_LORA, N_HEADS_A, qd), ((0, 0), (0, 0), (0, LANES - qd)))
    w_qb_p = w_qb_p.reshape(Q_LORA, N_HEADS_A * LANES).astype(MXU_DTYPE)

    kv = w_kv_b.reshape(KV_LORA, N_HEADS_A, NOPE_DIM + V_DIM_A)
    wk = jnp.pad(kv[:, :, :NOPE_DIM], ((0, 0), (0, 0), (0, LANES - NOPE_DIM)))
    wk = wk.reshape(KV_LORA, N_HEADS_A * LANES).astype(MXU_DTYPE)
    wvt = kv[:, :, NOPE_DIM:].reshape(KV_LORA, N_HEADS_A * V_DIM_A).T.astype(MXU_DTYPE)

    half = N_HEADS_A * V_DIM_A
    return (w_in_p, w_qb_p, wk, wvt, w_o[:half].astype(MXU_DTYPE), w_o[half:].astype(MXU_DTYPE),
            w_gate.astype(MXU_DTYPE), w_up.astype(MXU_DTYPE), w_down.astype(MXU_DTYPE))


def _rope_tables(pos):
    posf = pos.astype(jnp.float32)[:, None]
    n = pos.shape[0]

    def cs(d):
        inv = jnp.power(ROPE_THETA, -jnp.arange(0, d, 2, dtype=jnp.float32) / d)
        ang = posf * inv[None, :]
        return jnp.cos(ang), jnp.sin(ang)

    one, zero = (lambda w: jnp.ones((n, w), jnp.float32)), (lambda w: jnp.zeros((n, w), jnp.float32))
    c, s = cs(ROPE_DIM)
    cq = jnp.concatenate([one(NOPE_DIM), c, c, one(LANES - NOPE_DIM - ROPE_DIM)], axis=1)
    sq = jnp.concatenate([zero(NOPE_DIM), -s, s, zero(LANES - NOPE_DIM - ROPE_DIM)], axis=1)
    c, s = cs(ROT_DIM_B)
    c64 = jnp.tile(jnp.concatenate([c, c, one(HEAD_DIM_B - ROT_DIM_B)], axis=1), (1, LANES // HEAD_DIM_B))
    s64 = jnp.tile(jnp.concatenate([-s, s, zero(HEAD_DIM_B - ROT_DIM_B)], axis=1), (1, LANES // HEAD_DIM_B))
    return cq, sq, c64, s64


def _pick_tile(n, pref):
    t = min(pref, n)
    while n % t:
        t //= 2
    return t


def _round_up(n, m):
    return -(-n // m) * m


def _layer(x, pos0, past, weights, norms, final_norm):
    b, l, _ = x.shape
    g_attn, g_q, g_kv, g_ffn = norms
    w_in_p, w_qb_p, wk, wvt, woa, wob, wg, wu, wd = weights
    t_tok = b * l
    x2d = x.reshape(t_tok, D_MODEL)

    tabs = _rope_tables(pos0 + jnp.arange(l))
    if l >= TM // 2:
        tm = _pick_tile(l, TM)
        n_tab_tiles = l // tm
    else:
        tm = _pick_tile(t_tok, TM)
        tabs = tuple(jnp.tile(t, (tm // l, 1)) for t in tabs)
        n_tab_tiles = 1
    (ckv, kr128, kb, vb, ik128, iw, q_mla, q_b, iq, kb16, vb16, ik16) = _proj_call(
        x2d, tabs, n_tab_tiles, tm, g_attn, w_in_p, g_q, w_qb_p, g_kv)

    new = (ckv.reshape(b, l, KV_LORA), kr128.reshape(b, l, LANES)[..., NOPE_DIM:NOPE_DIM + ROPE_DIM],
           kb.reshape(b, l, N_KV_B, HEAD_DIM_B), vb.reshape(b, l, N_KV_B, HEAD_DIM_B),
           ik128.reshape(b, l, LANES)[..., :IDX_DIM])

    per_row = lambda a: a.reshape(b, l, a.shape[-1])
    p_len = 0 if past is None else past[0].shape[1]
    s_len = p_len + l
    lq = _round_up(l, LANES)
    tq_mla, tq_dsa = _pick_tile(lq, TQ_MLA), _pick_tile(lq, TQ_DSA)
    tk_mla, tk_dsa = (2 * TK_MLA, 2 * TK_DSA) if lq < TQ_DSA else (TK_MLA, TK_DSA)
    s_pad = _round_up(s_len, max(tk_mla, tk_dsa))
    if past is None:
        old = [None] * 5
    else:
        p_lat, p_kr, p_k, p_v, p_ik = past
        old = [p_lat,
               jnp.pad(p_kr, ((0, 0), (0, 0), (NOPE_DIM, LANES - NOPE_DIM - ROPE_DIM))),
               p_k.reshape(b, p_len, LANES).astype(MXU_DTYPE), p_v.reshape(b, p_len, LANES).astype(MXU_DTYPE),
               jnp.pad(p_ik, ((0, 0), (0, 0), (0, LANES - IDX_DIM))).astype(MXU_DTYPE)]

    def full_seq(prev, cur):
        parts = ([] if prev is None else [prev]) + [per_row(cur)]
        if s_pad > s_len:
            parts.append(jnp.zeros((b, s_pad - s_len, cur.shape[-1]), cur.dtype))
        return parts[0] if len(parts) == 1 else jnp.concatenate(parts, axis=1)

    c_all, kr_all, kb_all, vb_all, ik_all = (full_seq(o, c) for o, c in zip(old, (ckv, kr128, kb16, vb16, ik16)))
    vbt_all = vb_all.reshape(b, s_pad // tk_dsa, tk_dsa, LANES).transpose(0, 1, 3, 2)

    k_mla, vt_mla = _kvup_call(c_all, kr_all, wk, wvt, tk_mla)

    padq = lambda a: per_row(a) if lq == l else jnp.pad(per_row(a), ((0, 0), (0, lq - l), (0, 0)))
    o_a = _mla_call(padq(q_mla), k_mla, vt_mla, tq=tq_mla, tk=tk_mla, q_off=pos0, s_len=s_len)
    tri = jnp.tril(jnp.ones((tk_dsa, tk_dsa), MXU_DTYPE))
    o_b = _dsa_pipe_call(padq(iq), padq(q_b), padq(iw), ik_all, kb_all, vbt_all, tri,
                    tq=tq_dsa, tk=tk_dsa, q_off=pos0, s_len=s_len)
    o_a, o_b = (o[:, :l].reshape(t_tok, -1) for o in (o_a, o_b))

    y = _out_call(x2d, o_a, o_b, woa, wob, g_ffn, wg, wu, wd, final_norm, tm)
    return y.reshape(b, l, D_MODEL), new


def kernel(x_prompt, x_sample, cache_mla_latent, cache_mla_krope, cache_dsa_k, cache_dsa_v, cache_idx_k, attn_norm, w_in, q_a_norm, w_q_b, kv_a_norm, w_kv_b, w_o, ffn_norm, w_gate, w_up, w_down, final_norm):
    depth = w_in.shape[0]
    assert depth == 1, "the fused final norm assumes a single layer"
    past_len = cache_mla_latent.shape[2]
    li = 0
    weights = _prep_weights(w_in[li], w_q_b[li], w_kv_b[li], w_o[li], w_gate[li], w_up[li], w_down[li])
    norms = (attn_norm[li][None, :], q_a_norm[li][None, :], kv_a_norm[li][None, :], ffn_norm[li][None, :])
    fin = final_norm[None, :]
    past = (cache_mla_latent[li], cache_mla_krope[li], cache_dsa_k[li], cache_dsa_v[li], cache_idx_k[li])
    y_p, new_p = _layer(x_prompt, 0, None, weights, norms, fin)
    y_s, new_s = _layer(x_sample, past_len, past, weights, norms, fin)
    return (y_p, y_s) + tuple(a[None] for a in new_p) + tuple(a[None] for a in new_s)
```

```python
import functools

import numpy as np
import jax
import jax.numpy as jnp
from jax import lax
from jax.experimental import pallas as pl
from jax.experimental.pallas import tpu as pltpu

D_MODEL = 1024
CHUNK = 64
ROPE_THETA = 500000.0
EPS = 1e-6
N_HEADS_A = 8
NOPE_DIM = 64
ROPE_DIM = 32
V_DIM_A = 64
Q_LORA = 384
KV_LORA = 256
N_HEADS_B = 8
N_KV_B = 2
HEAD_DIM_B = 64
ROT_DIM_B = 16
N_IDX_HEADS = 8
IDX_DIM = 64
ROT_DIM_IDX = 16
TOPK_MAX = 256
D_FF = 2816
IN_SIZES = (Q_LORA, KV_LORA, ROPE_DIM, N_HEADS_B * HEAD_DIM_B, N_KV_B * HEAD_DIM_B, N_KV_B * HEAD_DIM_B,
            N_IDX_HEADS * IDX_DIM, IDX_DIM, N_IDX_HEADS)

LANES = 128
SUBLANES = 8
MXU_DTYPE = jnp.bfloat16
VMEM_LIMIT = 56 * 1024 * 1024
NEG_BIG = -1e30
LOWEST = -3.0e38
FLT_MIN = 1.1754943508222875e-38
LOG2E = 1.4426950408889634

TM = 512
TQ_MLA = 512
TK_MLA = 512
TQ_DSA = 256
TK_DSA = 512
MLA_HEADS_PER_STEP = 4

_C_QLAT = 0
_C_KVLAT = _C_QLAT + Q_LORA
_C_QB = _C_KVLAT + KV_LORA
_C_IQ = _C_QB + N_HEADS_B * HEAD_DIM_B
_C_KB = _C_IQ + N_IDX_HEADS * IDX_DIM
_C_VB = _C_KB + LANES
_C_IK = _C_VB + LANES
_C_KR = _C_IK + LANES
_C_IW = _C_KR + LANES
_C_END = _C_IW + LANES


def _cparams(n_axes):
    return pltpu.CompilerParams(dimension_semantics=("arbitrary",) * n_axes, vmem_limit_bytes=VMEM_LIMIT)


def _const_spec(shape):
    return pl.BlockSpec(shape, lambda *_: (0,) * len(shape), pipeline_mode=pl.Buffered(1))


def _rms(x, g):
    return (x * lax.rsqrt(jnp.mean(x * x, axis=-1, keepdims=True) + EPS)) * g


def _rope_tile(x, cos, sin, half, first_half):
    partner = jnp.where(first_half, pltpu.roll(x, LANES - half, 1), pltpu.roll(x, half, 1))
    return x * cos + partner * sin


def _proj_kernel(x_ref, cq_ref, sq_ref, c64_ref, s64_ref, gat_ref, win_ref, gq_ref, wqb_ref, gkv_ref,
                 ckv_ref, kr_ref, kb_ref, vb_ref, ik_ref, iw_ref, qm_ref, qb_ref, iq_ref,
                 kb16_ref, vb16_ref, ik16_ref):
    x = x_ref[...]
    hb = _rms(x, gat_ref[...]).astype(MXU_DTYPE)
    y = jnp.dot(hb, win_ref[...], preferred_element_type=jnp.float32)

    lane = lax.broadcasted_iota(jnp.int32, (1, LANES), 1)
    first_q = lane < NOPE_DIM + ROPE_DIM // 2
    first_64 = (lane & (HEAD_DIM_B - 1)) < ROT_DIM_B // 2
    cq, sq, c64, s64 = cq_ref[...], sq_ref[...], c64_ref[...], s64_ref[...]

    qn = _rms(y[:, _C_QLAT:_C_QLAT + Q_LORA], gq_ref[...]).astype(MXU_DTYPE)
    qa = jnp.dot(qn, wqb_ref[...], preferred_element_type=jnp.float32)
    scale_a = (NOPE_DIM + ROPE_DIM) ** -0.5 * LOG2E
    for h in range(N_HEADS_A):
        t = _rope_tile(qa[:, h * LANES:(h + 1) * LANES], cq, sq, ROPE_DIM // 2, first_q)
        qm_ref[:, h * LANES:(h + 1) * LANES] = (t * scale_a).astype(qm_ref.dtype)

    ckv_ref[...] = _rms(y[:, _C_KVLAT:_C_KVLAT + KV_LORA], gkv_ref[...])
    kr_ref[...] = _rope_tile(y[:, _C_KR:_C_KR + LANES], cq, sq, ROPE_DIM // 2, first_q)

    scale_b = HEAD_DIM_B ** -0.5 * LOG2E
    low = lane < HEAD_DIM_B
    heads_per_kv = N_HEADS_B // N_KV_B
    for t in range(N_HEADS_B // 2):
        pair = _rope_tile(y[:, _C_QB + t * LANES:_C_QB + (t + 1) * LANES], c64, s64, ROT_DIM_B // 2, first_64)
        pair = pair * scale_b
        swapped = pltpu.roll(pair, HEAD_DIM_B, 1)
        for u in range(2):
            h = 2 * t + u
            half = h // heads_per_kv
            src = pair if u == half else swapped
            tile = jnp.where(low if half == 0 else jnp.logical_not(low), src, 0.0)
            qb_ref[:, h * LANES:(h + 1) * LANES] = tile.astype(qb_ref.dtype)
    for t in range(N_IDX_HEADS // 2):
        pair = _rope_tile(y[:, _C_IQ + t * LANES:_C_IQ + (t + 1) * LANES], c64, s64, ROT_DIM_IDX // 2, first_64)
        swapped = pltpu.roll(pair, IDX_DIM, 1)
        for u in range(2):
            h = 2 * t + u
            iq_ref[:, h * LANES:(h + 1) * LANES] = jnp.where(low, pair if u == 0 else swapped,
                                                             0.0).astype(iq_ref.dtype)

    kb = _rope_tile(y[:, _C_KB:_C_KB + LANES], c64, s64, ROT_DIM_B // 2, first_64)
    kb_ref[...] = kb
    kb16_ref[...] = kb.astype(kb16_ref.dtype)
    vb = y[:, _C_VB:_C_VB + LANES]
    vb_ref[...] = vb
    vb16_ref[...] = vb.astype(vb16_ref.dtype)
    ik = _rope_tile(y[:, _C_IK:_C_IK + LANES], c64, s64, ROT_DIM_IDX // 2, first_64)
    ik_ref[...] = ik
    ik16_ref[...] = ik.astype(ik16_ref.dtype)
    iw_ref[...] = y[:, _C_IW:_C_IW + LANES] * ((N_IDX_HEADS * IDX_DIM) ** -0.5)


def _proj_call(x2d, tabs, n_tab_tiles, tm, g_attn, w_in_p, g_q, w_qb_p, g_kv):
    t_tok = x2d.shape[0]
    grid = (t_tok // tm,)
    row = lambda w: pl.BlockSpec((tm, w), lambda i: (i, 0))
    tab = pl.BlockSpec((tm, LANES), lambda i: (i % n_tab_tiles, 0))
    f32, b16 = jnp.float32, MXU_DTYPE
    out_shapes = [
        jax.ShapeDtypeStruct((t_tok, KV_LORA), f32),
        jax.ShapeDtypeStruct((t_tok, LANES), f32),
        jax.ShapeDtypeStruct((t_tok, LANES), f32),
        jax.ShapeDtypeStruct((t_tok, LANES), f32),
        jax.ShapeDtypeStruct((t_tok, LANES), f32),
        jax.ShapeDtypeStruct((t_tok, LANES), f32),
        jax.ShapeDtypeStruct((t_tok, N_HEADS_A * LANES), b16),
        jax.ShapeDtypeStruct((t_tok, N_HEADS_B * LANES), b16),
        jax.ShapeDtypeStruct((t_tok, N_IDX_HEADS * LANES), b16),
        jax.ShapeDtypeStruct((t_tok, LANES), b16),
        jax.ShapeDtypeStruct((t_tok, LANES), b16),
        jax.ShapeDtypeStruct((t_tok, LANES), b16),
    ]
    out_specs = [row(s.shape[1]) for s in out_shapes]
    in_specs = [row(D_MODEL), tab, tab, tab, tab, _const_spec((1, D_MODEL)), _const_spec(w_in_p.shape),
                _const_spec((1, Q_LORA)), _const_spec(w_qb_p.shape), _const_spec((1, KV_LORA))]
    return pl.pallas_call(
        _proj_kernel, grid=grid, in_specs=in_specs, out_specs=out_specs, out_shape=out_shapes,
        compiler_params=_cparams(1), name="proj",
    )(x2d, *tabs, g_attn, w_in_p, g_q, w_qb_p, g_kv)


_NT = (((1,), (1,)), ((), ()))


def _kvup_kernel(c_ref, kr_ref, wk_ref, wvt_ref, k_ref, vt_ref):
    cb = c_ref[...].astype(MXU_DTYPE)
    kn = jnp.dot(cb, wk_ref[...], preferred_element_type=jnp.float32)
    kr = kr_ref[...]
    for h in range(N_HEADS_A):
        k_ref[:, h * LANES:(h + 1) * LANES] = (kn[:, h * LANES:(h + 1) * LANES] + kr).astype(k_ref.dtype)
    vt_ref[...] = lax.dot_general(wvt_ref[...], cb, _NT, preferred_element_type=jnp.float32).astype(vt_ref.dtype)


def _kvup_call(c3d, kr3d, wk, wvt, tk):
    b, s_pad, _ = c3d.shape
    row = lambda w: pl.BlockSpec((None, tk, w), lambda bi, i: (bi, i, 0))
    vdim = N_HEADS_A * V_DIM_A
    return pl.pallas_call(
        _kvup_kernel, grid=(b, s_pad // tk),
        in_specs=[row(KV_LORA), row(LANES), _const_spec(wk.shape), _const_spec(wvt.shape)],
        out_specs=[row(N_HEADS_A * LANES), pl.BlockSpec((None, None, vdim, tk), lambda bi, i: (bi, i, 0, 0))],
        out_shape=[jax.ShapeDtypeStruct((b, s_pad, N_HEADS_A * LANES), MXU_DTYPE),
                   jax.ShapeDtypeStruct((b, s_pad // tk, vdim, tk), MXU_DTYPE)],
        compiler_params=_cparams(2), name="kvup",
    )(c3d, kr3d, wk, wvt)


_CHUNK_SHIFT = CHUNK.bit_length() - 1


def _admissible_t(p0, kb, tq, tk, s_len):
    kpos = kb * tk + lax.broadcasted_iota(jnp.int32, (tk, 1), 0)
    qpos = p0 + lax.broadcasted_iota(jnp.int32, (1, tq), 1)
    return ((qpos >> _CHUNK_SHIFT) >= (kpos >> _CHUNK_SHIFT)) & (kpos < s_len)


def _check_tiles(tq, tk, q_off, lq):
    assert tk & (tk - 1) == 0 and tk % tq == 0 and tk % CHUNK == 0, (tq, tk)
    assert q_off % tq == 0 or (lq == tq and q_off % tk + tq <= tk), (tq, tk, q_off, lq)


def _last_block(p0, tq, tk, s_len):
    return jnp.minimum(p0 + tq - 1, s_len - 1) >> (tk.bit_length() - 1)


def _fori_grouped(n, body, carry, group):
    shift = group.bit_length() - 1

    def several(j, c):
        for i in range(group):
            c = body(group * j + i, c)
        return c

    carry = lax.fori_loop(0, n >> shift, several, carry)
    return lax.fori_loop((n >> shift) << shift, n, body, carry)


def _softmax_init(n_heads, tq, acc_ref):
    acc_ref[...] = jnp.zeros(acc_ref.shape, jnp.float32)
    return jnp.full((n_heads, tq), NEG_BIG, jnp.float32), jnp.zeros((n_heads, tq), jnp.float32)


def _attn_step(state, values, next_scores, bias, s_ref, acc_ref):
    (m_prev, l_prev), maxima = state
    n_heads = s_ref.shape[0]
    sums, next_max = [], []
    if values is not None:
        m_new = jnp.maximum(m_prev, maxima)
        alpha = jnp.exp2(m_prev - m_new)
    for h in range(n_heads):
        if values is not None:
            p = jnp.exp2(s_ref[h] - m_new[h:h + 1, :])
            sums.append(jnp.sum(p, axis=0, keepdims=True))
            acc_ref[h] = alpha[h:h + 1, :] * acc_ref[h] + jnp.dot(values[h], p.astype(values[h].dtype),
                                                                   preferred_element_type=jnp.float32)
        if next_scores is not None:
            s = next_scores(h)
            if bias is not None:
                s = s + bias
            s_ref[h] = s
            next_max.append(jnp.max(s, axis=0, keepdims=True))
    carry = (m_new, alpha * l_prev + jnp.concatenate(sums, axis=0)) if values is not None else (m_prev, l_prev)
    return carry, (jnp.concatenate(next_max, axis=0) if next_scores is not None else maxima)


def _softmax_finish(l, acc_ref, o_ref):
    inv = 1.0 / l
    o_t = jnp.concatenate([acc_ref[h] * inv[h:h + 1, :] for h in range(l.shape[0])], axis=0)
    o_ref[...] = o_t.T.astype(o_ref.dtype)


def _mla_kernel(q_ref, k_ref, vt_ref, o_ref, s_ref, acc_ref, *, tq, tk, q_off, s_len):
    p0 = q_off + pl.program_id(2) * tq
    last = _last_block(p0, tq, tk, s_len)
    heads = range(MLA_HEADS_PER_STEP)

    def qk(kb):
        start = pl.multiple_of(kb * tk, tk)
        return lambda h: lax.dot_general(k_ref[pl.ds(start, tk), h * LANES:(h + 1) * LANES],
                                         q_ref[:, h * LANES:(h + 1) * LANES], _NT,
                                         preferred_element_type=jnp.float32)

    def mask_bias(kb):
        return jnp.where(_admissible_t(p0, kb, tq, tk, s_len), 0.0, NEG_BIG)

    def values(kb):
        vt_all = vt_ref[kb]
        return [vt_all[h * V_DIM_A:(h + 1) * V_DIM_A, :] for h in heads]

    def body(kb, state):
        return _attn_step(state, values(kb), qk(kb + 1), None, s_ref, acc_ref)

    def tail(state):
        return _attn_step(state, values(last - 1), qk(last), mask_bias(last), s_ref, acc_ref)

    state = _attn_step((_softmax_init(MLA_HEADS_PER_STEP, tq, acc_ref), None), None, qk(0), mask_bias(0), s_ref,
                       acc_ref)
    state = lax.fori_loop(0, jnp.maximum(last - 1, 0), body, state)
    state = lax.cond(last > 0, tail, lambda st: st, state)
    (_, l), _ = _attn_step(state, values(last), None, None, s_ref, acc_ref)
    _softmax_finish(l, acc_ref, o_ref)


def _mla_call(q, k, vt, *, tq, tk, q_off, s_len):
    b, lq, _ = q.shape
    _check_tiles(tq, tk, q_off, lq)
    s_pad = k.shape[1]
    hg = MLA_HEADS_PER_STEP
    kern = functools.partial(_mla_kernel, tq=tq, tk=tk, q_off=q_off, s_len=s_len)
    once = dict(pipeline_mode=pl.Buffered(1))
    return pl.pallas_call(
        kern, grid=(b, N_HEADS_A // hg, lq // tq),
        in_specs=[pl.BlockSpec((None, tq, hg * LANES), lambda bi, g, qi: (bi, qi, g)),
                  pl.BlockSpec((None, s_pad, hg * LANES), lambda bi, g, qi: (bi, 0, g), **once),
                  pl.BlockSpec((None, s_pad // tk, hg * V_DIM_A, tk), lambda bi, g, qi: (bi, 0, g, 0), **once)],
        out_specs=pl.BlockSpec((None, tq, hg * V_DIM_A), lambda bi, g, qi: (bi, qi, g)),
        out_shape=jax.ShapeDtypeStruct((b, lq, N_HEADS_A * V_DIM_A), MXU_DTYPE),
        scratch_shapes=[pltpu.VMEM((hg, tk, tq), jnp.float32), pltpu.VMEM((hg, V_DIM_A, tq), jnp.float32)],
        compiler_params=_cparams(3), name="mla",
    )(q, k, vt)


def _f2key(x):
    b = lax.bitcast_convert_type(x, jnp.int32)
    return jnp.where(b < 0, b ^ jnp.int32(0x7FFFFFFF), b)


def _key2f(k):
    return lax.bitcast_convert_type(jnp.where(k < 0, k ^ jnp.int32(0x7FFFFFFF), k), jnp.float32)


_COUNT_SLAB = 4 * SUBLANES
_SCORE_GROUP = 4
_COUNT_GROUP = 4
_SEARCH_BISECT_EVERY = 8
_SEARCH_MAX_PROBES = 40 * _SEARCH_BISECT_EVERY


def _dsa_kernel(iq_ref, qb_ref, iw_ref, ik_ref, kb_ref, vbt_ref, tri_ref, o_ref,
                sc_ref, wt_ref, s_ref, acc_ref, *, tq, tk, q_off, s_len, topk):
    p0 = q_off + pl.program_id(1) * tq
    last = _last_block(p0, tq, tk, s_len)
    n_blk = last + 1
    row = lambda v: jnp.full((1, tq), v, jnp.float32)
    colsum = lambda x: jnp.sum(x, axis=0, keepdims=True)

    wt_ref[...] = iw_ref[...].T[:SUBLANES, :]

    def score_block(kb):
        start = pl.multiple_of(kb * tk, tk)
        ikb = ik_ref[pl.ds(start, tk), :]
        acc = jnp.zeros((tk, tq), jnp.float32)
        for h in range(N_IDX_HEADS):
            x = lax.dot_general(ikb, iq_ref[:, h * LANES:(h + 1) * LANES], _NT, preferred_element_type=jnp.float32)
            acc = acc + wt_ref[h:h + 1, :] * jnp.maximum(x, 0.0)
        return acc

    def stats(carry, hi_fill, lo_fill):
        mx, mn, n_pos, n_nonneg = carry
        return (jnp.maximum(mx, jnp.max(hi_fill, axis=0, keepdims=True)),
                jnp.minimum(mn, jnp.min(lo_fill, axis=0, keepdims=True)),
                n_pos + colsum(jnp.where(hi_fill >= FLT_MIN, 1.0, 0.0)),
                n_nonneg + colsum(jnp.where(hi_fill >= 0.0, 1.0, 0.0)))

    def p1_body(kb, carry):
        acc = score_block(kb)
        sc_ref[kb] = acc
        return stats(carry, acc, acc)

    carry = _fori_grouped(last, p1_body, (row(-jnp.inf), row(jnp.inf), row(0.0), row(0.0)), _SCORE_GROUP)
    acc = score_block(last)
    adm = _admissible_t(p0, last, tq, tk, s_len)
    masked = jnp.where(adm, acc, -jnp.inf)
    sc_ref[last] = masked
    mx, mn, n_pos, n_nonneg = stats(carry, masked, jnp.where(adm, acc, jnp.inf))

    qpos = p0 + lax.broadcasted_iota(jnp.int32, (1, tq), 1)
    n_adm = jnp.minimum(((qpos >> _CHUNK_SHIFT) + 1) * CHUNK, s_len).astype(jnp.float32)
    k_f = jnp.float32(topk)
    log_k = jnp.log(k_f)
    phi = lambda c: jnp.log(jnp.maximum(c, 0.25)) - log_k

    few = n_adm <= k_f
    pos = n_pos >= k_f
    at_zero = jnp.logical_and(~pos, n_nonneg >= k_f)
    above_max = _key2f(_f2key(jnp.maximum(mx, LOWEST)) + 1)
    lo = jnp.where(pos, FLT_MIN, LOWEST)
    c_lo = jnp.where(pos, n_pos, n_adm)
    hi = jnp.where(pos | at_zero, above_max, 0.0)
    c_hi = jnp.where(pos | at_zero, 0.0, n_nonneg)
    lo_val = jnp.where(pos, 0.0, jnp.maximum(mn, LOWEST))
    tau = jnp.where(few, LOWEST, jnp.where(at_zero, 0.0, lo))
    r = jnp.where(at_zero & ~few, k_f - n_pos, 0.0)
    done = jnp.where(few | at_zero, 1.0, 0.0)

    def count_ge(g):
        def body(kb, cnt):
            hit = jnp.where(sc_ref[kb] >= g, 1.0, 0.0)
            return cnt + jnp.sum(hit.reshape(tk // _COUNT_SLAB, _COUNT_SLAB, tq), axis=0)

        return colsum(_fori_grouped(n_blk, body, jnp.zeros((_COUNT_SLAB, tq), jnp.float32), _COUNT_GROUP))

    def search_cond(st):
        return jnp.logical_and(st[1] > 0, st[0] < _SEARCH_MAX_PROBES)

    def next_probe(it, lo, hi, f_lo, f_hi, lo_val):
        lok, hik = _f2key(lo), _f2key(hi)
        midk = (lok >> 1) + (hik >> 1) + (lok & hik & 1)
        frac = f_lo / jnp.maximum(f_lo - f_hi, 1e-9)
        g_int = lo_val + (hi - lo_val) * frac
        bisect = (it % _SEARCH_BISECT_EVERY) == (_SEARCH_BISECT_EVERY - 1)
        g = jnp.where(bisect, _key2f(midk), g_int)
        gk = jnp.minimum(jnp.maximum(_f2key(g), lok + 1), jnp.maximum(hik - 1, lok + 1))
        return _key2f(gk)

    def search_body(st):
        it, _, g, lo, hi, c_lo, c_hi, f_lo, f_hi, lo_val, side, tau, r, done = st
        c = count_ge(g)
        f_g = phi(c)
        live = done == 0.0
        up = live & (c >= k_f)
        dn = live & (c < k_f)
        f_hi = jnp.where(up & (side > 0.0), f_hi * 0.5, f_hi)
        f_lo = jnp.where(dn & (side < 0.0), f_lo * 0.5, f_lo)
        lo, c_lo, f_lo, lo_val = (jnp.where(up, g, lo), jnp.where(up, c, c_lo), jnp.where(up, f_g, f_lo),
                                  jnp.where(up, g, lo_val))
        hi, c_hi, f_hi = jnp.where(dn, g, hi), jnp.where(dn, c, c_hi), jnp.where(dn, f_g, f_hi)
        side = jnp.where(up, 1.0, jnp.where(dn, -1.0, side))
        adjacent = (_f2key(lo) + 1) >= _f2key(hi)
        fin = live & ((c_lo == k_f) | adjacent)
        tau = jnp.where(fin, lo, tau)
        r = jnp.where(fin & (c_lo > k_f), k_f - c_hi, r)
        done = jnp.where(fin, 1.0, done)
        pending = jnp.sum(1.0 - done).astype(jnp.int32)
        g = next_probe(it + 1, lo, hi, f_lo, f_hi, lo_val)
        return (it + 1, pending, g, lo, hi, c_lo, c_hi, f_lo, f_hi, lo_val, side, tau, r, done)

    pending0 = jnp.sum(1.0 - done).astype(jnp.int32)
    f_lo0, f_hi0 = phi(c_lo), phi(c_hi)
    st = lax.while_loop(search_cond, search_body,
                        (jnp.int32(0), pending0, next_probe(0, lo, hi, f_lo0, f_hi0, lo_val), lo, hi, c_lo, c_hi,
                         f_lo0, f_hi0, lo_val, row(0.0), tau, r, done))
    tau, r = st[11], st[12]

    @pl.when(jnp.sum(r) > 0.0)
    def _():
        tied = r > 0.0
        budget = jnp.where(tied, r, jnp.inf)

        def body(kb, seen):
            blk = sc_ref[kb]
            eq = jnp.logical_and(blk == tau, tied)
            eq_f = jnp.where(eq, 1.0, 0.0)
            rank = jnp.dot(tri_ref[...], eq_f.astype(MXU_DTYPE), preferred_element_type=jnp.float32) + seen
            sc_ref[kb] = jnp.where(jnp.logical_and(eq, rank > budget), -jnp.inf, blk)
            return seen + colsum(eq_f)

        _fori_grouped(n_blk, body, row(0.0), _COUNT_GROUP)

    heads_per_kv = N_HEADS_B // N_KV_B

    def qk(kb):
        start = pl.multiple_of(kb * tk, tk)
        kblk = kb_ref[pl.ds(start, tk), :]
        return lambda h: lax.dot_general(kblk, qb_ref[:, h * LANES:(h + 1) * LANES], _NT,
                                         preferred_element_type=jnp.float32)

    def sel_bias(kb):
        return jnp.where(sc_ref[kb] >= tau, 0.0, NEG_BIG)

    def values(kb):
        vt_all = vbt_ref[kb]
        return [vt_all[(h // heads_per_kv) * HEAD_DIM_B:(h // heads_per_kv + 1) * HEAD_DIM_B, :]
                for h in range(N_HEADS_B)]

    def p3_body(kb, state):
        return _attn_step(state, values(kb), qk(kb + 1), sel_bias(kb + 1), s_ref, acc_ref)

    state = _attn_step((_softmax_init(N_HEADS_B, tq, acc_ref), None), None, qk(0), sel_bias(0), s_ref, acc_ref)
    state = lax.fori_loop(0, last, p3_body, state)
    (_, l), _ = _attn_step(state, values(last), None, None, s_ref, acc_ref)
    _softmax_finish(l, acc_ref, o_ref)


def _dsa_call(iq, qb, iw, ik, kb, vbt, tri, *, tq, tk, q_off, s_len):
    b, lq, _ = iq.shape
    _check_tiles(tq, tk, q_off, lq)
    s_pad = ik.shape[1]
    n_kb = s_pad // tk
    topk = min(TOPK_MAX, s_len // 4)
    kern = functools.partial(_dsa_kernel, tq=tq, tk=tk, q_off=q_off, s_len=s_len, topk=topk)
    qspec = lambda w: pl.BlockSpec((None, tq, w), lambda bi, qi: (bi, qi, 0))
    once = dict(pipeline_mode=pl.Buffered(1))
    kspec = pl.BlockSpec((None, s_pad, LANES), lambda bi, qi: (bi, 0, 0), **once)
    return pl.pallas_call(
        kern, grid=(b, lq // tq),
        in_specs=[qspec(N_IDX_HEADS * LANES), qspec(N_HEADS_B * LANES), qspec(LANES), kspec, kspec,
                  pl.BlockSpec((None, n_kb, LANES, tk), lambda bi, qi: (bi, 0, 0, 0), **once),
                  _const_spec(tri.shape)],
        out_specs=qspec(N_HEADS_B * HEAD_DIM_B),
        out_shape=jax.ShapeDtypeStruct((b, lq, N_HEADS_B * HEAD_DIM_B), MXU_DTYPE),
        scratch_shapes=[pltpu.VMEM((n_kb, tk, tq), jnp.float32), pltpu.VMEM((SUBLANES, tq), jnp.float32),
                        pltpu.VMEM((N_HEADS_B, tk, tq), jnp.float32),
                        pltpu.VMEM((N_HEADS_B, HEAD_DIM_B, tq), jnp.float32)],
        compiler_params=_cparams(2), name="dsa",
    )(iq, qb, iw, ik, kb, vbt, tri)


_FF_CHUNK = 256


def _out_kernel(x_ref, oa_ref, ob_ref, woa_ref, wob_ref, gf_ref, wg_ref, wu_ref, wd_ref, gfin_ref, y_ref):
    o = (jnp.dot(oa_ref[...], woa_ref[...], preferred_element_type=jnp.float32)
         + jnp.dot(ob_ref[...], wob_ref[...], preferred_element_type=jnp.float32))
    x1 = x_ref[...] + o
    h2 = _rms(x1, gf_ref[...]).astype(MXU_DTYPE)
    ffn = jnp.zeros_like(x1)
    for j in range(D_FF // _FF_CHUNK):
        sl = slice(j * _FF_CHUNK, (j + 1) * _FF_CHUNK)
        gate = jnp.dot(h2, wg_ref[:, sl], preferred_element_type=jnp.float32)
        up = jnp.dot(h2, wu_ref[:, sl], preferred_element_type=jnp.float32)
        act = (gate * jax.nn.sigmoid(gate)) * up
        ffn = ffn + jnp.dot(act.astype(MXU_DTYPE), wd_ref[sl, :], preferred_element_type=jnp.float32)
    y_ref[...] = _rms(x1 + ffn, gfin_ref[...])


def _out_call(x2d, oa, ob, woa, wob, g_ffn, wg, wu, wd, g_fin, tm):
    t_tok = x2d.shape[0]
    row = lambda w: pl.BlockSpec((tm, w), lambda i: (i, 0))
    half = N_HEADS_A * V_DIM_A
    return pl.pallas_call(
        _out_kernel, grid=(t_tok // tm,),
        in_specs=[row(D_MODEL), row(half), row(half), _const_spec(woa.shape), _const_spec(wob.shape),
                  _const_spec((1, D_MODEL)), _const_spec(wg.shape), _const_spec(wu.shape), _const_spec(wd.shape),
                  _const_spec((1, D_MODEL))],
        out_specs=row(D_MODEL), out_shape=jax.ShapeDtypeStruct((t_tok, D_MODEL), jnp.float32),
        compiler_params=_cparams(1), name="out",
    )(x2d, oa, ob, woa, wob, g_ffn, wg, wu, wd, g_fin)


def _prep_weights(w_in, w_q_b, w_kv_b, w_o, w_gate, w_up, w_down):
    off = np.concatenate([[0], np.cumsum(IN_SIZES)])
    o_qlat, o_kvlat, o_kr, o_qb, o_kb, o_vb, o_iq, o_ik, o_iw = off[:9]
    zeros = lambda n: jnp.zeros((D_MODEL, n), w_in.dtype)
    cols = [w_in[:, o_qlat:o_qlat + Q_LORA], w_in[:, o_kvlat:o_kvlat + KV_LORA],
            w_in[:, o_qb:o_qb + N_HEADS_B * HEAD_DIM_B], w_in[:, o_iq:o_iq + N_IDX_HEADS * IDX_DIM]]
    cols += [w_in[:, o_kb:o_kb + LANES], w_in[:, o_vb:o_vb + LANES],
             w_in[:, o_ik:o_ik + IDX_DIM], zeros(LANES - IDX_DIM),
             zeros(NOPE_DIM), w_in[:, o_kr:o_kr + ROPE_DIM], zeros(LANES - NOPE_DIM - ROPE_DIM),
             w_in[:, o_iw:o_iw + N_IDX_HEADS], zeros(LANES - N_IDX_HEADS)]
    w_in_p = jnp.concatenate(cols, axis=1).astype(MXU_DTYPE)
    assert w_in_p.shape[1] == _C_END

    qd = NOPE_DIM + ROPE_DIM
    w_qb_p = jnp.pad(w_q_b.reshape(Q_LORA, N_HEADS_A, qd), ((0, 0), (0, 0), (0, LANES - qd)))
    w_qb_p = w_qb_p.reshape(Q_LORA, N_HEADS_A * LANES).astype(MXU_DTYPE)

    kv = w_kv_b.reshape(KV_LORA, N_HEADS_A, NOPE_DIM + V_DIM_A)
    wk = jnp.pad(kv[:, :, :NOPE_DIM], ((0, 0), (0, 0), (0, LANES - NOPE_DIM)))
    wk = wk.reshape(KV_LORA, N_HEADS_A * LANES).astype(MXU_DTYPE)
    wvt = kv[:, :, NOPE_DIM:].reshape(KV_LORA, N_HEADS_A * V_DIM_A).T.astype(MXU_DTYPE)

    half = N_HEADS_A * V_DIM_A
    return (w_in_p, w_qb_p, wk, wvt, w_o[:half].astype(MXU_DTYPE), w_o[half:].astype(MXU_DTYPE),
            w_gate.astype(MXU_DTYPE), w_up.astype(MXU_DTYPE), w_down.astype(MXU_DTYPE))


def _rope_tables(pos):
    posf = pos.astype(jnp.float32)[:, None]
    n = pos.shape[0]

    def cs(d):
        inv = jnp.power(ROPE_THETA, -jnp.arange(0, d, 2, dtype=jnp.float32) / d)
        ang = posf * inv[None, :]
        return jnp.cos(ang), jnp.sin(ang)

    one, zero = (lambda w: jnp.ones((n, w), jnp.float32)), (lambda w: jnp.zeros((n, w), jnp.float32))
    c, s = cs(ROPE_DIM)
    cq = jnp.concatenate([one(NOPE_DIM), c, c, one(LANES - NOPE_DIM - ROPE_DIM)], axis=1)
    sq = jnp.concatenate([zero(NOPE_DIM), -s, s, zero(LANES - NOPE_DIM - ROPE_DIM)], axis=1)
    c, s = cs(ROT_DIM_B)
    c64 = jnp.tile(jnp.concatenate([c, c, one(HEAD_DIM_B - ROT_DIM_B)], axis=1), (1, LANES // HEAD_DIM_B))
    s64 = jnp.tile(jnp.concatenate([-s, s, zero(HEAD_DIM_B - ROT_DIM_B)], axis=1), (1, LANES // HEAD_DIM_B))
    return cq, sq, c64, s64


def _pick_tile(n, pref):
    t = min(pref, n)
    while n % t:
        t //= 2
    return t


def _round_up(n, m):
    return -(-n // m) * m


def _layer(x, pos0, past, weights, norms, final_norm):
    b, l, _ = x.shape
    g_attn, g_q, g_kv, g_ffn = norms
    w_in_p, w_qb_p, wk, wvt, woa, wob, wg, wu, wd = weights
    t_tok = b * l
    x2d = x.reshape(t_tok, D_MODEL)

    tabs = _rope_tables(pos0 + jnp.arange(l))
    if l >= TM // 2:
        tm = _pick_tile(l, TM)
        n_tab_tiles = l // tm
    else:
        tm = _pick_tile(t_tok, TM)
        tabs = tuple(jnp.tile(t, (tm // l, 1)) for t in tabs)
        n_tab_tiles = 1
    (ckv, kr128, kb, vb, ik128, iw, q_mla, q_b, iq, kb16, vb16, ik16) = _proj_call(
        x2d, tabs, n_tab_tiles, tm, g_attn, w_in_p, g_q, w_qb_p, g_kv)

    new = (ckv.reshape(b, l, KV_LORA), kr128.reshape(b, l, LANES)[..., NOPE_DIM:NOPE_DIM + ROPE_DIM],
           kb.reshape(b, l, N_KV_B, HEAD_DIM_B), vb.reshape(b, l, N_KV_B, HEAD_DIM_B),
           ik128.reshape(b, l, LANES)[..., :IDX_DIM])

    per_row = lambda a: a.reshape(b, l, a.shape[-1])
    p_len = 0 if past is None else past[0].shape[1]
    s_len = p_len + l
    lq = _round_up(l, LANES)
    tq_mla, tq_dsa = _pick_tile(lq, TQ_MLA), _pick_tile(lq, TQ_DSA)
    tk_mla, tk_dsa = (2 * TK_MLA, 2 * TK_DSA) if lq < TQ_DSA else (TK_MLA, TK_DSA)
    s_pad = _round_up(s_len, max(tk_mla, tk_dsa))
    if past is None:
        old = [None] * 5
    else:
        p_lat, p_kr, p_k, p_v, p_ik = past
        old = [p_lat,
               jnp.pad(p_kr, ((0, 0), (0, 0), (NOPE_DIM, LANES - NOPE_DIM - ROPE_DIM))),
               p_k.reshape(b, p_len, LANES).astype(MXU_DTYPE), p_v.reshape(b, p_len, LANES).astype(MXU_DTYPE),
               jnp.pad(p_ik, ((0, 0), (0, 0), (0, LANES - IDX_DIM))).astype(MXU_DTYPE)]

    def full_seq(prev, cur):
        parts = ([] if prev is None else [prev]) + [per_row(cur)]
        if s_pad > s_len:
            parts.append(jnp.zeros((b, s_pad - s_len, cur.shape[-1]), cur.dtype))
        return parts[0] if len(parts) == 1 else jnp.concatenate(parts, axis=1)

    c_all, kr_all, kb_all, vb_all, ik_all = (full_seq(o, c) for o, c in zip(old, (ckv, kr128, kb16, vb16, ik16)))
    vbt_all = vb_all.reshape(b, s_pad // tk_dsa, tk_dsa, LANES).transpose(0, 1, 3, 2)

    k_mla, vt_mla = _kvup_call(c_all, kr_all, wk, wvt, tk_mla)

    padq = lambda a: per_row(a) if lq == l else jnp.pad(per_row(a), ((0, 0), (0, lq - l), (0, 0)))
    o_a = _mla_call(padq(q_mla), k_mla, vt_mla, tq=tq_mla, tk=tk_mla, q_off=pos0, s_len=s_len)
    tri = jnp.tril(jnp.ones((tk_dsa, tk_dsa), MXU_DTYPE))
    o_b = _dsa_call(padq(iq), padq(q_b), padq(iw), ik_all, kb_all, vbt_all, tri,
                    tq=tq_dsa, tk=tk_dsa, q_off=pos0, s_len=s_len)
    o_a, o_b = (o[:, :l].reshape(t_tok, -1) for o in (o_a, o_b))

    y = _out_call(x2d, o_a, o_b, woa, wob, g_ffn, wg, wu, wd, final_norm, tm)
    return y.reshape(b, l, D_MODEL), new


def kernel(x_prompt, x_sample, cache_mla_latent, cache_mla_krope, cache_dsa_k, cache_dsa_v, cache_idx_k, attn_norm, w_in, q_a_norm, w_q_b, kv_a_norm, w_kv_b, w_o, ffn_norm, w_gate, w_up, w_down, final_norm):
    depth = w_in.shape[0]
    assert depth == 1, "the fused final norm assumes a single layer"
    past_len = cache_mla_latent.shape[2]
    li = 0
    weights = _prep_weights(w_in[li], w_q_b[li], w_kv_b[li], w_o[li], w_gate[li], w_up[li], w_down[li])
    norms = (attn_norm[li][None, :], q_a_norm[li][None, :], kv_a_norm[li][None, :], ffn_norm[li][None, :])
    fin = final_norm[None, :]
    past = (cache_mla_latent[li], cache_mla_krope[li], cache_dsa_k[li], cache_dsa_v[li], cache_idx_k[li])
    y_p, new_p = _layer(x_prompt, 0, None, weights, norms, fin)
    y_s, new_s = _layer(x_sample, past_len, past, weights, norms, fin)
    return (y_p, y_s) + tuple(a[None] for a in new_p) + tuple(a[None] for a in new_s)
```

```python
import functools

import numpy as np
import jax
import jax.numpy as jnp
from jax import lax
from jax.experimental import pallas as pl
from jax.experimental.pallas import tpu as pltpu

D_MODEL = 1024
CHUNK = 64
ROPE_THETA = 500000.0
EPS = 1e-6
N_HEADS_A = 8
NOPE_DIM = 64
ROPE_DIM = 32
V_DIM_A = 64
Q_LORA = 384
KV_LORA = 256
N_HEADS_B = 8
N_KV_B = 2
HEAD_DIM_B = 64
ROT_DIM_B = 16
N_IDX_HEADS = 8
IDX_DIM = 64
ROT_DIM_IDX = 16
TOPK_MAX = 256
D_FF = 2816
IN_SIZES = (Q_LORA, KV_LORA, ROPE_DIM, N_HEADS_B * HEAD_DIM_B, N_KV_B * HEAD_DIM_B, N_KV_B * HEAD_DIM_B,
            N_IDX_HEADS * IDX_DIM, IDX_DIM, N_IDX_HEADS)

LANES = 128
SUBLANES = 8
MXU_DTYPE = jnp.bfloat16
VMEM_LIMIT = 60 * 1024 * 1024
NEG_BIG = -1e30
LOWEST = -3.0e38
FLT_MIN = 1.1754943508222875e-38
LOG2E = 1.4426950408889634

TM = 512
TQ_MLA = 512
TK_MLA = 512
TQ_DSA = 256
TK_DSA = 512
MLA_HEADS_PER_STEP = 4

_C_QLAT = 0
_C_KVLAT = _C_QLAT + Q_LORA
_C_QB = _C_KVLAT + KV_LORA
_C_IQ = _C_QB + N_HEADS_B * HEAD_DIM_B
_C_KB = _C_IQ + N_IDX_HEADS * IDX_DIM
_C_VB = _C_KB + LANES
_C_IK = _C_VB + LANES
_C_KR = _C_IK + LANES
_C_IW = _C_KR + LANES
_C_END = _C_IW + LANES


def _cparams(n_axes):
    return pltpu.CompilerParams(dimension_semantics=("arbitrary",) * n_axes, vmem_limit_bytes=VMEM_LIMIT)


def _const_spec(shape):
    return pl.BlockSpec(shape, lambda *_: (0,) * len(shape), pipeline_mode=pl.Buffered(1))


def _rms(x, g):
    return (x * lax.rsqrt(jnp.mean(x * x, axis=-1, keepdims=True) + EPS)) * g


def _rope_tile(x, cos, sin, half, first_half):
    partner = jnp.where(first_half, pltpu.roll(x, LANES - half, 1), pltpu.roll(x, half, 1))
    return x * cos + partner * sin


def _proj_kernel(x_ref, cq_ref, sq_ref, c64_ref, s64_ref, gat_ref, win_ref, gq_ref, wqb_ref, gkv_ref,
                 ckv_ref, kr_ref, kb_ref, vb_ref, ik_ref, iw_ref, qm_ref, qb_ref, iq_ref,
                 kb16_ref, vb16_ref, ik16_ref):
    x = x_ref[...]
    hb = _rms(x, gat_ref[...]).astype(MXU_DTYPE)
    y = jnp.dot(hb, win_ref[...], preferred_element_type=jnp.float32)

    lane = lax.broadcasted_iota(jnp.int32, (1, LANES), 1)
    first_q = lane < NOPE_DIM + ROPE_DIM // 2
    first_64 = (lane & (HEAD_DIM_B - 1)) < ROT_DIM_B // 2
    cq, sq, c64, s64 = cq_ref[...], sq_ref[...], c64_ref[...], s64_ref[...]

    qn = _rms(y[:, _C_QLAT:_C_QLAT + Q_LORA], gq_ref[...]).astype(MXU_DTYPE)
    qa = jnp.dot(qn, wqb_ref[...], preferred_element_type=jnp.float32)
    scale_a = (NOPE_DIM + ROPE_DIM) ** -0.5 * LOG2E
    for h in range(N_HEADS_A):
        t = _rope_tile(qa[:, h * LANES:(h + 1) * LANES], cq, sq, ROPE_DIM // 2, first_q)
        qm_ref[:, h * LANES:(h + 1) * LANES] = (t * scale_a).astype(qm_ref.dtype)

    ckv_ref[...] = _rms(y[:, _C_KVLAT:_C_KVLAT + KV_LORA], gkv_ref[...])
    kr_ref[...] = _rope_tile(y[:, _C_KR:_C_KR + LANES], cq, sq, ROPE_DIM // 2, first_q)

    scale_b = HEAD_DIM_B ** -0.5 * LOG2E
    low = lane < HEAD_DIM_B
    heads_per_kv = N_HEADS_B // N_KV_B
    for t in range(N_HEADS_B // 2):
        pair = _rope_tile(y[:, _C_QB + t * LANES:_C_QB + (t + 1) * LANES], c64, s64, ROT_DIM_B // 2, first_64)
        pair = pair * scale_b
        swapped = pltpu.roll(pair, HEAD_DIM_B, 1)
        for u in range(2):
            h = 2 * t + u
            half = h // heads_per_kv
            src = pair if u == half else swapped
            tile = jnp.where(low if half == 0 else jnp.logical_not(low), src, 0.0)
            qb_ref[:, h * LANES:(h + 1) * LANES] = tile.astype(qb_ref.dtype)
    for t in range(N_IDX_HEADS // 2):
        pair = _rope_tile(y[:, _C_IQ + t * LANES:_C_IQ + (t + 1) * LANES], c64, s64, ROT_DIM_IDX // 2, first_64)
        swapped = pltpu.roll(pair, IDX_DIM, 1)
        for u in range(2):
            h = 2 * t + u
            iq_ref[:, h * LANES:(h + 1) * LANES] = jnp.where(low, pair if u == 0 else swapped,
                                                             0.0).astype(iq_ref.dtype)

    kb = _rope_tile(y[:, _C_KB:_C_KB + LANES], c64, s64, ROT_DIM_B // 2, first_64)
    kb_ref[...] = kb
    kb16_ref[...] = kb.astype(kb16_ref.dtype)
    vb = y[:, _C_VB:_C_VB + LANES]
    vb_ref[...] = vb
    vb16_ref[...] = vb.astype(vb16_ref.dtype)
    ik = _rope_tile(y[:, _C_IK:_C_IK + LANES], c64, s64, ROT_DIM_IDX // 2, first_64)
    ik_ref[...] = ik
    ik16_ref[...] = ik.astype(ik16_ref.dtype)
    iw_ref[...] = y[:, _C_IW:_C_IW + LANES] * ((N_IDX_HEADS * IDX_DIM) ** -0.5)


def _proj_call(x2d, tabs, n_tab_tiles, tm, g_attn, w_in_p, g_q, w_qb_p, g_kv):
    t_tok = x2d.shape[0]
    grid = (t_tok // tm,)
    row = lambda w: pl.BlockSpec((tm, w), lambda i: (i, 0))
    tab = pl.BlockSpec((tm, LANES), lambda i: (i % n_tab_tiles, 0))
    f32, b16 = jnp.float32, MXU_DTYPE
    out_shapes = [
        jax.ShapeDtypeStruct((t_tok, KV_LORA), f32),
        jax.ShapeDtypeStruct((t_tok, LANES), f32),
        jax.ShapeDtypeStruct((t_tok, LANES), f32),
        jax.ShapeDtypeStruct((t_tok, LANES), f32),
        jax.ShapeDtypeStruct((t_tok, LANES), f32),
        jax.ShapeDtypeStruct((t_tok, LANES), f32),
        jax.ShapeDtypeStruct((t_tok, N_HEADS_A * LANES), b16),
        jax.ShapeDtypeStruct((t_tok, N_HEADS_B * LANES), b16),
        jax.ShapeDtypeStruct((t_tok, N_IDX_HEADS * LANES), b16),
        jax.ShapeDtypeStruct((t_tok, LANES), b16),
        jax.ShapeDtypeStruct((t_tok, LANES), b16),
        jax.ShapeDtypeStruct((t_tok, LANES), b16),
    ]
    out_specs = [row(s.shape[1]) for s in out_shapes]
    in_specs = [row(D_MODEL), tab, tab, tab, tab, _const_spec((1, D_MODEL)), _const_spec(w_in_p.shape),
                _const_spec((1, Q_LORA)), _const_spec(w_qb_p.shape), _const_spec((1, KV_LORA))]
    return pl.pallas_call(
        _proj_kernel, grid=grid, in_specs=in_specs, out_specs=out_specs, out_shape=out_shapes,
        compiler_params=_cparams(1), name="proj",
    )(x2d, *tabs, g_attn, w_in_p, g_q, w_qb_p, g_kv)


_NT = (((1,), (1,)), ((), ()))


def _kvup_kernel(c_ref, kr_ref, wk_ref, wvt_ref, k_ref, vt_ref):
    cb = c_ref[...].astype(MXU_DTYPE)
    kn = jnp.dot(cb, wk_ref[...], preferred_element_type=jnp.float32)
    kr = kr_ref[...]
    for h in range(N_HEADS_A):
        k_ref[:, h * LANES:(h + 1) * LANES] = (kn[:, h * LANES:(h + 1) * LANES] + kr).astype(k_ref.dtype)
    vt_ref[...] = lax.dot_general(wvt_ref[...], cb, _NT, preferred_element_type=jnp.float32).astype(vt_ref.dtype)


def _kvup_call(c3d, kr3d, wk, wvt, tk):
    b, s_pad, _ = c3d.shape
    row = lambda w: pl.BlockSpec((None, tk, w), lambda bi, i: (bi, i, 0))
    vdim = N_HEADS_A * V_DIM_A
    return pl.pallas_call(
        _kvup_kernel, grid=(b, s_pad // tk),
        in_specs=[row(KV_LORA), row(LANES), _const_spec(wk.shape), _const_spec(wvt.shape)],
        out_specs=[row(N_HEADS_A * LANES), pl.BlockSpec((None, None, vdim, tk), lambda bi, i: (bi, i, 0, 0))],
        out_shape=[jax.ShapeDtypeStruct((b, s_pad, N_HEADS_A * LANES), MXU_DTYPE),
                   jax.ShapeDtypeStruct((b, s_pad // tk, vdim, tk), MXU_DTYPE)],
        compiler_params=_cparams(2), name="kvup",
    )(c3d, kr3d, wk, wvt)


_CHUNK_SHIFT = CHUNK.bit_length() - 1


def _admissible_t(p0, kb, tq, tk, s_len):
    kpos = kb * tk + lax.broadcasted_iota(jnp.int32, (tk, 1), 0)
    qpos = p0 + lax.broadcasted_iota(jnp.int32, (1, tq), 1)
    return ((qpos >> _CHUNK_SHIFT) >= (kpos >> _CHUNK_SHIFT)) & (kpos < s_len)


def _check_tiles(tq, tk, q_off, lq):
    assert tk & (tk - 1) == 0 and tk % tq == 0 and tk % CHUNK == 0, (tq, tk)
    assert q_off % tq == 0 or (lq == tq and q_off % tk + tq <= tk), (tq, tk, q_off, lq)


def _last_block(p0, tq, tk, s_len):
    return jnp.minimum(p0 + tq - 1, s_len - 1) >> (tk.bit_length() - 1)


def _fori_grouped(n, body, carry, group):
    shift = group.bit_length() - 1

    def several(j, c):
        for i in range(group):
            c = body(group * j + i, c)
        return c

    carry = lax.fori_loop(0, n >> shift, several, carry)
    return lax.fori_loop((n >> shift) << shift, n, body, carry)


def _softmax_init(n_heads, tq, acc_ref):
    acc_ref[...] = jnp.zeros(acc_ref.shape, jnp.float32)
    return jnp.full((n_heads, tq), NEG_BIG, jnp.float32), jnp.zeros((n_heads, tq), jnp.float32)


def _attn_step(state, values, next_scores, bias, s_ref, acc_ref):
    (m_prev, l_prev), maxima = state
    n_heads = s_ref.shape[0]
    sums, next_max = [], []
    if values is not None:
        m_new = jnp.maximum(m_prev, maxima)
        alpha = jnp.exp2(m_prev - m_new)
    for h in range(n_heads):
        if values is not None:
            p = jnp.exp2(s_ref[h] - m_new[h:h + 1, :])
            sums.append(jnp.sum(p, axis=0, keepdims=True))
            acc_ref[h] = alpha[h:h + 1, :] * acc_ref[h] + jnp.dot(values[h], p.astype(values[h].dtype),
                                                                   preferred_element_type=jnp.float32)
        if next_scores is not None:
            s = next_scores(h)
            if bias is not None:
                s = s + bias
            s_ref[h] = s
            next_max.append(jnp.max(s, axis=0, keepdims=True))
    carry = (m_new, alpha * l_prev + jnp.concatenate(sums, axis=0)) if values is not None else (m_prev, l_prev)
    return carry, (jnp.concatenate(next_max, axis=0) if next_scores is not None else maxima)


def _softmax_finish(l, acc_ref, o_ref):
    inv = 1.0 / l
    o_t = jnp.concatenate([acc_ref[h] * inv[h:h + 1, :] for h in range(l.shape[0])], axis=0)
    o_ref[...] = o_t.T.astype(o_ref.dtype)


def _mla_kernel(q_ref, k_ref, vt_ref, o_ref, s_ref, acc_ref, *, tq, tk, q_off, s_len):
    p0 = q_off + pl.program_id(2) * tq
    last = _last_block(p0, tq, tk, s_len)
    heads = range(MLA_HEADS_PER_STEP)

    def qk(kb):
        start = pl.multiple_of(kb * tk, tk)
        return lambda h: lax.dot_general(k_ref[pl.ds(start, tk), h * LANES:(h + 1) * LANES],
                                         q_ref[:, h * LANES:(h + 1) * LANES], _NT,
                                         preferred_element_type=jnp.float32)

    def mask_bias(kb):
        return jnp.where(_admissible_t(p0, kb, tq, tk, s_len), 0.0, NEG_BIG)

    def values(kb):
        vt_all = vt_ref[kb]
        return [vt_all[h * V_DIM_A:(h + 1) * V_DIM_A, :] for h in heads]

    def body(kb, state):
        return _attn_step(state, values(kb), qk(kb + 1), None, s_ref, acc_ref)

    def tail(state):
        return _attn_step(state, values(last - 1), qk(last), mask_bias(last), s_ref, acc_ref)

    state = _attn_step((_softmax_init(MLA_HEADS_PER_STEP, tq, acc_ref), None), None, qk(0), mask_bias(0), s_ref,
                       acc_ref)
    state = lax.fori_loop(0, jnp.maximum(last - 1, 0), body, state)
    state = lax.cond(last > 0, tail, lambda st: st, state)
    (_, l), _ = _attn_step(state, values(last), None, None, s_ref, acc_ref)
    _softmax_finish(l, acc_ref, o_ref)


def _mla_call(q, k, vt, *, tq, tk, q_off, s_len):
    b, lq, _ = q.shape
    _check_tiles(tq, tk, q_off, lq)
    s_pad = k.shape[1]
    hg = MLA_HEADS_PER_STEP
    kern = functools.partial(_mla_kernel, tq=tq, tk=tk, q_off=q_off, s_len=s_len)
    return pl.pallas_call(
        kern, grid=(b, N_HEADS_A // hg, lq // tq),
        in_specs=[pl.BlockSpec((None, tq, hg * LANES), lambda bi, g, qi: (bi, qi, g)),
                  pl.BlockSpec((None, s_pad, hg * LANES), lambda bi, g, qi: (bi, 0, g)),
                  pl.BlockSpec((None, s_pad // tk, hg * V_DIM_A, tk), lambda bi, g, qi: (bi, 0, g, 0))],
        out_specs=pl.BlockSpec((None, tq, hg * V_DIM_A), lambda bi, g, qi: (bi, qi, g)),
        out_shape=jax.ShapeDtypeStruct((b, lq, N_HEADS_A * V_DIM_A), MXU_DTYPE),
        scratch_shapes=[pltpu.VMEM((hg, tk, tq), jnp.float32), pltpu.VMEM((hg, V_DIM_A, tq), jnp.float32)],
        compiler_params=_cparams(3), name="mla",
    )(q, k, vt)


def _f2key(x):
    b = lax.bitcast_convert_type(x, jnp.int32)
    return jnp.where(b < 0, b ^ jnp.int32(0x7FFFFFFF), b)


def _key2f(k):
    return lax.bitcast_convert_type(jnp.where(k < 0, k ^ jnp.int32(0x7FFFFFFF), k), jnp.float32)


_COUNT_SLAB = 4 * SUBLANES
_SCORE_GROUP = 4
_COUNT_GROUP = 4
_SEARCH_BISECT_EVERY = 8
_SEARCH_MAX_PROBES = 40 * _SEARCH_BISECT_EVERY


def _dsa_kernel(iq_ref, qb_ref, iw_ref, ik_ref, kb_ref, vbt_ref, tri_ref, o_ref,
                sc_ref, wt_ref, s_ref, acc_ref, *, tq, tk, q_off, s_len, topk):
    p0 = q_off + pl.program_id(1) * tq
    last = _last_block(p0, tq, tk, s_len)
    n_blk = last + 1
    row = lambda v: jnp.full((1, tq), v, jnp.float32)
    colsum = lambda x: jnp.sum(x, axis=0, keepdims=True)

    wt_ref[...] = iw_ref[...].T[:SUBLANES, :]

    def score_block(kb):
        start = pl.multiple_of(kb * tk, tk)
        ikb = ik_ref[pl.ds(start, tk), :]
        acc = jnp.zeros((tk, tq), jnp.float32)
        for h in range(N_IDX_HEADS):
            x = lax.dot_general(ikb, iq_ref[:, h * LANES:(h + 1) * LANES], _NT, preferred_element_type=jnp.float32)
            acc = acc + wt_ref[h:h + 1, :] * jnp.maximum(x, 0.0)
        return acc

    def stats(carry, hi_fill, lo_fill):
        mx, mn, n_pos, n_nonneg = carry
        return (jnp.maximum(mx, jnp.max(hi_fill, axis=0, keepdims=True)),
                jnp.minimum(mn, jnp.min(lo_fill, axis=0, keepdims=True)),
                n_pos + colsum(jnp.where(hi_fill >= FLT_MIN, 1.0, 0.0)),
                n_nonneg + colsum(jnp.where(hi_fill >= 0.0, 1.0, 0.0)))

    def p1_body(kb, carry):
        acc = score_block(kb)
        sc_ref[kb] = acc
        return stats(carry, acc, acc)

    carry = _fori_grouped(last, p1_body, (row(-jnp.inf), row(jnp.inf), row(0.0), row(0.0)), _SCORE_GROUP)
    acc = score_block(last)
    adm = _admissible_t(p0, last, tq, tk, s_len)
    masked = jnp.where(adm, acc, -jnp.inf)
    sc_ref[last] = masked
    mx, mn, n_pos, n_nonneg = stats(carry, masked, jnp.where(adm, acc, jnp.inf))

    qpos = p0 + lax.broadcasted_iota(jnp.int32, (1, tq), 1)
    n_adm = jnp.minimum(((qpos >> _CHUNK_SHIFT) + 1) * CHUNK, s_len).astype(jnp.float32)
    k_f = jnp.float32(topk)
    log_k = jnp.log(k_f)
    phi = lambda c: jnp.log(jnp.maximum(c, 0.25)) - log_k

    few = n_adm <= k_f
    pos = n_pos >= k_f
    at_zero = jnp.logical_and(~pos, n_nonneg >= k_f)
    above_max = _key2f(_f2key(jnp.maximum(mx, LOWEST)) + 1)
    lo = jnp.where(pos, FLT_MIN, LOWEST)
    c_lo = jnp.where(pos, n_pos, n_adm)
    hi = jnp.where(pos | at_zero, above_max, 0.0)
    c_hi = jnp.where(pos | at_zero, 0.0, n_nonneg)
    lo_val = jnp.where(pos, 0.0, jnp.maximum(mn, LOWEST))
    tau = jnp.where(few, LOWEST, jnp.where(at_zero, 0.0, lo))
    r = jnp.where(at_zero & ~few, k_f - n_pos, 0.0)
    done = jnp.where(few | at_zero, 1.0, 0.0)

    def count_ge(g):
        def body(kb, cnt):
            hit = jnp.where(sc_ref[kb] >= g, 1.0, 0.0)
            return cnt + jnp.sum(hit.reshape(tk // _COUNT_SLAB, _COUNT_SLAB, tq), axis=0)

        return colsum(_fori_grouped(n_blk, body, jnp.zeros((_COUNT_SLAB, tq), jnp.float32), _COUNT_GROUP))

    def search_cond(st):
        return jnp.logical_and(st[1] > 0, st[0] < _SEARCH_MAX_PROBES)

    def next_probe(it, lo, hi, f_lo, f_hi, lo_val):
        lok, hik = _f2key(lo), _f2key(hi)
        midk = (lok >> 1) + (hik >> 1) + (lok & hik & 1)
        frac = f_lo / jnp.maximum(f_lo - f_hi, 1e-9)
        g_int = lo_val + (hi - lo_val) * frac
        bisect = (it % _SEARCH_BISECT_EVERY) == (_SEARCH_BISECT_EVERY - 1)
        g = jnp.where(bisect, _key2f(midk), g_int)
        gk = jnp.minimum(jnp.maximum(_f2key(g), lok + 1), jnp.maximum(hik - 1, lok + 1))
        return _key2f(gk)

    def search_body(st):
        it, _, g, lo, hi, c_lo, c_hi, f_lo, f_hi, lo_val, side, tau, r, done = st
        c = count_ge(g)
        f_g = phi(c)
        live = done == 0.0
        up = live & (c >= k_f)
        dn = live & (c < k_f)
        f_hi = jnp.where(up & (side > 0.0), f_hi * 0.5, f_hi)
        f_lo = jnp.where(dn & (side < 0.0), f_lo * 0.5, f_lo)
        lo, c_lo, f_lo, lo_val = (jnp.where(up, g, lo), jnp.where(up, c, c_lo), jnp.where(up, f_g, f_lo),
                                  jnp.where(up, g, lo_val))
        hi, c_hi, f_hi = jnp.where(dn, g, hi), jnp.where(dn, c, c_hi), jnp.where(dn, f_g, f_hi)
        side = jnp.where(up, 1.0, jnp.where(dn, -1.0, side))
        adjacent = (_f2key(lo) + 1) >= _f2key(hi)
        fin = live & ((c_lo == k_f) | adjacent)
        tau = jnp.where(fin, lo, tau)
        r = jnp.where(fin & (c_lo > k_f), k_f - c_hi, r)
        done = jnp.where(fin, 1.0, done)
        pending = jnp.sum(1.0 - done).astype(jnp.int32)
        g = next_probe(it + 1, lo, hi, f_lo, f_hi, lo_val)
        return (it + 1, pending, g, lo, hi, c_lo, c_hi, f_lo, f_hi, lo_val, side, tau, r, done)

    pending0 = jnp.sum(1.0 - done).astype(jnp.int32)
    f_lo0, f_hi0 = phi(c_lo), phi(c_hi)
    st = lax.while_loop(search_cond, search_body,
                        (jnp.int32(0), pending0, next_probe(0, lo, hi, f_lo0, f_hi0, lo_val), lo, hi, c_lo, c_hi,
                         f_lo0, f_hi0, lo_val, row(0.0), tau, r, done))
    tau, r = st[11], st[12]

    @pl.when(jnp.sum(r) > 0.0)
    def _():
        tied = r > 0.0
        budget = jnp.where(tied, r, jnp.inf)

        def body(kb, seen):
            blk = sc_ref[kb]
            eq = jnp.logical_and(blk == tau, tied)
            eq_f = jnp.where(eq, 1.0, 0.0)
            rank = jnp.dot(tri_ref[...], eq_f.astype(MXU_DTYPE), preferred_element_type=jnp.float32) + seen
            sc_ref[kb] = jnp.where(jnp.logical_and(eq, rank > budget), -jnp.inf, blk)
            return seen + colsum(eq_f)

        _fori_grouped(n_blk, body, row(0.0), _COUNT_GROUP)

    heads_per_kv = N_HEADS_B // N_KV_B

    def qk(kb):
        start = pl.multiple_of(kb * tk, tk)
        kblk = kb_ref[pl.ds(start, tk), :]
        return lambda h: lax.dot_general(kblk, qb_ref[:, h * LANES:(h + 1) * LANES], _NT,
                                         preferred_element_type=jnp.float32)

    def sel_bias(kb):
        return jnp.where(sc_ref[kb] >= tau, 0.0, NEG_BIG)

    def values(kb):
        vt_all = vbt_ref[kb]
        return [vt_all[(h // heads_per_kv) * HEAD_DIM_B:(h // heads_per_kv + 1) * HEAD_DIM_B, :]
                for h in range(N_HEADS_B)]

    def p3_body(kb, state):
        return _attn_step(state, values(kb), qk(kb + 1), sel_bias(kb + 1), s_ref, acc_ref)

    state = _attn_step((_softmax_init(N_HEADS_B, tq, acc_ref), None), None, qk(0), sel_bias(0), s_ref, acc_ref)
    state = lax.fori_loop(0, last, p3_body, state)
    (_, l), _ = _attn_step(state, values(last), None, None, s_ref, acc_ref)
    _softmax_finish(l, acc_ref, o_ref)


def _dsa_call(iq, qb, iw, ik, kb, vbt, tri, *, tq, tk, q_off, s_len):
    b, lq, _ = iq.shape
    _check_tiles(tq, tk, q_off, lq)
    s_pad = ik.shape[1]
    n_kb = s_pad // tk
    topk = min(TOPK_MAX, s_len // 4)
    kern = functools.partial(_dsa_kernel, tq=tq, tk=tk, q_off=q_off, s_len=s_len, topk=topk)
    qspec = lambda w: pl.BlockSpec((None, tq, w), lambda bi, qi: (bi, qi, 0))
    once = dict(pipeline_mode=pl.Buffered(1))
    kspec = pl.BlockSpec((None, s_pad, LANES), lambda bi, qi: (bi, 0, 0), **once)
    return pl.pallas_call(
        kern, grid=(b, lq // tq),
        in_specs=[qspec(N_IDX_HEADS * LANES), qspec(N_HEADS_B * LANES), qspec(LANES), kspec, kspec,
                  pl.BlockSpec((None, n_kb, LANES, tk), lambda bi, qi: (bi, 0, 0, 0), **once),
                  _const_spec(tri.shape)],
        out_specs=qspec(N_HEADS_B * HEAD_DIM_B),
        out_shape=jax.ShapeDtypeStruct((b, lq, N_HEADS_B * HEAD_DIM_B), MXU_DTYPE),
        scratch_shapes=[pltpu.VMEM((n_kb, tk, tq), jnp.float32), pltpu.VMEM((SUBLANES, tq), jnp.float32),
                        pltpu.VMEM((N_HEADS_B, tk, tq), jnp.float32),
                        pltpu.VMEM((N_HEADS_B, HEAD_DIM_B, tq), jnp.float32)],
        compiler_params=_cparams(2), name="dsa",
    )(iq, qb, iw, ik, kb, vbt, tri)


_FF_CHUNK = 256


def _out_kernel(x_ref, oa_ref, ob_ref, woa_ref, wob_ref, gf_ref, wg_ref, wu_ref, wd_ref, gfin_ref, y_ref):
    o = (jnp.dot(oa_ref[...], woa_ref[...], preferred_element_type=jnp.float32)
         + jnp.dot(ob_ref[...], wob_ref[...], preferred_element_type=jnp.float32))
    x1 = x_ref[...] + o
    h2 = _rms(x1, gf_ref[...]).astype(MXU_DTYPE)
    ffn = jnp.zeros_like(x1)
    for j in range(D_FF // _FF_CHUNK):
        sl = slice(j * _FF_CHUNK, (j + 1) * _FF_CHUNK)
        gate = jnp.dot(h2, wg_ref[:, sl], preferred_element_type=jnp.float32)
        up = jnp.dot(h2, wu_ref[:, sl], preferred_element_type=jnp.float32)
        act = (gate * jax.nn.sigmoid(gate)) * up
        ffn = ffn + jnp.dot(act.astype(MXU_DTYPE), wd_ref[sl, :], preferred_element_type=jnp.float32)
    y_ref[...] = _rms(x1 + ffn, gfin_ref[...])


def _out_call(x2d, oa, ob, woa, wob, g_ffn, wg, wu, wd, g_fin, tm):
    t_tok = x2d.shape[0]
    row = lambda w: pl.BlockSpec((tm, w), lambda i: (i, 0))
    half = N_HEADS_A * V_DIM_A
    return pl.pallas_call(
        _out_kernel, grid=(t_tok // tm,),
        in_specs=[row(D_MODEL), row(half), row(half), _const_spec(woa.shape), _const_spec(wob.shape),
                  _const_spec((1, D_MODEL)), _const_spec(wg.shape), _const_spec(wu.shape), _const_spec(wd.shape),
                  _const_spec((1, D_MODEL))],
        out_specs=row(D_MODEL), out_shape=jax.ShapeDtypeStruct((t_tok, D_MODEL), jnp.float32),
        compiler_params=_cparams(1), name="out",
    )(x2d, oa, ob, woa, wob, g_ffn, wg, wu, wd, g_fin)


def _prep_weights(w_in, w_q_b, w_kv_b, w_o, w_gate, w_up, w_down):
    off = np.concatenate([[0], np.cumsum(IN_SIZES)])
    o_qlat, o_kvlat, o_kr, o_qb, o_kb, o_vb, o_iq, o_ik, o_iw = off[:9]
    zeros = lambda n: jnp.zeros((D_MODEL, n), w_in.dtype)
    cols = [w_in[:, o_qlat:o_qlat + Q_LORA], w_in[:, o_kvlat:o_kvlat + KV_LORA],
            w_in[:, o_qb:o_qb + N_HEADS_B * HEAD_DIM_B], w_in[:, o_iq:o_iq + N_IDX_HEADS * IDX_DIM]]
    cols += [w_in[:, o_kb:o_kb + LANES], w_in[:, o_vb:o_vb + LANES],
             w_in[:, o_ik:o_ik + IDX_DIM], zeros(LANES - IDX_DIM),
             zeros(NOPE_DIM), w_in[:, o_kr:o_kr + ROPE_DIM], zeros(LANES - NOPE_DIM - ROPE_DIM),
             w_in[:, o_iw:o_iw + N_IDX_HEADS], zeros(LANES - N_IDX_HEADS)]
    w_in_p = jnp.concatenate(cols, axis=1).astype(MXU_DTYPE)
    assert w_in_p.shape[1] == _C_END

    qd = NOPE_DIM + ROPE_DIM
    w_qb_p = jnp.pad(w_q_b.reshape(Q_LORA, N_HEADS_A, qd), ((0, 0), (0, 0), (0, LANES - qd)))
    w_qb_p = w_qb_p.reshape(Q_LORA, N_HEADS_A * LANES).astype(MXU_DTYPE)

    kv = w_kv_b.reshape(KV_LORA, N_HEADS_A, NOPE_DIM + V_DIM_A)
    wk = jnp.pad(kv[:, :, :NOPE_DIM], ((0, 0), (0, 0), (0, LANES - NOPE_DIM)))
    wk = wk.reshape(KV_LORA, N_HEADS_A * LANES).astype(MXU_DTYPE)
    wvt = kv[:, :, NOPE_DIM:].reshape(KV_LORA, N_HEADS_A * V_DIM_A).T.astype(MXU_DTYPE)

    half = N_HEADS_A * V_DIM_A
    return (w_in_p, w_qb_p, wk, wvt, w_o[:half].astype(MXU_DTYPE), w_o[half:].astype(MXU_DTYPE),
            w_gate.astype(MXU_DTYPE), w_up.astype(MXU_DTYPE), w_down.astype(MXU_DTYPE))


def _rope_tables(pos):
    posf = pos.astype(jnp.float32)[:, None]
    n = pos.shape[0]

    def cs(d):
        inv = jnp.power(ROPE_THETA, -jnp.arange(0, d, 2, dtype=jnp.float32) / d)
        ang = posf * inv[None, :]
        return jnp.cos(ang), jnp.sin(ang)

    one, zero = (lambda w: jnp.ones((n, w), jnp.float32)), (lambda w: jnp.zeros((n, w), jnp.float32))
    c, s = cs(ROPE_DIM)
    cq = jnp.concatenate([one(NOPE_DIM), c, c, one(LANES - NOPE_DIM - ROPE_DIM)], axis=1)
    sq = jnp.concatenate([zero(NOPE_DIM), -s, s, zero(LANES - NOPE_DIM - ROPE_DIM)], axis=1)
    c, s = cs(ROT_DIM_B)
    c64 = jnp.tile(jnp.concatenate([c, c, one(HEAD_DIM_B - ROT_DIM_B)], axis=1), (1, LANES // HEAD_DIM_B))
    s64 = jnp.tile(jnp.concatenate([-s, s, zero(HEAD_DIM_B - ROT_DIM_B)], axis=1), (1, LANES // HEAD_DIM_B))
    return cq, sq, c64, s64


def _pick_tile(n, pref):
    t = min(pref, n)
    while n % t:
        t //= 2
    return t


def _round_up(n, m):
    return -(-n // m) * m


def _layer(x, pos0, past, weights, norms, final_norm):
    b, l, _ = x.shape
    g_attn, g_q, g_kv, g_ffn = norms
    w_in_p, w_qb_p, wk, wvt, woa, wob, wg, wu, wd = weights
    t_tok = b * l
    x2d = x.reshape(t_tok, D_MODEL)

    tabs = _rope_tables(pos0 + jnp.arange(l))
    if l >= TM // 2:
        tm = _pick_tile(l, TM)
        n_tab_tiles = l // tm
    else:
        tm = _pick_tile(t_tok, TM)
        tabs = tuple(jnp.tile(t, (tm // l, 1)) for t in tabs)
        n_tab_tiles = 1
    (ckv, kr128, kb, vb, ik128, iw, q_mla, q_b, iq, kb16, vb16, ik16) = _proj_call(
        x2d, tabs, n_tab_tiles, tm, g_attn, w_in_p, g_q, w_qb_p, g_kv)

    new = (ckv.reshape(b, l, KV_LORA), kr128.reshape(b, l, LANES)[..., NOPE_DIM:NOPE_DIM + ROPE_DIM],
           kb.reshape(b, l, N_KV_B, HEAD_DIM_B), vb.reshape(b, l, N_KV_B, HEAD_DIM_B),
           ik128.reshape(b, l, LANES)[..., :IDX_DIM])

    per_row = lambda a: a.reshape(b, l, a.shape[-1])
    p_len = 0 if past is None else past[0].shape[1]
    s_len = p_len + l
    lq = _round_up(l, LANES)
    tq_mla, tq_dsa = _pick_tile(lq, TQ_MLA), _pick_tile(lq, TQ_DSA)
    tk_mla, tk_dsa = (2 * TK_MLA, 2 * TK_DSA) if lq < TQ_DSA else (TK_MLA, TK_DSA)
    s_pad = _round_up(s_len, max(tk_mla, tk_dsa))
    if past is None:
        old = [None] * 5
    else:
        p_lat, p_kr, p_k, p_v, p_ik = past
        old = [p_lat,
               jnp.pad(p_kr, ((0, 0), (0, 0), (NOPE_DIM, LANES - NOPE_DIM - ROPE_DIM))),
               p_k.reshape(b, p_len, LANES).astype(MXU_DTYPE), p_v.reshape(b, p_len, LANES).astype(MXU_DTYPE),
               jnp.pad(p_ik, ((0, 0), (0, 0), (0, LANES - IDX_DIM))).astype(MXU_DTYPE)]

    def full_seq(prev, cur):
        parts = ([] if prev is None else [prev]) + [per_row(cur)]
        if s_pad > s_len:
            parts.append(jnp.zeros((b, s_pad - s_len, cur.shape[-1]), cur.dtype))
        return parts[0] if len(parts) == 1 else jnp.concatenate(parts, axis=1)

    c_all, kr_all, kb_all, vb_all, ik_all = (full_seq(o, c) for o, c in zip(old, (ckv, kr128, kb16, vb16, ik16)))
    vbt_all = vb_all.reshape(b, s_pad // tk_dsa, tk_dsa, LANES).transpose(0, 1, 3, 2)

    k_mla, vt_mla = _kvup_call(c_all, kr_all, wk, wvt, tk_mla)

    padq = lambda a: per_row(a) if lq == l else jnp.pad(per_row(a), ((0, 0), (0, lq - l), (0, 0)))
    o_a = _mla_call(padq(q_mla), k_mla, vt_mla, tq=tq_mla, tk=tk_mla, q_off=pos0, s_len=s_len)
    tri = jnp.tril(jnp.ones((tk_dsa, tk_dsa), MXU_DTYPE))
    o_b = _dsa_call(padq(iq), padq(q_b), padq(iw), ik_all, kb_all, vbt_all, tri,
                    tq=tq_dsa, tk=tk_dsa, q_off=pos0, s_len=s_len)
    o_a, o_b = (o[:, :l].reshape(t_tok, -1) for o in (o_a, o_b))

    y = _out_call(x2d, o_a, o_b, woa, wob, g_ffn, wg, wu, wd, final_norm, tm)
    return y.reshape(b, l, D_MODEL), new


def kernel(x_prompt, x_sample, cache_mla_latent, cache_mla_krope, cache_dsa_k, cache_dsa_v, cache_idx_k, attn_norm, w_in, q_a_norm, w_q_b, kv_a_norm, w_kv_b, w_o, ffn_norm, w_gate, w_up, w_down, final_norm):
    depth = w_in.shape[0]
    assert depth == 1, "the fused final norm assumes a single layer"
    past_len = cache_mla_latent.shape[2]
    li = 0
    weights = _prep_weights(w_in[li], w_q_b[li], w_kv_b[li], w_o[li], w_gate[li], w_up[li], w_down[li])
    norms = (attn_norm[li][None, :], q_a_norm[li][None, :], kv_a_norm[li][None, :], ffn_norm[li][None, :])
    fin = final_norm[None, :]
    past = (cache_mla_latent[li], cache_mla_krope[li], cache_dsa_k[li], cache_dsa_v[li], cache_idx_k[li])
    y_p, new_p = _layer(x_prompt, 0, None, weights, norms, fin)
    y_s, new_s = _layer(x_sample, past_len, past, weights, norms, fin)
    return (y_p, y_s) + tuple(a[None] for a in new_p) + tuple(a[None] for a in new_s)
```

```python
import functools

import numpy as np
import jax
import jax.numpy as jnp
from jax import lax
from jax.experimental import pallas as pl
from jax.experimental.pallas import tpu as pltpu

D_MODEL = 1024
CHUNK = 64
ROPE_THETA = 500000.0
EPS = 1e-6
N_HEADS_A = 8
NOPE_DIM = 64
ROPE_DIM = 32
V_DIM_A = 64
Q_LORA = 384
KV_LORA = 256
N_HEADS_B = 8
N_KV_B = 2
HEAD_DIM_B = 64
ROT_DIM_B = 16
N_IDX_HEADS = 8
IDX_DIM = 64
ROT_DIM_IDX = 16
TOPK_MAX = 256
D_FF = 2816
IN_SIZES = (Q_LORA, KV_LORA, ROPE_DIM, N_HEADS_B * HEAD_DIM_B, N_KV_B * HEAD_DIM_B, N_KV_B * HEAD_DIM_B,
            N_IDX_HEADS * IDX_DIM, IDX_DIM, N_IDX_HEADS)

LANES = 128
SUBLANES = 8
MXU_DTYPE = jnp.bfloat16
VMEM_LIMIT = 60 * 1024 * 1024
NEG_BIG = -1e30
LOWEST = -3.0e38
FLT_MIN = 1.1754943508222875e-38
LOG2E = 1.4426950408889634

TM = 512
TQ_MLA = 512
TK_MLA = 512
TQ_DSA = 256
TK_DSA = 512
MLA_HEADS_PER_STEP = 4

_C_QLAT = 0
_C_KVLAT = _C_QLAT + Q_LORA
_C_QB = _C_KVLAT + KV_LORA
_C_IQ = _C_QB + N_HEADS_B * HEAD_DIM_B
_C_KB = _C_IQ + N_IDX_HEADS * IDX_DIM
_C_VB = _C_KB + LANES
_C_IK = _C_VB + LANES
_C_KR = _C_IK + LANES
_C_IW = _C_KR + LANES
_C_END = _C_IW + LANES


def _cparams(n_axes):
    return pltpu.CompilerParams(dimension_semantics=("arbitrary",) * n_axes, vmem_limit_bytes=VMEM_LIMIT)


def _const_spec(shape):
    return pl.BlockSpec(shape, lambda *_: (0,) * len(shape), pipeline_mode=pl.Buffered(1))


def _rms(x, g):
    return (x * lax.rsqrt(jnp.mean(x * x, axis=-1, keepdims=True) + EPS)) * g


def _rope_tile(x, cos, sin, half, first_half):
    partner = jnp.where(first_half, pltpu.roll(x, LANES - half, 1), pltpu.roll(x, half, 1))
    return x * cos + partner * sin


def _proj_kernel(x_ref, cq_ref, sq_ref, c64_ref, s64_ref, gat_ref, win_ref, gq_ref, wqb_ref, gkv_ref,
                 ckv_ref, kr_ref, kb_ref, vb_ref, ik_ref, iw_ref, qm_ref, qb_ref, iq_ref,
                 kb16_ref, vb16_ref, ik16_ref):
    x = x_ref[...]
    hb = _rms(x, gat_ref[...]).astype(MXU_DTYPE)
    y = jnp.dot(hb, win_ref[...], preferred_element_type=jnp.float32)

    lane = lax.broadcasted_iota(jnp.int32, (1, LANES), 1)
    first_q = lane < NOPE_DIM + ROPE_DIM // 2
    first_64 = (lane & (HEAD_DIM_B - 1)) < ROT_DIM_B // 2
    cq, sq, c64, s64 = cq_ref[...], sq_ref[...], c64_ref[...], s64_ref[...]

    qn = _rms(y[:, _C_QLAT:_C_QLAT + Q_LORA], gq_ref[...]).astype(MXU_DTYPE)
    qa = jnp.dot(qn, wqb_ref[...], preferred_element_type=jnp.float32)
    scale_a = (NOPE_DIM + ROPE_DIM) ** -0.5 * LOG2E
    for h in range(N_HEADS_A):
        t = _rope_tile(qa[:, h * LANES:(h + 1) * LANES], cq, sq, ROPE_DIM // 2, first_q)
        qm_ref[:, h * LANES:(h + 1) * LANES] = (t * scale_a).astype(qm_ref.dtype)

    ckv_ref[...] = _rms(y[:, _C_KVLAT:_C_KVLAT + KV_LORA], gkv_ref[...])
    kr_ref[...] = _rope_tile(y[:, _C_KR:_C_KR + LANES], cq, sq, ROPE_DIM // 2, first_q)

    scale_b = HEAD_DIM_B ** -0.5 * LOG2E
    low = lane < HEAD_DIM_B
    heads_per_kv = N_HEADS_B // N_KV_B
    for t in range(N_HEADS_B // 2):
        pair = _rope_tile(y[:, _C_QB + t * LANES:_C_QB + (t + 1) * LANES], c64, s64, ROT_DIM_B // 2, first_64)
        pair = pair * scale_b
        swapped = pltpu.roll(pair, HEAD_DIM_B, 1)
        for u in range(2):
            h = 2 * t + u
            half = h // heads_per_kv
            src = pair if u == half else swapped
            tile = jnp.where(low if half == 0 else jnp.logical_not(low), src, 0.0)
            qb_ref[:, h * LANES:(h + 1) * LANES] = tile.astype(qb_ref.dtype)
    for t in range(N_IDX_HEADS // 2):
        pair = _rope_tile(y[:, _C_IQ + t * LANES:_C_IQ + (t + 1) * LANES], c64, s64, ROT_DIM_IDX // 2, first_64)
        swapped = pltpu.roll(pair, IDX_DIM, 1)
        for u in range(2):
            h = 2 * t + u
            iq_ref[:, h * LANES:(h + 1) * LANES] = jnp.where(low, pair if u == 0 else swapped,
                                                             0.0).astype(iq_ref.dtype)

    kb = _rope_tile(y[:, _C_KB:_C_KB + LANES], c64, s64, ROT_DIM_B // 2, first_64)
    kb_ref[...] = kb
    kb16_ref[...] = kb.astype(kb16_ref.dtype)
    vb = y[:, _C_VB:_C_VB + LANES]
    vb_ref[...] = vb
    vb16_ref[...] = vb.astype(vb16_ref.dtype)
    ik = _rope_tile(y[:, _C_IK:_C_IK + LANES], c64, s64, ROT_DIM_IDX // 2, first_64)
    ik_ref[...] = ik
    ik16_ref[...] = ik.astype(ik16_ref.dtype)
    iw_ref[...] = y[:, _C_IW:_C_IW + LANES] * ((N_IDX_HEADS * IDX_DIM) ** -0.5)


def _proj_call(x2d, tabs, n_tab_tiles, tm, g_attn, w_in_p, g_q, w_qb_p, g_kv):
    t_tok = x2d.shape[0]
    grid = (t_tok // tm,)
    row = lambda w: pl.BlockSpec((tm, w), lambda i: (i, 0))
    tab = pl.BlockSpec((tm, LANES), lambda i: (i % n_tab_tiles, 0))
    f32, b16 = jnp.float32, MXU_DTYPE
    out_shapes = [
        jax.ShapeDtypeStruct((t_tok, KV_LORA), f32),
        jax.ShapeDtypeStruct((t_tok, LANES), f32),
        jax.ShapeDtypeStruct((t_tok, LANES), f32),
        jax.ShapeDtypeStruct((t_tok, LANES), f32),
        jax.ShapeDtypeStruct((t_tok, LANES), f32),
        jax.ShapeDtypeStruct((t_tok, LANES), f32),
        jax.ShapeDtypeStruct((t_tok, N_HEADS_A * LANES), b16),
        jax.ShapeDtypeStruct((t_tok, N_HEADS_B * LANES), b16),
        jax.ShapeDtypeStruct((t_tok, N_IDX_HEADS * LANES), b16),
        jax.ShapeDtypeStruct((t_tok, LANES), b16),
        jax.ShapeDtypeStruct((t_tok, LANES), b16),
        jax.ShapeDtypeStruct((t_tok, LANES), b16),
    ]
    out_specs = [row(s.shape[1]) for s in out_shapes]
    in_specs = [row(D_MODEL), tab, tab, tab, tab, _const_spec((1, D_MODEL)), _const_spec(w_in_p.shape),
                _const_spec((1, Q_LORA)), _const_spec(w_qb_p.shape), _const_spec((1, KV_LORA))]
    return pl.pallas_call(
        _proj_kernel, grid=grid, in_specs=in_specs, out_specs=out_specs, out_shape=out_shapes,
        compiler_params=_cparams(1), name="proj",
    )(x2d, *tabs, g_attn, w_in_p, g_q, w_qb_p, g_kv)


_NT = (((1,), (1,)), ((), ()))


def _kvup_kernel(c_ref, kr_ref, wk_ref, wvt_ref, k_ref, vt_ref):
    cb = c_ref[...].astype(MXU_DTYPE)
    kn = jnp.dot(cb, wk_ref[...], preferred_element_type=jnp.float32)
    kr = kr_ref[...]
    for h in range(N_HEADS_A):
        k_ref[:, h * LANES:(h + 1) * LANES] = (kn[:, h * LANES:(h + 1) * LANES] + kr).astype(k_ref.dtype)
    vt_ref[...] = lax.dot_general(wvt_ref[...], cb, _NT, preferred_element_type=jnp.float32).astype(vt_ref.dtype)


def _kvup_call(c3d, kr3d, wk, wvt, tk):
    b, s_pad, _ = c3d.shape
    row = lambda w: pl.BlockSpec((None, tk, w), lambda bi, i: (bi, i, 0))
    vdim = N_HEADS_A * V_DIM_A
    return pl.pallas_call(
        _kvup_kernel, grid=(b, s_pad // tk),
        in_specs=[row(KV_LORA), row(LANES), _const_spec(wk.shape), _const_spec(wvt.shape)],
        out_specs=[row(N_HEADS_A * LANES), pl.BlockSpec((None, None, vdim, tk), lambda bi, i: (bi, i, 0, 0))],
        out_shape=[jax.ShapeDtypeStruct((b, s_pad, N_HEADS_A * LANES), MXU_DTYPE),
                   jax.ShapeDtypeStruct((b, s_pad // tk, vdim, tk), MXU_DTYPE)],
        compiler_params=_cparams(2), name="kvup",
    )(c3d, kr3d, wk, wvt)


_CHUNK_SHIFT = CHUNK.bit_length() - 1


def _admissible_t(p0, kb, tq, tk, s_len):
    kpos = kb * tk + lax.broadcasted_iota(jnp.int32, (tk, 1), 0)
    qpos = p0 + lax.broadcasted_iota(jnp.int32, (1, tq), 1)
    return ((qpos >> _CHUNK_SHIFT) >= (kpos >> _CHUNK_SHIFT)) & (kpos < s_len)


def _check_tiles(tq, tk, q_off, lq):
    assert tk & (tk - 1) == 0 and tk % tq == 0 and tk % CHUNK == 0, (tq, tk)
    assert q_off % tq == 0 or (lq == tq and q_off % tk + tq <= tk), (tq, tk, q_off, lq)


def _last_block(p0, tq, tk, s_len):
    return jnp.minimum(p0 + tq - 1, s_len - 1) >> (tk.bit_length() - 1)


def _fori_grouped(n, body, carry, group):
    shift = group.bit_length() - 1

    def several(j, c):
        for i in range(group):
            c = body(group * j + i, c)
        return c

    carry = lax.fori_loop(0, n >> shift, several, carry)
    return lax.fori_loop((n >> shift) << shift, n, body, carry)


def _softmax_init(n_heads, tq, acc_ref):
    acc_ref[...] = jnp.zeros(acc_ref.shape, jnp.float32)
    return jnp.full((n_heads, tq), NEG_BIG, jnp.float32), jnp.zeros((n_heads, tq), jnp.float32)


def _attn_step(state, values, next_scores, bias, s_ref, acc_ref):
    (m_prev, l_prev), maxima = state
    n_heads = s_ref.shape[0]
    sums, next_max = [], []
    if values is not None:
        m_new = jnp.maximum(m_prev, maxima)
        alpha = jnp.exp2(m_prev - m_new)
    for h in range(n_heads):
        if values is not None:
            p = jnp.exp2(s_ref[h] - m_new[h:h + 1, :])
            sums.append(jnp.sum(p, axis=0, keepdims=True))
            acc_ref[h] = alpha[h:h + 1, :] * acc_ref[h] + jnp.dot(values[h], p.astype(values[h].dtype),
                                                                   preferred_element_type=jnp.float32)
        if next_scores is not None:
            s = next_scores(h)
            if bias is not None:
                s = s + bias
            s_ref[h] = s
            next_max.append(jnp.max(s, axis=0, keepdims=True))
    carry = (m_new, alpha * l_prev + jnp.concatenate(sums, axis=0)) if values is not None else (m_prev, l_prev)
    return carry, (jnp.concatenate(next_max, axis=0) if next_scores is not None else maxima)


def _softmax_finish(l, acc_ref, o_ref):
    inv = 1.0 / l
    o_t = jnp.concatenate([acc_ref[h] * inv[h:h + 1, :] for h in range(l.shape[0])], axis=0)
    o_ref[...] = o_t.T.astype(o_ref.dtype)


def _mla_kernel(q_ref, k_ref, vt_ref, o_ref, s_ref, acc_ref, *, tq, tk, q_off, s_len):
    p0 = q_off + pl.program_id(2) * tq
    last = _last_block(p0, tq, tk, s_len)
    heads = range(MLA_HEADS_PER_STEP)

    def qk(kb):
        start = pl.multiple_of(kb * tk, tk)
        return lambda h: lax.dot_general(k_ref[pl.ds(start, tk), h * LANES:(h + 1) * LANES],
                                         q_ref[:, h * LANES:(h + 1) * LANES], _NT,
                                         preferred_element_type=jnp.float32)

    def mask_bias(kb):
        return jnp.where(_admissible_t(p0, kb, tq, tk, s_len), 0.0, NEG_BIG)

    def values(kb):
        vt_all = vt_ref[kb]
        return [vt_all[h * V_DIM_A:(h + 1) * V_DIM_A, :] for h in heads]

    def body(kb, state):
        return _attn_step(state, values(kb), qk(kb + 1), None, s_ref, acc_ref)

    def tail(state):
        return _attn_step(state, values(last - 1), qk(last), mask_bias(last), s_ref, acc_ref)

    state = _attn_step((_softmax_init(MLA_HEADS_PER_STEP, tq, acc_ref), None), None, qk(0), mask_bias(0), s_ref,
                       acc_ref)
    state = lax.fori_loop(0, jnp.maximum(last - 1, 0), body, state)
    state = lax.cond(last > 0, tail, lambda st: st, state)
    (_, l), _ = _attn_step(state, values(last), None, None, s_ref, acc_ref)
    _softmax_finish(l, acc_ref, o_ref)


def _mla_call(q, k, vt, *, tq, tk, q_off, s_len):
    b, lq, _ = q.shape
    _check_tiles(tq, tk, q_off, lq)
    s_pad = k.shape[1]
    hg = MLA_HEADS_PER_STEP
    kern = functools.partial(_mla_kernel, tq=tq, tk=tk, q_off=q_off, s_len=s_len)
    return pl.pallas_call(
        kern, grid=(b, N_HEADS_A // hg, lq // tq),
        in_specs=[pl.BlockSpec((None, tq, hg * LANES), lambda bi, g, qi: (bi, qi, g)),
                  pl.BlockSpec((None, s_pad, hg * LANES), lambda bi, g, qi: (bi, 0, g)),
                  pl.BlockSpec((None, s_pad // tk, hg * V_DIM_A, tk), lambda bi, g, qi: (bi, 0, g, 0))],
        out_specs=pl.BlockSpec((None, tq, hg * V_DIM_A), lambda bi, g, qi: (bi, qi, g)),
        out_shape=jax.ShapeDtypeStruct((b, lq, N_HEADS_A * V_DIM_A), MXU_DTYPE),
        scratch_shapes=[pltpu.VMEM((hg, tk, tq), jnp.float32), pltpu.VMEM((hg, V_DIM_A, tq), jnp.float32)],
        compiler_params=_cparams(3), name="mla",
    )(q, k, vt)


def _f2key(x):
    b = lax.bitcast_convert_type(x, jnp.int32)
    return jnp.where(b < 0, b ^ jnp.int32(0x7FFFFFFF), b)


def _key2f(k):
    return lax.bitcast_convert_type(jnp.where(k < 0, k ^ jnp.int32(0x7FFFFFFF), k), jnp.float32)


_COUNT_SLAB = 4 * SUBLANES
_SCORE_GROUP = 4
_COUNT_GROUP = 4
_SEARCH_BISECT_EVERY = 8
_SEARCH_MAX_PROBES = 40 * _SEARCH_BISECT_EVERY


def _dsa_kernel(iq_ref, qb_ref, iw_ref, ik_ref, kb_ref, vbt_ref, tri_ref, o_ref,
                sc_ref, wt_ref, s_ref, acc_ref, *, tq, tk, q_off, s_len, topk):
    p0 = q_off + pl.program_id(1) * tq
    last = _last_block(p0, tq, tk, s_len)
    n_blk = last + 1
    row = lambda v: jnp.full((1, tq), v, jnp.float32)
    colsum = lambda x: jnp.sum(x, axis=0, keepdims=True)

    wt_ref[...] = iw_ref[...].T[:SUBLANES, :]

    def score_block(kb):
        start = pl.multiple_of(kb * tk, tk)
        ikb = ik_ref[pl.ds(start, tk), :]
        acc = jnp.zeros((tk, tq), jnp.float32)
        for h in range(N_IDX_HEADS):
            x = lax.dot_general(ikb, iq_ref[:, h * LANES:(h + 1) * LANES], _NT, preferred_element_type=jnp.float32)
            acc = acc + wt_ref[h:h + 1, :] * jnp.maximum(x, 0.0)
        return acc

    def stats(carry, hi_fill, lo_fill):
        mx, mn, n_pos, n_nonneg = carry
        return (jnp.maximum(mx, jnp.max(hi_fill, axis=0, keepdims=True)),
                jnp.minimum(mn, jnp.min(lo_fill, axis=0, keepdims=True)),
                n_pos + colsum(jnp.where(hi_fill >= FLT_MIN, 1.0, 0.0)),
                n_nonneg + colsum(jnp.where(hi_fill >= 0.0, 1.0, 0.0)))

    def p1_body(kb, carry):
        acc = score_block(kb)
        sc_ref[kb] = acc
        return stats(carry, acc, acc)

    carry = _fori_grouped(last, p1_body, (row(-jnp.inf), row(jnp.inf), row(0.0), row(0.0)), _SCORE_GROUP)
    acc = score_block(last)
    adm = _admissible_t(p0, last, tq, tk, s_len)
    masked = jnp.where(adm, acc, -jnp.inf)
    sc_ref[last] = masked
    mx, mn, n_pos, n_nonneg = stats(carry, masked, jnp.where(adm, acc, jnp.inf))

    qpos = p0 + lax.broadcasted_iota(jnp.int32, (1, tq), 1)
    n_adm = jnp.minimum(((qpos >> _CHUNK_SHIFT) + 1) * CHUNK, s_len).astype(jnp.float32)
    k_f = jnp.float32(topk)
    log_k = jnp.log(k_f)
    phi = lambda c: jnp.log(jnp.maximum(c, 0.25)) - log_k

    few = n_adm <= k_f
    pos = n_pos >= k_f
    at_zero = jnp.logical_and(~pos, n_nonneg >= k_f)
    above_max = _key2f(_f2key(jnp.maximum(mx, LOWEST)) + 1)
    lo = jnp.where(pos, FLT_MIN, LOWEST)
    c_lo = jnp.where(pos, n_pos, n_adm)
    hi = jnp.where(pos | at_zero, above_max, 0.0)
    c_hi = jnp.where(pos | at_zero, 0.0, n_nonneg)
    lo_val = jnp.where(pos, 0.0, jnp.maximum(mn, LOWEST))
    tau = jnp.where(few, LOWEST, jnp.where(at_zero, 0.0, lo))
    r = jnp.where(at_zero & ~few, k_f - n_pos, 0.0)
    done = jnp.where(few | at_zero, 1.0, 0.0)

    def count_ge(g):
        def body(kb, cnt):
            hit = jnp.where(sc_ref[kb] >= g, 1.0, 0.0)
            return cnt + jnp.sum(hit.reshape(tk // _COUNT_SLAB, _COUNT_SLAB, tq), axis=0)

        return colsum(_fori_grouped(n_blk, body, jnp.zeros((_COUNT_SLAB, tq), jnp.float32), _COUNT_GROUP))

    def search_cond(st):
        return jnp.logical_and(st[1] > 0, st[0] < _SEARCH_MAX_PROBES)

    def next_probe(it, lo, hi, f_lo, f_hi, lo_val):
        lok, hik = _f2key(lo), _f2key(hi)
        midk = (lok >> 1) + (hik >> 1) + (lok & hik & 1)
        frac = f_lo / jnp.maximum(f_lo - f_hi, 1e-9)
        g_int = lo_val + (hi - lo_val) * frac
        bisect = (it % _SEARCH_BISECT_EVERY) == (_SEARCH_BISECT_EVERY - 1)
        g = jnp.where(bisect, _key2f(midk), g_int)
        gk = jnp.minimum(jnp.maximum(_f2key(g), lok + 1), jnp.maximum(hik - 1, lok + 1))
        return _key2f(gk)

    def search_body(st):
        it, _, g, lo, hi, c_lo, c_hi, f_lo, f_hi, lo_val, side, tau, r, done = st
        c = count_ge(g)
        f_g = phi(c)
        live = done == 0.0
        up = live & (c >= k_f)
        dn = live & (c < k_f)
        f_hi = jnp.where(up & (side > 0.0), f_hi * 0.5, f_hi)
        f_lo = jnp.where(dn & (side < 0.0), f_lo * 0.5, f_lo)
        lo, c_lo, f_lo, lo_val = (jnp.where(up, g, lo), jnp.where(up, c, c_lo), jnp.where(up, f_g, f_lo),
                                  jnp.where(up, g, lo_val))
        hi, c_hi, f_hi = jnp.where(dn, g, hi), jnp.where(dn, c, c_hi), jnp.where(dn, f_g, f_hi)
        side = jnp.where(up, 1.0, jnp.where(dn, -1.0, side))
        adjacent = (_f2key(lo) + 1) >= _f2key(hi)
        fin = live & ((c_lo == k_f) | adjacent)
        tau = jnp.where(fin, lo, tau)
        r = jnp.where(fin & (c_lo > k_f), k_f - c_hi, r)
        done = jnp.where(fin, 1.0, done)
        pending = jnp.sum(1.0 - done).astype(jnp.int32)
        g = next_probe(it + 1, lo, hi, f_lo, f_hi, lo_val)
        return (it + 1, pending, g, lo, hi, c_lo, c_hi, f_lo, f_hi, lo_val, side, tau, r, done)

    pending0 = jnp.sum(1.0 - done).astype(jnp.int32)
    f_lo0, f_hi0 = phi(c_lo), phi(c_hi)
    st = lax.while_loop(search_cond, search_body,
                        (jnp.int32(0), pending0, next_probe(0, lo, hi, f_lo0, f_hi0, lo_val), lo, hi, c_lo, c_hi,
                         f_lo0, f_hi0, lo_val, row(0.0), tau, r, done))
    tau, r = st[11], st[12]

    @pl.when(jnp.sum(r) > 0.0)
    def _():
        tied = r > 0.0
        budget = jnp.where(tied, r, jnp.inf)

        def body(kb, seen):
            blk = sc_ref[kb]
            eq = jnp.logical_and(blk == tau, tied)
            eq_f = jnp.where(eq, 1.0, 0.0)
            rank = jnp.dot(tri_ref[...], eq_f.astype(MXU_DTYPE), preferred_element_type=jnp.float32) + seen
            sc_ref[kb] = jnp.where(jnp.logical_and(eq, rank > budget), -jnp.inf, blk)
            return seen + colsum(eq_f)

        _fori_grouped(n_blk, body, row(0.0), _COUNT_GROUP)

    heads_per_kv = N_HEADS_B // N_KV_B

    def qk(kb):
        start = pl.multiple_of(kb * tk, tk)
        kblk = kb_ref[pl.ds(start, tk), :]
        return lambda h: lax.dot_general(kblk, qb_ref[:, h * LANES:(h + 1) * LANES], _NT,
                                         preferred_element_type=jnp.float32)

    def sel_bias(kb):
        return jnp.where(sc_ref[kb] >= tau, 0.0, NEG_BIG)

    def values(kb):
        vt_all = vbt_ref[kb]
        return [vt_all[(h // heads_per_kv) * HEAD_DIM_B:(h // heads_per_kv + 1) * HEAD_DIM_B, :]
                for h in range(N_HEADS_B)]

    def p3_body(kb, state):
        return _attn_step(state, values(kb), qk(kb + 1), sel_bias(kb + 1), s_ref, acc_ref)

    state = _attn_step((_softmax_init(N_HEADS_B, tq, acc_ref), None), None, qk(0), sel_bias(0), s_ref, acc_ref)
    state = lax.fori_loop(0, last, p3_body, state)
    (_, l), _ = _attn_step(state, values(last), None, None, s_ref, acc_ref)
    _softmax_finish(l, acc_ref, o_ref)


def _dsa_call(iq, qb, iw, ik, kb, vbt, tri, *, tq, tk, q_off, s_len):
    b, lq, _ = iq.shape
    _check_tiles(tq, tk, q_off, lq)
    s_pad = ik.shape[1]
    n_kb = s_pad // tk
    topk = min(TOPK_MAX, s_len // 4)
    kern = functools.partial(_dsa_kernel, tq=tq, tk=tk, q_off=q_off, s_len=s_len, topk=topk)
    qspec = lambda w: pl.BlockSpec((None, tq, w), lambda bi, qi: (bi, qi, 0))
    kspec = pl.BlockSpec((None, s_pad, LANES), lambda bi, qi: (bi, 0, 0))
    return pl.pallas_call(
        kern, grid=(b, lq // tq),
        in_specs=[qspec(N_IDX_HEADS * LANES), qspec(N_HEADS_B * LANES), qspec(LANES), kspec, kspec,
                  pl.BlockSpec((None, n_kb, LANES, tk), lambda bi, qi: (bi, 0, 0, 0)),
                  _const_spec(tri.shape)],
        out_specs=qspec(N_HEADS_B * HEAD_DIM_B),
        out_shape=jax.ShapeDtypeStruct((b, lq, N_HEADS_B * HEAD_DIM_B), MXU_DTYPE),
        scratch_shapes=[pltpu.VMEM((n_kb, tk, tq), jnp.float32), pltpu.VMEM((SUBLANES, tq), jnp.float32),
                        pltpu.VMEM((N_HEADS_B, tk, tq), jnp.float32),
                        pltpu.VMEM((N_HEADS_B, HEAD_DIM_B, tq), jnp.float32)],
        compiler_params=_cparams(2), name="dsa",
    )(iq, qb, iw, ik, kb, vbt, tri)


_FF_CHUNK = 256


def _out_kernel(x_ref, oa_ref, ob_ref, woa_ref, wob_ref, gf_ref, wg_ref, wu_ref, wd_ref, gfin_ref, y_ref):
    o = (jnp.dot(oa_ref[...], woa_ref[...], preferred_element_type=jnp.float32)
         + jnp.dot(ob_ref[...], wob_ref[...], preferred_element_type=jnp.float32))
    x1 = x_ref[...] + o
    h2 = _rms(x1, gf_ref[...]).astype(MXU_DTYPE)
    ffn = jnp.zeros_like(x1)
    for j in range(D_FF // _FF_CHUNK):
        sl = slice(j * _FF_CHUNK, (j + 1) * _FF_CHUNK)
        gate = jnp.dot(h2, wg_ref[:, sl], preferred_element_type=jnp.float32)
        up = jnp.dot(h2, wu_ref[:, sl], preferred_element_type=jnp.float32)
        act = (gate * jax.nn.sigmoid(gate)) * up
        ffn = ffn + jnp.dot(act.astype(MXU_DTYPE), wd_ref[sl, :], preferred_element_type=jnp.float32)
    y_ref[...] = _rms(x1 + ffn, gfin_ref[...])


def _out_call(x2d, oa, ob, woa, wob, g_ffn, wg, wu, wd, g_fin, tm):
    t_tok = x2d.shape[0]
    row = lambda w: pl.BlockSpec((tm, w), lambda i: (i, 0))
    half = N_HEADS_A * V_DIM_A
    return pl.pallas_call(
        _out_kernel, grid=(t_tok // tm,),
        in_specs=[row(D_MODEL), row(half), row(half), _const_spec(woa.shape), _const_spec(wob.shape),
                  _const_spec((1, D_MODEL)), _const_spec(wg.shape), _const_spec(wu.shape), _const_spec(wd.shape),
                  _const_spec((1, D_MODEL))],
        out_specs=row(D_MODEL), out_shape=jax.ShapeDtypeStruct((t_tok, D_MODEL), jnp.float32),
        compiler_params=_cparams(1), name="out",
    )(x2d, oa, ob, woa, wob, g_ffn, wg, wu, wd, g_fin)


def _prep_weights(w_in, w_q_b, w_kv_b, w_o, w_gate, w_up, w_down):
    off = np.concatenate([[0], np.cumsum(IN_SIZES)])
    o_qlat, o_kvlat, o_kr, o_qb, o_kb, o_vb, o_iq, o_ik, o_iw = off[:9]
    zeros = lambda n: jnp.zeros((D_MODEL, n), w_in.dtype)
    cols = [w_in[:, o_qlat:o_qlat + Q_LORA], w_in[:, o_kvlat:o_kvlat + KV_LORA],
            w_in[:, o_qb:o_qb + N_HEADS_B * HEAD_DIM_B], w_in[:, o_iq:o_iq + N_IDX_HEADS * IDX_DIM]]
    cols += [w_in[:, o_kb:o_kb + LANES], w_in[:, o_vb:o_vb + LANES],
             w_in[:, o_ik:o_ik + IDX_DIM], zeros(LANES - IDX_DIM),
             zeros(NOPE_DIM), w_in[:, o_kr:o_kr + ROPE_DIM], zeros(LANES - NOPE_DIM - ROPE_DIM),
             w_in[:, o_iw:o_iw + N_IDX_HEADS], zeros(LANES - N_IDX_HEADS)]
    w_in_p = jnp.concatenate(cols, axis=1).astype(MXU_DTYPE)
    assert w_in_p.shape[1] == _C_END

    qd = NOPE_DIM + ROPE_DIM
    w_qb_p = jnp.pad(w_q_b.reshape(Q_LORA, N_HEADS_A, qd), ((0, 0), (0, 0), (0, LANES - qd)))
    w_qb_p = w_qb_p.reshape(Q_LORA, N_HEADS_A * LANES).astype(MXU_DTYPE)

    kv = w_kv_b.reshape(KV_LORA, N_HEADS_A, NOPE_DIM + V_DIM_A)
    wk = jnp.pad(kv[:, :, :NOPE_DIM], ((0, 0), (0, 0), (0, LANES - NOPE_DIM)))
    wk = wk.reshape(KV_LORA, N_HEADS_A * LANES).astype(MXU_DTYPE)
    wvt = kv[:, :, NOPE_DIM:].reshape(KV_LORA, N_HEADS_A * V_DIM_A).T.astype(MXU_DTYPE)

    half = N_HEADS_A * V_DIM_A
    return (w_in_p, w_qb_p, wk, wvt, w_o[:half].astype(MXU_DTYPE), w_o[half:].astype(MXU_DTYPE),
            w_gate.astype(MXU_DTYPE), w_up.astype(MXU_DTYPE), w_down.astype(MXU_DTYPE))


def _rope_tables(pos):
    posf = pos.astype(jnp.float32)[:, None]
    n = pos.shape[0]

    def cs(d):
        inv = jnp.power(ROPE_THETA, -jnp.arange(0, d, 2, dtype=jnp.float32) / d)
        ang = posf * inv[None, :]
        return jnp.cos(ang), jnp.sin(ang)

    one, zero = (lambda w: jnp.ones((n, w), jnp.float32)), (lambda w: jnp.zeros((n, w), jnp.float32))
    c, s = cs(ROPE_DIM)
    cq = jnp.concatenate([one(NOPE_DIM), c, c, one(LANES - NOPE_DIM - ROPE_DIM)], axis=1)
    sq = jnp.concatenate([zero(NOPE_DIM), -s, s, zero(LANES - NOPE_DIM - ROPE_DIM)], axis=1)
    c, s = cs(ROT_DIM_B)
    c64 = jnp.tile(jnp.concatenate([c, c, one(HEAD_DIM_B - ROT_DIM_B)], axis=1), (1, LANES // HEAD_DIM_B))
    s64 = jnp.tile(jnp.concatenate([-s, s, zero(HEAD_DIM_B - ROT_DIM_B)], axis=1), (1, LANES // HEAD_DIM_B))
    return cq, sq, c64, s64


def _pick_tile(n, pref):
    t = min(pref, n)
    while n % t:
        t //= 2
    return t


def _round_up(n, m):
    return -(-n // m) * m


def _layer(x, pos0, past, weights, norms, final_norm):
    b, l, _ = x.shape
    g_attn, g_q, g_kv, g_ffn = norms
    w_in_p, w_qb_p, wk, wvt, woa, wob, wg, wu, wd = weights
    t_tok = b * l
    x2d = x.reshape(t_tok, D_MODEL)

    tabs = _rope_tables(pos0 + jnp.arange(l))
    if l >= TM // 2:
        tm = _pick_tile(l, TM)
        n_tab_tiles = l // tm
    else:
        tm = _pick_tile(t_tok, TM)
        tabs = tuple(jnp.tile(t, (tm // l, 1)) for t in tabs)
        n_tab_tiles = 1
    (ckv, kr128, kb, vb, ik128, iw, q_mla, q_b, iq, kb16, vb16, ik16) = _proj_call(
        x2d, tabs, n_tab_tiles, tm, g_attn, w_in_p, g_q, w_qb_p, g_kv)

    new = (ckv.reshape(b, l, KV_LORA), kr128.reshape(b, l, LANES)[..., NOPE_DIM:NOPE_DIM + ROPE_DIM],
           kb.reshape(b, l, N_KV_B, HEAD_DIM_B), vb.reshape(b, l, N_KV_B, HEAD_DIM_B),
           ik128.reshape(b, l, LANES)[..., :IDX_DIM])

    per_row = lambda a: a.reshape(b, l, a.shape[-1])
    p_len = 0 if past is None else past[0].shape[1]
    s_len = p_len + l
    lq = _round_up(l, LANES)
    tq_mla, tq_dsa = _pick_tile(lq, TQ_MLA), _pick_tile(lq, TQ_DSA)
    tk_mla, tk_dsa = (2 * TK_MLA, 2 * TK_DSA) if lq < TQ_DSA else (TK_MLA, TK_DSA)
    s_pad = _round_up(s_len, max(tk_mla, tk_dsa))
    if past is None:
        old = [None] * 5
    else:
        p_lat, p_kr, p_k, p_v, p_ik = past
        old = [p_lat,
               jnp.pad(p_kr, ((0, 0), (0, 0), (NOPE_DIM, LANES - NOPE_DIM - ROPE_DIM))),
               p_k.reshape(b, p_len, LANES).astype(MXU_DTYPE), p_v.reshape(b, p_len, LANES).astype(MXU_DTYPE),
               jnp.pad(p_ik, ((0, 0), (0, 0), (0, LANES - IDX_DIM))).astype(MXU_DTYPE)]

    def full_seq(prev, cur):
        parts = ([] if prev is None else [prev]) + [per_row(cur)]
        if s_pad > s_len:
            parts.append(jnp.zeros((b, s_pad - s_len, cur.shape[-1]), cur.dtype))
        return parts[0] if len(parts) == 1 else jnp.concatenate(parts, axis=1)

    c_all, kr_all, kb_all, vb_all, ik_all = (full_seq(o, c) for o, c in zip(old, (ckv, kr128, kb16, vb16, ik16)))
    vbt_all = vb_all.reshape(b, s_pad // tk_dsa, tk_dsa, LANES).transpose(0, 1, 3, 2)

    k_mla, vt_mla = _kvup_call(c_all, kr_all, wk, wvt, tk_mla)

    padq = lambda a: per_row(a) if lq == l else jnp.pad(per_row(a), ((0, 0), (0, lq - l), (0, 0)))
    o_a = _mla_call(padq(q_mla), k_mla, vt_mla, tq=tq_mla, tk=tk_mla, q_off=pos0, s_len=s_len)
    tri = jnp.tril(jnp.ones((tk_dsa, tk_dsa), MXU_DTYPE))
    o_b = _dsa_call(padq(iq), padq(q_b), padq(iw), ik_all, kb_all, vbt_all, tri,
                    tq=tq_dsa, tk=tk_dsa, q_off=pos0, s_len=s_len)
    o_a, o_b = (o[:, :l].reshape(t_tok, -1) for o in (o_a, o_b))

    y = _out_call(x2d, o_a, o_b, woa, wob, g_ffn, wg, wu, wd, final_norm, tm)
    return y.reshape(b, l, D_MODEL), new


def kernel(x_prompt, x_sample, cache_mla_latent, cache_mla_krope, cache_dsa_k, cache_dsa_v, cache_idx_k, attn_norm, w_in, q_a_norm, w_q_b, kv_a_norm, w_kv_b, w_o, ffn_norm, w_gate, w_up, w_down, final_norm):
    depth = w_in.shape[0]
    assert depth == 1, "the fused final norm assumes a single layer"
    past_len = cache_mla_latent.shape[2]
    li = 0
    weights = _prep_weights(w_in[li], w_q_b[li], w_kv_b[li], w_o[li], w_gate[li], w_up[li], w_down[li])
    norms = (attn_norm[li][None, :], q_a_norm[li][None, :], kv_a_norm[li][None, :], ffn_norm[li][None, :])
    fin = final_norm[None, :]
    past = (cache_mla_latent[li], cache_mla_krope[li], cache_dsa_k[li], cache_dsa_v[li], cache_idx_k[li])
    y_p, new_p = _layer(x_prompt, 0, None, weights, norms, fin)
    y_s, new_s = _layer(x_sample, past_len, past, weights, norms, fin)
    return (y_p, y_s) + tuple(a[None] for a in new_p) + tuple(a[None] for a in new_s)
```

```python
import functools

import numpy as np
import jax
import jax.numpy as jnp
from jax import lax
from jax.experimental import pallas as pl
from jax.experimental.pallas import tpu as pltpu

D_MODEL = 1024
CHUNK = 64
ROPE_THETA = 500000.0
EPS = 1e-6
N_HEADS_A = 8
NOPE_DIM = 64
ROPE_DIM = 32
V_DIM_A = 64
Q_LORA = 384
KV_LORA = 256
N_HEADS_B = 8
N_KV_B = 2
HEAD_DIM_B = 64
ROT_DIM_B = 16
N_IDX_HEADS = 8
IDX_DIM = 64
ROT_DIM_IDX = 16
TOPK_MAX = 256
D_FF = 2816
IN_SIZES = (Q_LORA, KV_LORA, ROPE_DIM, N_HEADS_B * HEAD_DIM_B, N_KV_B * HEAD_DIM_B, N_KV_B * HEAD_DIM_B,
            N_IDX_HEADS * IDX_DIM, IDX_DIM, N_IDX_HEADS)

LANES = 128
SUBLANES = 8
MXU_DTYPE = jnp.bfloat16
VMEM_LIMIT = 60 * 1024 * 1024
NEG_BIG = -1e30
LOWEST = -3.0e38
FLT_MIN = 1.1754943508222875e-38
LOG2E = 1.4426950408889634

TM = 512
TQ_MLA = 512
TK_MLA = 512
TQ_DSA = 256
TK_DSA = 512
MLA_HEADS_PER_STEP = 4

_C_QLAT = 0
_C_KVLAT = _C_QLAT + Q_LORA
_C_QB = _C_KVLAT + KV_LORA
_C_IQ = _C_QB + N_HEADS_B * HEAD_DIM_B
_C_KB = _C_IQ + N_IDX_HEADS * IDX_DIM
_C_VB = _C_KB + LANES
_C_IK = _C_VB + LANES
_C_KR = _C_IK + LANES
_C_IW = _C_KR + LANES
_C_END = _C_IW + LANES


def _cparams(n_axes):
    return pltpu.CompilerParams(dimension_semantics=("arbitrary",) * n_axes, vmem_limit_bytes=VMEM_LIMIT)


def _const_spec(shape):
    return pl.BlockSpec(shape, lambda *_: (0,) * len(shape), pipeline_mode=pl.Buffered(1))


def _rms(x, g):
    return (x * lax.rsqrt(jnp.mean(x * x, axis=-1, keepdims=True) + EPS)) * g


def _rope_tile(x, cos, sin, half, first_half):
    partner = jnp.where(first_half, pltpu.roll(x, LANES - half, 1), pltpu.roll(x, half, 1))
    return x * cos + partner * sin


def _proj_kernel(x_ref, cq_ref, sq_ref, c64_ref, s64_ref, gat_ref, win_ref, gq_ref, wqb_ref, gkv_ref,
                 ckv_ref, kr_ref, kb_ref, vb_ref, ik_ref, iw_ref, qm_ref, qb_ref, iq_ref,
                 kb16_ref, vb16_ref, ik16_ref):
    x = x_ref[...]
    hb = _rms(x, gat_ref[...]).astype(MXU_DTYPE)
    y = jnp.dot(hb, win_ref[...], preferred_element_type=jnp.float32)

    lane = lax.broadcasted_iota(jnp.int32, (1, LANES), 1)
    first_q = lane < NOPE_DIM + ROPE_DIM // 2
    first_64 = (lane & (HEAD_DIM_B - 1)) < ROT_DIM_B // 2
    cq, sq, c64, s64 = cq_ref[...], sq_ref[...], c64_ref[...], s64_ref[...]

    qn = _rms(y[:, _C_QLAT:_C_QLAT + Q_LORA], gq_ref[...]).astype(MXU_DTYPE)
    qa = jnp.dot(qn, wqb_ref[...], preferred_element_type=jnp.float32)
    scale_a = (NOPE_DIM + ROPE_DIM) ** -0.5 * LOG2E
    for h in range(N_HEADS_A):
        t = _rope_tile(qa[:, h * LANES:(h + 1) * LANES], cq, sq, ROPE_DIM // 2, first_q)
        qm_ref[:, h * LANES:(h + 1) * LANES] = (t * scale_a).astype(qm_ref.dtype)

    ckv_ref[...] = _rms(y[:, _C_KVLAT:_C_KVLAT + KV_LORA], gkv_ref[...])
    kr_ref[...] = _rope_tile(y[:, _C_KR:_C_KR + LANES], cq, sq, ROPE_DIM // 2, first_q)

    scale_b = HEAD_DIM_B ** -0.5 * LOG2E
    low = lane < HEAD_DIM_B
    heads_per_kv = N_HEADS_B // N_KV_B
    for t in range(N_HEADS_B // 2):
        pair = _rope_tile(y[:, _C_QB + t * LANES:_C_QB + (t + 1) * LANES], c64, s64, ROT_DIM_B // 2, first_64)
        pair = pair * scale_b
        swapped = pltpu.roll(pair, HEAD_DIM_B, 1)
        for u in range(2):
            h = 2 * t + u
            half = h // heads_per_kv
            src = pair if u == half else swapped
            tile = jnp.where(low if half == 0 else jnp.logical_not(low), src, 0.0)
            qb_ref[:, h * LANES:(h + 1) * LANES] = tile.astype(qb_ref.dtype)
    for t in range(N_IDX_HEADS // 2):
        pair = _rope_tile(y[:, _C_IQ + t * LANES:_C_IQ + (t + 1) * LANES], c64, s64, ROT_DIM_IDX // 2, first_64)
        swapped = pltpu.roll(pair, IDX_DIM, 1)
        for u in range(2):
            h = 2 * t + u
            iq_ref[:, h * LANES:(h + 1) * LANES] = jnp.where(low, pair if u == 0 else swapped,
                                                             0.0).astype(iq_ref.dtype)

    kb = _rope_tile(y[:, _C_KB:_C_KB + LANES], c64, s64, ROT_DIM_B // 2, first_64)
    kb_ref[...] = kb
    kb16_ref[...] = kb.astype(kb16_ref.dtype)
    vb = y[:, _C_VB:_C_VB + LANES]
    vb_ref[...] = vb
    vb16_ref[...] = vb.astype(vb16_ref.dtype)
    ik = _rope_tile(y[:, _C_IK:_C_IK + LANES], c64, s64, ROT_DIM_IDX // 2, first_64)
    ik_ref[...] = ik
    ik16_ref[...] = ik.astype(ik16_ref.dtype)
    iw_ref[...] = y[:, _C_IW:_C_IW + LANES] * ((N_IDX_HEADS * IDX_DIM) ** -0.5)


def _proj_call(x2d, tabs, n_tab_tiles, tm, g_attn, w_in_p, g_q, w_qb_p, g_kv):
    t_tok = x2d.shape[0]
    grid = (t_tok // tm,)
    row = lambda w: pl.BlockSpec((tm, w), lambda i: (i, 0))
    tab = pl.BlockSpec((tm, LANES), lambda i: (i % n_tab_tiles, 0))
    f32, b16 = jnp.float32, MXU_DTYPE
    out_shapes = [
        jax.ShapeDtypeStruct((t_tok, KV_LORA), f32),
        jax.ShapeDtypeStruct((t_tok, LANES), f32),
        jax.ShapeDtypeStruct((t_tok, LANES), f32),
        jax.ShapeDtypeStruct((t_tok, LANES), f32),
        jax.ShapeDtypeStruct((t_tok, LANES), f32),
        jax.ShapeDtypeStruct((t_tok, LANES), f32),
        jax.ShapeDtypeStruct((t_tok, N_HEADS_A * LANES), b16),
        jax.ShapeDtypeStruct((t_tok, N_HEADS_B * LANES), b16),
        jax.ShapeDtypeStruct((t_tok, N_IDX_HEADS * LANES), b16),
        jax.ShapeDtypeStruct((t_tok, LANES), b16),
        jax.ShapeDtypeStruct((t_tok, LANES), b16),
        jax.ShapeDtypeStruct((t_tok, LANES), b16),
    ]
    out_specs = [row(s.shape[1]) for s in out_shapes]
    in_specs = [row(D_MODEL), tab, tab, tab, tab, _const_spec((1, D_MODEL)), _const_spec(w_in_p.shape),
                _const_spec((1, Q_LORA)), _const_spec(w_qb_p.shape), _const_spec((1, KV_LORA))]
    return pl.pallas_call(
        _proj_kernel, grid=grid, in_specs=in_specs, out_specs=out_specs, out_shape=out_shapes,
        compiler_params=_cparams(1), name="proj",
    )(x2d, *tabs, g_attn, w_in_p, g_q, w_qb_p, g_kv)


_NT = (((1,), (1,)), ((), ()))


def _kvup_kernel(c_ref, kr_ref, wk_ref, wvt_ref, k_ref, vt_ref):
    cb = c_ref[...].astype(MXU_DTYPE)
    kn = jnp.dot(cb, wk_ref[...], preferred_element_type=jnp.float32)
    kr = kr_ref[...]
    for h in range(N_HEADS_A):
        k_ref[:, h * LANES:(h + 1) * LANES] = (kn[:, h * LANES:(h + 1) * LANES] + kr).astype(k_ref.dtype)
    vt_ref[...] = lax.dot_general(wvt_ref[...], cb, _NT, preferred_element_type=jnp.float32).astype(vt_ref.dtype)


def _kvup_call(c3d, kr3d, wk, wvt, tk):
    b, s_pad, _ = c3d.shape
    row = lambda w: pl.BlockSpec((None, tk, w), lambda bi, i: (bi, i, 0))
    vdim = N_HEADS_A * V_DIM_A
    return pl.pallas_call(
        _kvup_kernel, grid=(b, s_pad // tk),
        in_specs=[row(KV_LORA), row(LANES), _const_spec(wk.shape), _const_spec(wvt.shape)],
        out_specs=[row(N_HEADS_A * LANES), pl.BlockSpec((None, None, vdim, tk), lambda bi, i: (bi, i, 0, 0))],
        out_shape=[jax.ShapeDtypeStruct((b, s_pad, N_HEADS_A * LANES), MXU_DTYPE),
                   jax.ShapeDtypeStruct((b, s_pad // tk, vdim, tk), MXU_DTYPE)],
        compiler_params=_cparams(2), name="kvup",
    )(c3d, kr3d, wk, wvt)


_CHUNK_SHIFT = CHUNK.bit_length() - 1


def _admissible_t(p0, kb, tq, tk, s_len):
    kpos = kb * tk + lax.broadcasted_iota(jnp.int32, (tk, 1), 0)
    qpos = p0 + lax.broadcasted_iota(jnp.int32, (1, tq), 1)
    return ((qpos >> _CHUNK_SHIFT) >= (kpos >> _CHUNK_SHIFT)) & (kpos < s_len)


def _check_tiles(tq, tk, q_off, lq):
    assert tk & (tk - 1) == 0 and tk % tq == 0 and tk % CHUNK == 0, (tq, tk)
    assert q_off % tq == 0 or (lq == tq and q_off % tk + tq <= tk), (tq, tk, q_off, lq)


def _last_block(p0, tq, tk, s_len):
    return jnp.minimum(p0 + tq - 1, s_len - 1) >> (tk.bit_length() - 1)


def _fori_grouped(n, body, carry, group):
    shift = group.bit_length() - 1

    def several(j, c):
        for i in range(group):
            c = body(group * j + i, c)
        return c

    carry = lax.fori_loop(0, n >> shift, several, carry)
    return lax.fori_loop((n >> shift) << shift, n, body, carry)


def _softmax_init(n_heads, tq, acc_ref):
    acc_ref[...] = jnp.zeros(acc_ref.shape, jnp.float32)
    return jnp.full((n_heads, tq), NEG_BIG, jnp.float32), jnp.zeros((n_heads, tq), jnp.float32)


def _attn_step(state, values, next_scores, bias, s_ref, acc_ref):
    (m_prev, l_prev), maxima = state
    n_heads = s_ref.shape[0]
    sums, next_max = [], []
    if values is not None:
        m_new = jnp.maximum(m_prev, maxima)
        alpha = jnp.exp2(m_prev - m_new)
    for h in range(n_heads):
        if values is not None:
            p = jnp.exp2(s_ref[h] - m_new[h:h + 1, :])
            sums.append(jnp.sum(p, axis=0, keepdims=True))
            acc_ref[h] = alpha[h:h + 1, :] * acc_ref[h] + jnp.dot(values[h], p.astype(values[h].dtype),
                                                                   preferred_element_type=jnp.float32)
        if next_scores is not None:
            s = next_scores(h)
            if bias is not None:
                s = s + bias
            s_ref[h] = s
            next_max.append(jnp.max(s, axis=0, keepdims=True))
    carry = (m_new, alpha * l_prev + jnp.concatenate(sums, axis=0)) if values is not None else (m_prev, l_prev)
    return carry, (jnp.concatenate(next_max, axis=0) if next_scores is not None else maxima)


def _softmax_finish(l, acc_ref, o_ref):
    inv = 1.0 / l
    o_t = jnp.concatenate([acc_ref[h] * inv[h:h + 1, :] for h in range(l.shape[0])], axis=0)
    o_ref[...] = o_t.T.astype(o_ref.dtype)


def _mla_kernel(q_ref, k_ref, vt_ref, o_ref, s_ref, acc_ref, *, tq, tk, q_off, s_len):
    p0 = q_off + pl.program_id(2) * tq
    last = _last_block(p0, tq, tk, s_len)
    heads = range(MLA_HEADS_PER_STEP)

    def qk(kb):
        start = pl.multiple_of(kb * tk, tk)
        return lambda h: lax.dot_general(k_ref[pl.ds(start, tk), h * LANES:(h + 1) * LANES],
                                         q_ref[:, h * LANES:(h + 1) * LANES], _NT,
                                         preferred_element_type=jnp.float32)

    def mask_bias(kb):
        return jnp.where(_admissible_t(p0, kb, tq, tk, s_len), 0.0, NEG_BIG)

    def values(kb):
        vt_all = vt_ref[kb]
        return [vt_all[h * V_DIM_A:(h + 1) * V_DIM_A, :] for h in heads]

    def body(kb, state):
        return _attn_step(state, values(kb), qk(kb + 1), None, s_ref, acc_ref)

    def tail(state):
        return _attn_step(state, values(last - 1), qk(last), mask_bias(last), s_ref, acc_ref)

    state = _attn_step((_softmax_init(MLA_HEADS_PER_STEP, tq, acc_ref), None), None, qk(0), mask_bias(0), s_ref,
                       acc_ref)
    state = lax.fori_loop(0, jnp.maximum(last - 1, 0), body, state)
    state = lax.cond(last > 0, tail, lambda st: st, state)
    (_, l), _ = _attn_step(state, values(last), None, None, s_ref, acc_ref)
    _softmax_finish(l, acc_ref, o_ref)


def _mla_call(q, k, vt, *, tq, tk, q_off, s_len):
    b, lq, _ = q.shape
    _check_tiles(tq, tk, q_off, lq)
    s_pad = k.shape[1]
    hg = MLA_HEADS_PER_STEP
    kern = functools.partial(_mla_kernel, tq=tq, tk=tk, q_off=q_off, s_len=s_len)
    return pl.pallas_call(
        kern, grid=(b, N_HEADS_A // hg, lq // tq),
        in_specs=[pl.BlockSpec((None, tq, hg * LANES), lambda bi, g, qi: (bi, qi, g)),
                  pl.BlockSpec((None, s_pad, hg * LANES), lambda bi, g, qi: (bi, 0, g)),
                  pl.BlockSpec((None, s_pad // tk, hg * V_DIM_A, tk), lambda bi, g, qi: (bi, 0, g, 0))],
        out_specs=pl.BlockSpec((None, tq, hg * V_DIM_A), lambda bi, g, qi: (bi, qi, g)),
        out_shape=jax.ShapeDtypeStruct((b, lq, N_HEADS_A * V_DIM_A), MXU_DTYPE),
        scratch_shapes=[pltpu.VMEM((hg, tk, tq), jnp.float32), pltpu.VMEM((hg, V_DIM_A, tq), jnp.float32)],
        compiler_params=_cparams(3), name="mla",
    )(q, k, vt)


def _f2key(x):
    b = lax.bitcast_convert_type(x, jnp.int32)
    return jnp.where(b < 0, b ^ jnp.int32(0x7FFFFFFF), b)


def _key2f(k):
    return lax.bitcast_convert_type(jnp.where(k < 0, k ^ jnp.int32(0x7FFFFFFF), k), jnp.float32)


_COUNT_SLAB = 4 * SUBLANES
_SCORE_GROUP = 4
_COUNT_GROUP = 4
_SEARCH_BISECT_EVERY = 8
_SEARCH_MAX_PROBES = 40 * _SEARCH_BISECT_EVERY


def _dsa_kernel(iq_ref, qb_ref, iw_ref, ik_ref, kb_ref, vbt_ref, tri_ref, o_ref,
                sc_ref, wt_ref, s_ref, acc_ref, *, tq, tk, q_off, s_len, topk):
    p0 = q_off + pl.program_id(1) * tq
    last = _last_block(p0, tq, tk, s_len)
    n_blk = last + 1
    row = lambda v: jnp.full((1, tq), v, jnp.float32)
    colsum = lambda x: jnp.sum(x, axis=0, keepdims=True)

    wt_ref[...] = iw_ref[...].T[:SUBLANES, :]

    def score_block(kb):
        start = pl.multiple_of(kb * tk, tk)
        ikb = ik_ref[pl.ds(start, tk), :]
        acc = jnp.zeros((tk, tq), jnp.float32)
        for h in range(N_IDX_HEADS):
            x = lax.dot_general(ikb, iq_ref[:, h * LANES:(h + 1) * LANES], _NT, preferred_element_type=jnp.float32)
            acc = acc + wt_ref[h:h + 1, :] * jnp.maximum(x, 0.0)
        return acc

    def stats(carry, hi_fill, lo_fill):
        mx, mn, n_pos, n_nonneg = carry
        return (jnp.maximum(mx, jnp.max(hi_fill, axis=0, keepdims=True)),
                jnp.minimum(mn, jnp.min(lo_fill, axis=0, keepdims=True)),
                n_pos + colsum(jnp.where(hi_fill >= FLT_MIN, 1.0, 0.0)),
                n_nonneg + colsum(jnp.where(hi_fill >= 0.0, 1.0, 0.0)))

    def p1_body(kb, carry):
        acc = score_block(kb)
        sc_ref[kb] = acc
        return stats(carry, acc, acc)

    carry = _fori_grouped(last, p1_body, (row(-jnp.inf), row(jnp.inf), row(0.0), row(0.0)), _SCORE_GROUP)
    acc = score_block(last)
    adm = _admissible_t(p0, last, tq, tk, s_len)
    masked = jnp.where(adm, acc, -jnp.inf)
    sc_ref[last] = masked
    mx, mn, n_pos, n_nonneg = stats(carry, masked, jnp.where(adm, acc, jnp.inf))

    qpos = p0 + lax.broadcasted_iota(jnp.int32, (1, tq), 1)
    n_adm = jnp.minimum(((qpos >> _CHUNK_SHIFT) + 1) * CHUNK, s_len).astype(jnp.float32)
    k_f = jnp.float32(topk)
    log_k = jnp.log(k_f)
    phi = lambda c: jnp.log(jnp.maximum(c, 0.25)) - log_k

    few = n_adm <= k_f
    pos = n_pos >= k_f
    at_zero = jnp.logical_and(~pos, n_nonneg >= k_f)
    above_max = _key2f(_f2key(jnp.maximum(mx, LOWEST)) + 1)
    lo = jnp.where(pos, FLT_MIN, LOWEST)
    c_lo = jnp.where(pos, n_pos, n_adm)
    hi = jnp.where(pos | at_zero, above_max, 0.0)
    c_hi = jnp.where(pos | at_zero, 0.0, n_nonneg)
    lo_val = jnp.where(pos, 0.0, jnp.maximum(mn, LOWEST))
    tau = jnp.where(few, LOWEST, jnp.where(at_zero, 0.0, lo))
    r = jnp.where(at_zero & ~few, k_f - n_pos, 0.0)
    done = jnp.where(few | at_zero, 1.0, 0.0)

    def count_ge(g):
        def body(kb, cnt):
            hit = jnp.where(sc_ref[kb] >= g, 1.0, 0.0)
            return cnt + jnp.sum(hit.reshape(tk // _COUNT_SLAB, _COUNT_SLAB, tq), axis=0)

        return colsum(_fori_grouped(n_blk, body, jnp.zeros((_COUNT_SLAB, tq), jnp.float32), _COUNT_GROUP))

    def search_cond(st):
        return jnp.logical_and(st[1] > 0, st[0] < _SEARCH_MAX_PROBES)

    def next_probe(it, lo, hi, f_lo, f_hi, lo_val):
        lok, hik = _f2key(lo), _f2key(hi)
        midk = (lok >> 1) + (hik >> 1) + (lok & hik & 1)
        frac = f_lo / jnp.maximum(f_lo - f_hi, 1e-9)
        g_int = lo_val + (hi - lo_val) * frac
        bisect = (it % _SEARCH_BISECT_EVERY) == (_SEARCH_BISECT_EVERY - 1)
        g = jnp.where(bisect, _key2f(midk), g_int)
        gk = jnp.minimum(jnp.maximum(_f2key(g), lok + 1), jnp.maximum(hik - 1, lok + 1))
        return _key2f(gk)

    def search_body(st):
        it, _, g, lo, hi, c_lo, c_hi, f_lo, f_hi, lo_val, side, tau, r, done = st
        c = count_ge(g)
        f_g = phi(c)
        live = done == 0.0
        up = live & (c >= k_f)
        dn = live & (c < k_f)
        f_hi = jnp.where(up & (side > 0.0), f_hi * 0.5, f_hi)
        f_lo = jnp.where(dn & (side < 0.0), f_lo * 0.5, f_lo)
        lo, c_lo, f_lo, lo_val = (jnp.where(up, g, lo), jnp.where(up, c, c_lo), jnp.where(up, f_g, f_lo),
                                  jnp.where(up, g, lo_val))
        hi, c_hi, f_hi = jnp.where(dn, g, hi), jnp.where(dn, c, c_hi), jnp.where(dn, f_g, f_hi)
        side = jnp.where(up, 1.0, jnp.where(dn, -1.0, side))
        adjacent = (_f2key(lo) + 1) >= _f2key(hi)
        fin = live & ((c_lo == k_f) | adjacent)
        tau = jnp.where(fin, lo, tau)
        r = jnp.where(fin & (c_lo > k_f), k_f - c_hi, r)
        done = jnp.where(fin, 1.0, done)
        pending = jnp.sum(1.0 - done).astype(jnp.int32)
        g = next_probe(it + 1, lo, hi, f_lo, f_hi, lo_val)
        return (it + 1, pending, g, lo, hi, c_lo, c_hi, f_lo, f_hi, lo_val, side, tau, r, done)

    pending0 = jnp.sum(1.0 - done).astype(jnp.int32)
    f_lo0, f_hi0 = phi(c_lo), phi(c_hi)
    st = lax.while_loop(search_cond, search_body,
                        (jnp.int32(0), pending0, next_probe(0, lo, hi, f_lo0, f_hi0, lo_val), lo, hi, c_lo, c_hi,
                         f_lo0, f_hi0, lo_val, row(0.0), tau, r, done))
    tau, r = st[11], st[12]

    @pl.when(jnp.sum(r) > 0.0)
    def _():
        tied = r > 0.0
        budget = jnp.where(tied, r, jnp.inf)

        def body(kb, seen):
            blk = sc_ref[kb]
            eq = jnp.logical_and(blk == tau, tied)
            eq_f = jnp.where(eq, 1.0, 0.0)
            rank = jnp.dot(tri_ref[...], eq_f.astype(MXU_DTYPE), preferred_element_type=jnp.float32) + seen
            sc_ref[kb] = jnp.where(jnp.logical_and(eq, rank > budget), -jnp.inf, blk)
            return seen + colsum(eq_f)

        _fori_grouped(n_blk, body, row(0.0), _COUNT_GROUP)

    heads_per_kv = N_HEADS_B // N_KV_B

    def qk(kb):
        start = pl.multiple_of(kb * tk, tk)
        kblk = kb_ref[pl.ds(start, tk), :]
        return lambda h: lax.dot_general(kblk, qb_ref[:, h * LANES:(h + 1) * LANES], _NT,
                                         preferred_element_type=jnp.float32)

    def sel_bias(kb):
        return jnp.where(sc_ref[kb] >= tau, 0.0, NEG_BIG)

    def values(kb):
        vt_all = vbt_ref[kb]
        return [vt_all[(h // heads_per_kv) * HEAD_DIM_B:(h // heads_per_kv + 1) * HEAD_DIM_B, :]
                for h in range(N_HEADS_B)]

    def p3_body(kb, state):
        return _attn_step(state, values(kb), qk(kb + 1), sel_bias(kb + 1), s_ref, acc_ref)

    state = _attn_step((_softmax_init(N_HEADS_B, tq, acc_ref), None), None, qk(0), sel_bias(0), s_ref, acc_ref)
    state = lax.fori_loop(0, last, p3_body, state)
    (_, l), _ = _attn_step(state, values(last), None, None, s_ref, acc_ref)
    _softmax_finish(l, acc_ref, o_ref)


def _dsa_call(iq, qb, iw, ik, kb, vbt, tri, *, tq, tk, q_off, s_len):
    b, lq, _ = iq.shape
    _check_tiles(tq, tk, q_off, lq)
    s_pad = ik.shape[1]
    n_kb = s_pad // tk
    topk = min(TOPK_MAX, s_len // 4)
    kern = functools.partial(_dsa_kernel, tq=tq, tk=tk, q_off=q_off, s_len=s_len, topk=topk)
    qspec = lambda w: pl.BlockSpec((None, tq, w), lambda bi, qi: (bi, qi, 0))
    kspec = pl.BlockSpec((None, s_pad, LANES), lambda bi, qi: (bi, 0, 0))
    return pl.pallas_call(
        kern, grid=(b, lq // tq),
        in_specs=[qspec(N_IDX_HEADS * LANES), qspec(N_HEADS_B * LANES), qspec(LANES), kspec, kspec,
                  pl.BlockSpec((None, n_kb, LANES, tk), lambda bi, qi: (bi, 0, 0, 0)),
                  _const_spec(tri.shape)],
        out_specs=qspec(N_HEADS_B * HEAD_DIM_B),
        out_shape=jax.ShapeDtypeStruct((b, lq, N_HEADS_B * HEAD_DIM_B), MXU_DTYPE),
        scratch_shapes=[pltpu.VMEM((n_kb, tk, tq), jnp.float32), pltpu.VMEM((SUBLANES, tq), jnp.float32),
                        pltpu.VMEM((N_HEADS_B, tk, tq), jnp.float32),
                        pltpu.VMEM((N_HEADS_B, HEAD_DIM_B, tq), jnp.float32)],
        compiler_params=_cparams(2), name="dsa",
    )(iq, qb, iw, ik, kb, vbt, tri)


_FF_CHUNK = 256


def _out_kernel(x_ref, oa_ref, ob_ref, woa_ref, wob_ref, gf_ref, wg_ref, wu_ref, wd_ref, gfin_ref, y_ref):
    o = (jnp.dot(oa_ref[...], woa_ref[...], preferred_element_type=jnp.float32)
         + jnp.dot(ob_ref[...], wob_ref[...], preferred_element_type=jnp.float32))
    x1 = x_ref[...] + o
    h2 = _rms(x1, gf_ref[...]).astype(MXU_DTYPE)
    ffn = jnp.zeros_like(x1)
    for j in range(D_FF // _FF_CHUNK):
        sl = slice(j * _FF_CHUNK, (j + 1) * _FF_CHUNK)
        gate = jnp.dot(h2, wg_ref[:, sl], preferred_element_type=jnp.float32)
        up = jnp.dot(h2, wu_ref[:, sl], preferred_element_type=jnp.float32)
        act = (gate * jax.nn.sigmoid(gate)) * up
        ffn = ffn + jnp.dot(act.astype(MXU_DTYPE), wd_ref[sl, :], preferred_element_type=jnp.float32)
    y_ref[...] = _rms(x1 + ffn, gfin_ref[...])


def _out_call(x2d, oa, ob, woa, wob, g_ffn, wg, wu, wd, g_fin, tm):
    t_tok = x2d.shape[0]
    row = lambda w: pl.BlockSpec((tm, w), lambda i: (i, 0))
    half = N_HEADS_A * V_DIM_A
    return pl.pallas_call(
        _out_kernel, grid=(t_tok // tm,),
        in_specs=[row(D_MODEL), row(half), row(half), _const_spec(woa.shape), _const_spec(wob.shape),
                  _const_spec((1, D_MODEL)), _const_spec(wg.shape), _const_spec(wu.shape), _const_spec(wd.shape),
                  _const_spec((1, D_MODEL))],
        out_specs=row(D_MODEL), out_shape=jax.ShapeDtypeStruct((t_tok, D_MODEL), jnp.float32),
        compiler_params=_cparams(1), name="out",
    )(x2d, oa, ob, woa, wob, g_ffn, wg, wu, wd, g_fin)


def _prep_weights(w_in, w_q_b, w_kv_b, w_o, w_gate, w_up, w_down):
    off = np.concatenate([[0], np.cumsum(IN_SIZES)])
    o_qlat, o_kvlat, o_kr, o_qb, o_kb, o_vb, o_iq, o_ik, o_iw = off[:9]
    zeros = lambda n: jnp.zeros((D_MODEL, n), w_in.dtype)
    cols = [w_in[:, o_qlat:o_qlat + Q_LORA], w_in[:, o_kvlat:o_kvlat + KV_LORA],
            w_in[:, o_qb:o_qb + N_HEADS_B * HEAD_DIM_B], w_in[:, o_iq:o_iq + N_IDX_HEADS * IDX_DIM]]
    cols += [w_in[:, o_kb:o_kb + LANES], w_in[:, o_vb:o_vb + LANES],
             w_in[:, o_ik:o_ik + IDX_DIM], zeros(LANES - IDX_DIM),
             zeros(NOPE_DIM), w_in[:, o_kr:o_kr + ROPE_DIM], zeros(LANES - NOPE_DIM - ROPE_DIM),
             w_in[:, o_iw:o_iw + N_IDX_HEADS], zeros(LANES - N_IDX_HEADS)]
    w_in_p = jnp.concatenate(cols, axis=1).astype(MXU_DTYPE)
    assert w_in_p.shape[1] == _C_END

    qd = NOPE_DIM + ROPE_DIM
    w_qb_p = jnp.pad(w_q_b.reshape(Q_LORA, N_HEADS_A, qd), ((0, 0), (0, 0), (0, LANES - qd)))
    w_qb_p = w_qb_p.reshape(Q_LORA, N_HEADS_A * LANES).astype(MXU_DTYPE)

    kv = w_kv_b.reshape(KV_LORA, N_HEADS_A, NOPE_DIM + V_DIM_A)
    wk = jnp.pad(kv[:, :, :NOPE_DIM], ((0, 0), (0, 0), (0, LANES - NOPE_DIM)))
    wk = wk.reshape(KV_LORA, N_HEADS_A * LANES).astype(MXU_DTYPE)
    wvt = kv[:, :, NOPE_DIM:].reshape(KV_LORA, N_HEADS_A * V_DIM_A).T.astype(MXU_DTYPE)

    half = N_HEADS_A * V_DIM_A
    return (w_in_p, w_qb_p, wk, wvt, w_o[:half].astype(MXU_DTYPE), w_o[half:].astype(MXU_DTYPE),
            w_gate.astype(MXU_DTYPE), w_up.astype(MXU_DTYPE), w_down.astype(MXU_DTYPE))


def _rope_tables(pos):
    posf = pos.astype(jnp.float32)[:, None]
    n = pos.shape[0]

    def cs(d):
        inv = jnp.power(ROPE_THETA, -jnp.arange(0, d, 2, dtype=jnp.float32) / d)
        ang = posf * inv[None, :]
        return jnp.cos(ang), jnp.sin(ang)

    one, zero = (lambda w: jnp.ones((n, w), jnp.float32)), (lambda w: jnp.zeros((n, w), jnp.float32))
    c, s = cs(ROPE_DIM)
    cq = jnp.concatenate([one(NOPE_DIM), c, c, one(LANES - NOPE_DIM - ROPE_DIM)], axis=1)
    sq = jnp.concatenate([zero(NOPE_DIM), -s, s, zero(LANES - NOPE_DIM - ROPE_DIM)], axis=1)
    c, s = cs(ROT_DIM_B)
    c64 = jnp.tile(jnp.concatenate([c, c, one(HEAD_DIM_B - ROT_DIM_B)], axis=1), (1, LANES // HEAD_DIM_B))
    s64 = jnp.tile(jnp.concatenate([-s, s, zero(HEAD_DIM_B - ROT_DIM_B)], axis=1), (1, LANES // HEAD_DIM_B))
    return cq, sq, c64, s64


def _pick_tile(n, pref):
    t = min(pref, n)
    while n % t:
        t //= 2
    return t


def _round_up(n, m):
    return -(-n // m) * m


def _layer(x, pos0, past, weights, norms, final_norm):
    b, l, _ = x.shape
    g_attn, g_q, g_kv, g_ffn = norms
    w_in_p, w_qb_p, wk, wvt, woa, wob, wg, wu, wd = weights
    t_tok = b * l
    x2d = x.reshape(t_tok, D_MODEL)

    tabs = _rope_tables(pos0 + jnp.arange(l))
    if l >= TM // 2:
        tm = _pick_tile(l, TM)
        n_tab_tiles = l // tm
    else:
        tm = _pick_tile(t_tok, TM)
        tabs = tuple(jnp.tile(t, (tm // l, 1)) for t in tabs)
        n_tab_tiles = 1
    (ckv, kr128, kb, vb, ik128, iw, q_mla, q_b, iq, kb16, vb16, ik16) = _proj_call(
        x2d, tabs, n_tab_tiles, tm, g_attn, w_in_p, g_q, w_qb_p, g_kv)

    def per_head(a):
        halves = [a[:, j * HEAD_DIM_B:(j + 1) * HEAD_DIM_B].reshape(b, l, 1, HEAD_DIM_B) for j in range(N_KV_B)]
        return jnp.concatenate(halves, axis=2)

    new = (ckv.reshape(b, l, KV_LORA), kr128.reshape(b, l, LANES)[..., NOPE_DIM:NOPE_DIM + ROPE_DIM],
           per_head(kb), per_head(vb),
           ik128.reshape(b, l, LANES)[..., :IDX_DIM])

    per_row = lambda a: a.reshape(b, l, a.shape[-1])
    p_len = 0 if past is None else past[0].shape[1]
    s_len = p_len + l
    lq = _round_up(l, LANES)
    tq_mla, tq_dsa = _pick_tile(lq, TQ_MLA), _pick_tile(lq, TQ_DSA)
    tk_mla, tk_dsa = (2 * TK_MLA, 2 * TK_DSA) if lq < TQ_DSA else (TK_MLA, TK_DSA)
    s_pad = _round_up(s_len, max(tk_mla, tk_dsa))
    if past is None:
        old = [None] * 5
    else:
        p_lat, p_kr, p_k, p_v, p_ik = past
        old = [p_lat,
               jnp.pad(p_kr, ((0, 0), (0, 0), (NOPE_DIM, LANES - NOPE_DIM - ROPE_DIM))),
               p_k.reshape(b, p_len, LANES).astype(MXU_DTYPE), p_v.reshape(b, p_len, LANES).astype(MXU_DTYPE),
               jnp.pad(p_ik, ((0, 0), (0, 0), (0, LANES - IDX_DIM))).astype(MXU_DTYPE)]

    def full_seq(prev, cur):
        parts = ([] if prev is None else [prev]) + [per_row(cur)]
        if s_pad > s_len:
            parts.append(jnp.zeros((b, s_pad - s_len, cur.shape[-1]), cur.dtype))
        return parts[0] if len(parts) == 1 else jnp.concatenate(parts, axis=1)

    c_all, kr_all, kb_all, vb_all, ik_all = (full_seq(o, c) for o, c in zip(old, (ckv, kr128, kb16, vb16, ik16)))
    vbt_all = vb_all.reshape(b, s_pad // tk_dsa, tk_dsa, LANES).transpose(0, 1, 3, 2)

    k_mla, vt_mla = _kvup_call(c_all, kr_all, wk, wvt, tk_mla)

    padq = lambda a: per_row(a) if lq == l else jnp.pad(per_row(a), ((0, 0), (0, lq - l), (0, 0)))
    o_a = _mla_call(padq(q_mla), k_mla, vt_mla, tq=tq_mla, tk=tk_mla, q_off=pos0, s_len=s_len)
    tri = jnp.tril(jnp.ones((tk_dsa, tk_dsa), MXU_DTYPE))
    o_b = _dsa_call(padq(iq), padq(q_b), padq(iw), ik_all, kb_all, vbt_all, tri,
                    tq=tq_dsa, tk=tk_dsa, q_off=pos0, s_len=s_len)
    o_a, o_b = (o[:, :l].reshape(t_tok, -1) for o in (o_a, o_b))

    y = _out_call(x2d, o_a, o_b, woa, wob, g_ffn, wg, wu, wd, final_norm, tm)
    return y.reshape(b, l, D_MODEL), new


def kernel(x_prompt, x_sample, cache_mla_latent, cache_mla_krope, cache_dsa_k, cache_dsa_v, cache_idx_k, attn_norm, w_in, q_a_norm, w_q_b, kv_a_norm, w_kv_b, w_o, ffn_norm, w_gate, w_up, w_down, final_norm):
    depth = w_in.shape[0]
    assert depth == 1, "the fused final norm assumes a single layer"
    past_len = cache_mla_latent.shape[2]
    li = 0
    weights = _prep_weights(w_in[li], w_q_b[li], w_kv_b[li], w_o[li], w_gate[li], w_up[li], w_down[li])
    norms = (attn_norm[li][None, :], q_a_norm[li][None, :], kv_a_norm[li][None, :], ffn_norm[li][None, :])
    fin = final_norm[None, :]
    past = (cache_mla_latent[li], cache_mla_krope[li], cache_dsa_k[li], cache_dsa_v[li], cache_idx_k[li])
    y_p, new_p = _layer(x_prompt, 0, None, weights, norms, fin)
    y_s, new_s = _layer(x_sample, past_len, past, weights, norms, fin)
    return (y_p, y_s) + tuple(a[None] for a in new_p) + tuple(a[None] for a in new_s)
```

```python
import functools

import numpy as np
import jax
import jax.numpy as jnp
from jax import lax
from jax.experimental import pallas as pl
from jax.experimental.pallas import tpu as pltpu

D_MODEL = 1024
CHUNK = 64
ROPE_THETA = 500000.0
EPS = 1e-6
N_HEADS_A = 8
NOPE_DIM = 64
ROPE_DIM = 32
V_DIM_A = 64
Q_LORA = 384
KV_LORA = 256
N_HEADS_B = 8
N_KV_B = 2
HEAD_DIM_B = 64
ROT_DIM_B = 16
N_IDX_HEADS = 8
IDX_DIM = 64
ROT_DIM_IDX = 16
TOPK_MAX = 256
D_FF = 2816
IN_SIZES = (Q_LORA, KV_LORA, ROPE_DIM, N_HEADS_B * HEAD_DIM_B, N_KV_B * HEAD_DIM_B, N_KV_B * HEAD_DIM_B,
            N_IDX_HEADS * IDX_DIM, IDX_DIM, N_IDX_HEADS)

LANES = 128
SUBLANES = 8
MXU_DTYPE = jnp.bfloat16
VMEM_LIMIT = 60 * 1024 * 1024
NEG_BIG = -1e30
LOWEST = -3.0e38
FLT_MIN = 1.1754943508222875e-38
LOG2E = 1.4426950408889634

TM = 512
TQ_MLA = 512
TK_MLA = 512
TQ_DSA = 256
TK_DSA = 512
MLA_HEADS_PER_STEP = 4

_C_QLAT = 0
_C_KVLAT = _C_QLAT + Q_LORA
_C_QB = _C_KVLAT + KV_LORA
_C_IQ = _C_QB + N_HEADS_B * HEAD_DIM_B
_C_KB = _C_IQ + N_IDX_HEADS * IDX_DIM
_C_VB = _C_KB + LANES
_C_IK = _C_VB + LANES
_C_KR = _C_IK + LANES
_C_IW = _C_KR + LANES
_C_END = _C_IW + LANES


def _cparams(n_axes):
    return pltpu.CompilerParams(dimension_semantics=("arbitrary",) * n_axes, vmem_limit_bytes=VMEM_LIMIT)


def _const_spec(shape):
    return pl.BlockSpec(shape, lambda *_: (0,) * len(shape), pipeline_mode=pl.Buffered(1))


def _rms(x, g):
    return (x * lax.rsqrt(jnp.mean(x * x, axis=-1, keepdims=True) + EPS)) * g


def _rope_tile(x, cos, sin, half, first_half):
    partner = jnp.where(first_half, pltpu.roll(x, LANES - half, 1), pltpu.roll(x, half, 1))
    return x * cos + partner * sin


def _proj_kernel(x_ref, cq_ref, sq_ref, c64_ref, s64_ref, gat_ref, win_ref, gq_ref, wqb_ref, gkv_ref,
                 ckv_ref, kr_ref, kb_ref, vb_ref, ik_ref, iw_ref, qm_ref, qb_ref, iq_ref,
                 kb16_ref, vb16_ref, ik16_ref, kr32_ref):
    x = x_ref[...]
    hb = _rms(x, gat_ref[...]).astype(MXU_DTYPE)
    y = jnp.dot(hb, win_ref[...], preferred_element_type=jnp.float32)

    lane = lax.broadcasted_iota(jnp.int32, (1, LANES), 1)
    first_q = lane < NOPE_DIM + ROPE_DIM // 2
    first_64 = (lane & (HEAD_DIM_B - 1)) < ROT_DIM_B // 2
    cq, sq, c64, s64 = cq_ref[...], sq_ref[...], c64_ref[...], s64_ref[...]

    qn = _rms(y[:, _C_QLAT:_C_QLAT + Q_LORA], gq_ref[...]).astype(MXU_DTYPE)
    qa = jnp.dot(qn, wqb_ref[...], preferred_element_type=jnp.float32)
    scale_a = (NOPE_DIM + ROPE_DIM) ** -0.5 * LOG2E
    for h in range(N_HEADS_A):
        t = _rope_tile(qa[:, h * LANES:(h + 1) * LANES], cq, sq, ROPE_DIM // 2, first_q)
        qm_ref[:, h * LANES:(h + 1) * LANES] = (t * scale_a).astype(qm_ref.dtype)

    ckv_ref[...] = _rms(y[:, _C_KVLAT:_C_KVLAT + KV_LORA], gkv_ref[...])
    kr = _rope_tile(y[:, _C_KR:_C_KR + LANES], cq, sq, ROPE_DIM // 2, first_q)
    kr_ref[...] = kr
    kr32_ref[...] = pltpu.roll(kr, LANES - NOPE_DIM, 1)[:, :ROPE_DIM]

    scale_b = HEAD_DIM_B ** -0.5 * LOG2E
    low = lane < HEAD_DIM_B
    heads_per_kv = N_HEADS_B // N_KV_B
    for t in range(N_HEADS_B // 2):
        pair = _rope_tile(y[:, _C_QB + t * LANES:_C_QB + (t + 1) * LANES], c64, s64, ROT_DIM_B // 2, first_64)
        pair = pair * scale_b
        swapped = pltpu.roll(pair, HEAD_DIM_B, 1)
        for u in range(2):
            h = 2 * t + u
            half = h // heads_per_kv
            src = pair if u == half else swapped
            tile = jnp.where(low if half == 0 else jnp.logical_not(low), src, 0.0)
            qb_ref[:, h * LANES:(h + 1) * LANES] = tile.astype(qb_ref.dtype)
    for t in range(N_IDX_HEADS // 2):
        pair = _rope_tile(y[:, _C_IQ + t * LANES:_C_IQ + (t + 1) * LANES], c64, s64, ROT_DIM_IDX // 2, first_64)
        swapped = pltpu.roll(pair, IDX_DIM, 1)
        for u in range(2):
            h = 2 * t + u
            iq_ref[:, h * LANES:(h + 1) * LANES] = jnp.where(low, pair if u == 0 else swapped,
                                                             0.0).astype(iq_ref.dtype)

    kb = _rope_tile(y[:, _C_KB:_C_KB + LANES], c64, s64, ROT_DIM_B // 2, first_64)
    kb_ref[...] = kb
    kb16_ref[...] = kb.astype(kb16_ref.dtype)
    vb = y[:, _C_VB:_C_VB + LANES]
    vb_ref[...] = vb
    vb16_ref[...] = vb.astype(vb16_ref.dtype)
    ik = _rope_tile(y[:, _C_IK:_C_IK + LANES], c64, s64, ROT_DIM_IDX // 2, first_64)
    ik_ref[...] = ik[:, :IDX_DIM]
    ik16_ref[...] = ik.astype(ik16_ref.dtype)
    iw_ref[...] = y[:, _C_IW:_C_IW + LANES] * ((N_IDX_HEADS * IDX_DIM) ** -0.5)


def _proj_call(x2d, tabs, n_tab_tiles, tm, g_attn, w_in_p, g_q, w_qb_p, g_kv):
    t_tok = x2d.shape[0]
    grid = (t_tok // tm,)
    row = lambda w: pl.BlockSpec((tm, w), lambda i: (i, 0))
    tab = pl.BlockSpec((tm, LANES), lambda i: (i % n_tab_tiles, 0))
    f32, b16 = jnp.float32, MXU_DTYPE
    out_shapes = [
        jax.ShapeDtypeStruct((t_tok, KV_LORA), f32),
        jax.ShapeDtypeStruct((t_tok, LANES), f32),
        jax.ShapeDtypeStruct((t_tok, LANES), f32),
        jax.ShapeDtypeStruct((t_tok, LANES), f32),
        jax.ShapeDtypeStruct((t_tok, IDX_DIM), f32),
        jax.ShapeDtypeStruct((t_tok, LANES), f32),
        jax.ShapeDtypeStruct((t_tok, N_HEADS_A * LANES), b16),
        jax.ShapeDtypeStruct((t_tok, N_HEADS_B * LANES), b16),
        jax.ShapeDtypeStruct((t_tok, N_IDX_HEADS * LANES), b16),
        jax.ShapeDtypeStruct((t_tok, LANES), b16),
        jax.ShapeDtypeStruct((t_tok, LANES), b16),
        jax.ShapeDtypeStruct((t_tok, LANES), b16),
        jax.ShapeDtypeStruct((t_tok, ROPE_DIM), f32),
    ]
    out_specs = [row(s.shape[1]) for s in out_shapes]
    in_specs = [row(D_MODEL), tab, tab, tab, tab, _const_spec((1, D_MODEL)), _const_spec(w_in_p.shape),
                _const_spec((1, Q_LORA)), _const_spec(w_qb_p.shape), _const_spec((1, KV_LORA))]
    return pl.pallas_call(
        _proj_kernel, grid=grid, in_specs=in_specs, out_specs=out_specs, out_shape=out_shapes,
        compiler_params=_cparams(1), name="proj",
    )(x2d, *tabs, g_attn, w_in_p, g_q, w_qb_p, g_kv)


_NT = (((1,), (1,)), ((), ()))


def _kvup_kernel(c_ref, kr_ref, wk_ref, wvt_ref, k_ref, vt_ref):
    cb = c_ref[...].astype(MXU_DTYPE)
    kn = jnp.dot(cb, wk_ref[...], preferred_element_type=jnp.float32)
    kr = kr_ref[...]
    for h in range(N_HEADS_A):
        k_ref[:, h * LANES:(h + 1) * LANES] = (kn[:, h * LANES:(h + 1) * LANES] + kr).astype(k_ref.dtype)
    vt_ref[...] = lax.dot_general(wvt_ref[...], cb, _NT, preferred_element_type=jnp.float32).astype(vt_ref.dtype)


def _kvup_call(c3d, kr3d, wk, wvt, tk):
    b, s_pad, _ = c3d.shape
    row = lambda w: pl.BlockSpec((None, tk, w), lambda bi, i: (bi, i, 0))
    vdim = N_HEADS_A * V_DIM_A
    return pl.pallas_call(
        _kvup_kernel, grid=(b, s_pad // tk),
        in_specs=[row(KV_LORA), row(LANES), _const_spec(wk.shape), _const_spec(wvt.shape)],
        out_specs=[row(N_HEADS_A * LANES), pl.BlockSpec((None, None, vdim, tk), lambda bi, i: (bi, i, 0, 0))],
        out_shape=[jax.ShapeDtypeStruct((b, s_pad, N_HEADS_A * LANES), MXU_DTYPE),
                   jax.ShapeDtypeStruct((b, s_pad // tk, vdim, tk), MXU_DTYPE)],
        compiler_params=_cparams(2), name="kvup",
    )(c3d, kr3d, wk, wvt)


_CHUNK_SHIFT = CHUNK.bit_length() - 1


def _admissible_t(p0, kb, tq, tk, s_len):
    kpos = kb * tk + lax.broadcasted_iota(jnp.int32, (tk, 1), 0)
    qpos = p0 + lax.broadcasted_iota(jnp.int32, (1, tq), 1)
    return ((qpos >> _CHUNK_SHIFT) >= (kpos >> _CHUNK_SHIFT)) & (kpos < s_len)


def _check_tiles(tq, tk, q_off, lq):
    assert tk & (tk - 1) == 0 and tk % tq == 0 and tk % CHUNK == 0, (tq, tk)
    assert q_off % tq == 0 or (lq == tq and q_off % tk + tq <= tk), (tq, tk, q_off, lq)


def _last_block(p0, tq, tk, s_len):
    return jnp.minimum(p0 + tq - 1, s_len - 1) >> (tk.bit_length() - 1)


def _fori_grouped(n, body, carry, group):
    shift = group.bit_length() - 1

    def several(j, c):
        for i in range(group):
            c = body(group * j + i, c)
        return c

    carry = lax.fori_loop(0, n >> shift, several, carry)
    return lax.fori_loop((n >> shift) << shift, n, body, carry)


def _softmax_init(n_heads, tq, acc_ref):
    acc_ref[...] = jnp.zeros(acc_ref.shape, jnp.float32)
    return jnp.full((n_heads, tq), NEG_BIG, jnp.float32), jnp.zeros((n_heads, tq), jnp.float32)


def _attn_step(state, values, next_scores, bias, s_ref, acc_ref):
    (m_prev, l_prev), maxima = state
    n_heads = s_ref.shape[0]
    sums, next_max = [], []
    if values is not None:
        m_new = jnp.maximum(m_prev, maxima)
        alpha = jnp.exp2(m_prev - m_new)
    for h in range(n_heads):
        if values is not None:
            p = jnp.exp2(s_ref[h] - m_new[h:h + 1, :])
            sums.append(jnp.sum(p, axis=0, keepdims=True))
            acc_ref[h] = alpha[h:h + 1, :] * acc_ref[h] + jnp.dot(values[h], p.astype(values[h].dtype),
                                                                   preferred_element_type=jnp.float32)
        if next_scores is not None:
            s = next_scores(h)
            if bias is not None:
                s = s + bias
            s_ref[h] = s
            next_max.append(jnp.max(s, axis=0, keepdims=True))
    carry = (m_new, alpha * l_prev + jnp.concatenate(sums, axis=0)) if values is not None else (m_prev, l_prev)
    return carry, (jnp.concatenate(next_max, axis=0) if next_scores is not None else maxima)


def _softmax_finish(l, acc_ref, o_ref):
    inv = 1.0 / l
    o_t = jnp.concatenate([acc_ref[h] * inv[h:h + 1, :] for h in range(l.shape[0])], axis=0)
    o_ref[...] = o_t.T.astype(o_ref.dtype)


def _mla_kernel(q_ref, k_ref, vt_ref, o_ref, s_ref, acc_ref, *, tq, tk, q_off, s_len):
    p0 = q_off + pl.program_id(2) * tq
    last = _last_block(p0, tq, tk, s_len)
    heads = range(MLA_HEADS_PER_STEP)

    def qk(kb):
        start = pl.multiple_of(kb * tk, tk)
        return lambda h: lax.dot_general(k_ref[pl.ds(start, tk), h * LANES:(h + 1) * LANES],
                                         q_ref[:, h * LANES:(h + 1) * LANES], _NT,
                                         preferred_element_type=jnp.float32)

    def mask_bias(kb):
        return jnp.where(_admissible_t(p0, kb, tq, tk, s_len), 0.0, NEG_BIG)

    def values(kb):
        vt_all = vt_ref[kb]
        return [vt_all[h * V_DIM_A:(h + 1) * V_DIM_A, :] for h in heads]

    def body(kb, state):
        return _attn_step(state, values(kb), qk(kb + 1), None, s_ref, acc_ref)

    def tail(state):
        return _attn_step(state, values(last - 1), qk(last), mask_bias(last), s_ref, acc_ref)

    state = _attn_step((_softmax_init(MLA_HEADS_PER_STEP, tq, acc_ref), None), None, qk(0), mask_bias(0), s_ref,
                       acc_ref)
    state = lax.fori_loop(0, jnp.maximum(last - 1, 0), body, state)
    state = lax.cond(last > 0, tail, lambda st: st, state)
    (_, l), _ = _attn_step(state, values(last), None, None, s_ref, acc_ref)
    _softmax_finish(l, acc_ref, o_ref)


def _mla_call(q, k, vt, *, tq, tk, q_off, s_len):
    b, lq, _ = q.shape
    _check_tiles(tq, tk, q_off, lq)
    s_pad = k.shape[1]
    hg = MLA_HEADS_PER_STEP
    kern = functools.partial(_mla_kernel, tq=tq, tk=tk, q_off=q_off, s_len=s_len)
    return pl.pallas_call(
        kern, grid=(b, N_HEADS_A // hg, lq // tq),
        in_specs=[pl.BlockSpec((None, tq, hg * LANES), lambda bi, g, qi: (bi, qi, g)),
                  pl.BlockSpec((None, s_pad, hg * LANES), lambda bi, g, qi: (bi, 0, g)),
                  pl.BlockSpec((None, s_pad // tk, hg * V_DIM_A, tk), lambda bi, g, qi: (bi, 0, g, 0))],
        out_specs=pl.BlockSpec((None, tq, hg * V_DIM_A), lambda bi, g, qi: (bi, qi, g)),
        out_shape=jax.ShapeDtypeStruct((b, lq, N_HEADS_A * V_DIM_A), MXU_DTYPE),
        scratch_shapes=[pltpu.VMEM((hg, tk, tq), jnp.float32), pltpu.VMEM((hg, V_DIM_A, tq), jnp.float32)],
        compiler_params=_cparams(3), name="mla",
    )(q, k, vt)


def _f2key(x):
    b = lax.bitcast_convert_type(x, jnp.int32)
    return jnp.where(b < 0, b ^ jnp.int32(0x7FFFFFFF), b)


def _key2f(k):
    return lax.bitcast_convert_type(jnp.where(k < 0, k ^ jnp.int32(0x7FFFFFFF), k), jnp.float32)


_COUNT_SLAB = 4 * SUBLANES
_SCORE_GROUP = 4
_COUNT_GROUP = 4
_SEARCH_BISECT_EVERY = 8
_SEARCH_MAX_PROBES = 40 * _SEARCH_BISECT_EVERY


def _dsa_kernel(iq_ref, qb_ref, iw_ref, ik_ref, kb_ref, vbt_ref, tri_ref, o_ref,
                sc_ref, wt_ref, s_ref, acc_ref, *, tq, tk, q_off, s_len, topk):
    p0 = q_off + pl.program_id(1) * tq
    last = _last_block(p0, tq, tk, s_len)
    n_blk = last + 1
    row = lambda v: jnp.full((1, tq), v, jnp.float32)
    colsum = lambda x: jnp.sum(x, axis=0, keepdims=True)

    wt_ref[...] = iw_ref[...].T[:SUBLANES, :]

    def score_block(kb):
        start = pl.multiple_of(kb * tk, tk)
        ikb = ik_ref[pl.ds(start, tk), :]
        acc = jnp.zeros((tk, tq), jnp.float32)
        for h in range(N_IDX_HEADS):
            x = lax.dot_general(ikb, iq_ref[:, h * LANES:(h + 1) * LANES], _NT, preferred_element_type=jnp.float32)
            acc = acc + wt_ref[h:h + 1, :] * jnp.maximum(x, 0.0)
        return acc

    def stats(carry, hi_fill, lo_fill):
        mx, mn, n_pos, n_nonneg = carry
        return (jnp.maximum(mx, jnp.max(hi_fill, axis=0, keepdims=True)),
                jnp.minimum(mn, jnp.min(lo_fill, axis=0, keepdims=True)),
                n_pos + colsum(jnp.where(hi_fill >= FLT_MIN, 1.0, 0.0)),
                n_nonneg + colsum(jnp.where(hi_fill >= 0.0, 1.0, 0.0)))

    def p1_body(kb, carry):
        acc = score_block(kb)
        sc_ref[kb] = acc
        return stats(carry, acc, acc)

    carry = _fori_grouped(last, p1_body, (row(-jnp.inf), row(jnp.inf), row(0.0), row(0.0)), _SCORE_GROUP)
    acc = score_block(last)
    adm = _admissible_t(p0, last, tq, tk, s_len)
    masked = jnp.where(adm, acc, -jnp.inf)
    sc_ref[last] = masked
    mx, mn, n_pos, n_nonneg = stats(carry, masked, jnp.where(adm, acc, jnp.inf))

    qpos = p0 + lax.broadcasted_iota(jnp.int32, (1, tq), 1)
    n_adm = jnp.minimum(((qpos >> _CHUNK_SHIFT) + 1) * CHUNK, s_len).astype(jnp.float32)
    k_f = jnp.float32(topk)
    log_k = jnp.log(k_f)
    phi = lambda c: jnp.log(jnp.maximum(c, 0.25)) - log_k

    few = n_adm <= k_f
    pos = n_pos >= k_f
    at_zero = jnp.logical_and(~pos, n_nonneg >= k_f)
    above_max = _key2f(_f2key(jnp.maximum(mx, LOWEST)) + 1)
    lo = jnp.where(pos, FLT_MIN, LOWEST)
    c_lo = jnp.where(pos, n_pos, n_adm)
    hi = jnp.where(pos | at_zero, above_max, 0.0)
    c_hi = jnp.where(pos | at_zero, 0.0, n_nonneg)
    lo_val = jnp.where(pos, 0.0, jnp.maximum(mn, LOWEST))
    tau = jnp.where(few, LOWEST, jnp.where(at_zero, 0.0, lo))
    r = jnp.where(at_zero & ~few, k_f - n_pos, 0.0)
    done = jnp.where(few | at_zero, 1.0, 0.0)

    def count_ge(g):
        def body(kb, cnt):
            hit = jnp.where(sc_ref[kb] >= g, 1.0, 0.0)
            return cnt + jnp.sum(hit.reshape(tk // _COUNT_SLAB, _COUNT_SLAB, tq), axis=0)

        return colsum(_fori_grouped(n_blk, body, jnp.zeros((_COUNT_SLAB, tq), jnp.float32), _COUNT_GROUP))

    def search_cond(st):
        return jnp.logical_and(st[1] > 0, st[0] < _SEARCH_MAX_PROBES)

    def next_probe(it, lo, hi, f_lo, f_hi, lo_val):
        lok, hik = _f2key(lo), _f2key(hi)
        midk = (lok >> 1) + (hik >> 1) + (lok & hik & 1)
        frac = f_lo / jnp.maximum(f_lo - f_hi, 1e-9)
        g_int = lo_val + (hi - lo_val) * frac
        bisect = (it % _SEARCH_BISECT_EVERY) == (_SEARCH_BISECT_EVERY - 1)
        g = jnp.where(bisect, _key2f(midk), g_int)
        gk = jnp.minimum(jnp.maximum(_f2key(g), lok + 1), jnp.maximum(hik - 1, lok + 1))
        return _key2f(gk)

    def search_body(st):
        it, _, g, lo, hi, c_lo, c_hi, f_lo, f_hi, lo_val, side, tau, r, done = st
        c = count_ge(g)
        f_g = phi(c)
        live = done == 0.0
        up = live & (c >= k_f)
        dn = live & (c < k_f)
        f_hi = jnp.where(up & (side > 0.0), f_hi * 0.5, f_hi)
        f_lo = jnp.where(dn & (side < 0.0), f_lo * 0.5, f_lo)
        lo, c_lo, f_lo, lo_val = (jnp.where(up, g, lo), jnp.where(up, c, c_lo), jnp.where(up, f_g, f_lo),
                                  jnp.where(up, g, lo_val))
        hi, c_hi, f_hi = jnp.where(dn, g, hi), jnp.where(dn, c, c_hi), jnp.where(dn, f_g, f_hi)
        side = jnp.where(up, 1.0, jnp.where(dn, -1.0, side))
        adjacent = (_f2key(lo) + 1) >= _f2key(hi)
        fin = live & ((c_lo == k_f) | adjacent)
        tau = jnp.where(fin, lo, tau)
        r = jnp.where(fin & (c_lo > k_f), k_f - c_hi, r)
        done = jnp.where(fin, 1.0, done)
        pending = jnp.sum(1.0 - done).astype(jnp.int32)
        g = next_probe(it + 1, lo, hi, f_lo, f_hi, lo_val)
        return (it + 1, pending, g, lo, hi, c_lo, c_hi, f_lo, f_hi, lo_val, side, tau, r, done)

    pending0 = jnp.sum(1.0 - done).astype(jnp.int32)
    f_lo0, f_hi0 = phi(c_lo), phi(c_hi)
    st = lax.while_loop(search_cond, search_body,
                        (jnp.int32(0), pending0, next_probe(0, lo, hi, f_lo0, f_hi0, lo_val), lo, hi, c_lo, c_hi,
                         f_lo0, f_hi0, lo_val, row(0.0), tau, r, done))
    tau, r = st[11], st[12]

    @pl.when(jnp.sum(r) > 0.0)
    def _():
        tied = r > 0.0
        budget = jnp.where(tied, r, jnp.inf)

        def body(kb, seen):
            blk = sc_ref[kb]
            eq = jnp.logical_and(blk == tau, tied)
            eq_f = jnp.where(eq, 1.0, 0.0)
            rank = jnp.dot(tri_ref[...], eq_f.astype(MXU_DTYPE), preferred_element_type=jnp.float32) + seen
            sc_ref[kb] = jnp.where(jnp.logical_and(eq, rank > budget), -jnp.inf, blk)
            return seen + colsum(eq_f)

        _fori_grouped(n_blk, body, row(0.0), _COUNT_GROUP)

    heads_per_kv = N_HEADS_B // N_KV_B

    def qk(kb):
        start = pl.multiple_of(kb * tk, tk)
        kblk = kb_ref[pl.ds(start, tk), :]
        return lambda h: lax.dot_general(kblk, qb_ref[:, h * LANES:(h + 1) * LANES], _NT,
                                         preferred_element_type=jnp.float32)

    def sel_bias(kb):
        return jnp.where(sc_ref[kb] >= tau, 0.0, NEG_BIG)

    def values(kb):
        vt_all = vbt_ref[kb]
        return [vt_all[(h // heads_per_kv) * HEAD_DIM_B:(h // heads_per_kv + 1) * HEAD_DIM_B, :]
                for h in range(N_HEADS_B)]

    def p3_body(kb, state):
        return _attn_step(state, values(kb), qk(kb + 1), sel_bias(kb + 1), s_ref, acc_ref)

    state = _attn_step((_softmax_init(N_HEADS_B, tq, acc_ref), None), None, qk(0), sel_bias(0), s_ref, acc_ref)
    state = lax.fori_loop(0, last, p3_body, state)
    (_, l), _ = _attn_step(state, values(last), None, None, s_ref, acc_ref)
    _softmax_finish(l, acc_ref, o_ref)


def _dsa_call(iq, qb, iw, ik, kb, vbt, tri, *, tq, tk, q_off, s_len):
    b, lq, _ = iq.shape
    _check_tiles(tq, tk, q_off, lq)
    s_pad = ik.shape[1]
    n_kb = s_pad // tk
    topk = min(TOPK_MAX, s_len // 4)
    kern = functools.partial(_dsa_kernel, tq=tq, tk=tk, q_off=q_off, s_len=s_len, topk=topk)
    qspec = lambda w: pl.BlockSpec((None, tq, w), lambda bi, qi: (bi, qi, 0))
    kspec = pl.BlockSpec((None, s_pad, LANES), lambda bi, qi: (bi, 0, 0))
    return pl.pallas_call(
        kern, grid=(b, lq // tq),
        in_specs=[qspec(N_IDX_HEADS * LANES), qspec(N_HEADS_B * LANES), qspec(LANES), kspec, kspec,
                  pl.BlockSpec((None, n_kb, LANES, tk), lambda bi, qi: (bi, 0, 0, 0)),
                  _const_spec(tri.shape)],
        out_specs=qspec(N_HEADS_B * HEAD_DIM_B),
        out_shape=jax.ShapeDtypeStruct((b, lq, N_HEADS_B * HEAD_DIM_B), MXU_DTYPE),
        scratch_shapes=[pltpu.VMEM((n_kb, tk, tq), jnp.float32), pltpu.VMEM((SUBLANES, tq), jnp.float32),
                        pltpu.VMEM((N_HEADS_B, tk, tq), jnp.float32),
                        pltpu.VMEM((N_HEADS_B, HEAD_DIM_B, tq), jnp.float32)],
        compiler_params=_cparams(2), name="dsa",
    )(iq, qb, iw, ik, kb, vbt, tri)


_FF_CHUNK = 256


def _out_kernel(x_ref, oa_ref, ob_ref, woa_ref, wob_ref, gf_ref, wg_ref, wu_ref, wd_ref, gfin_ref, y_ref):
    o = (jnp.dot(oa_ref[...], woa_ref[...], preferred_element_type=jnp.float32)
         + jnp.dot(ob_ref[...], wob_ref[...], preferred_element_type=jnp.float32))
    x1 = x_ref[...] + o
    h2 = _rms(x1, gf_ref[...]).astype(MXU_DTYPE)
    ffn = jnp.zeros_like(x1)
    for j in range(D_FF // _FF_CHUNK):
        sl = slice(j * _FF_CHUNK, (j + 1) * _FF_CHUNK)
        gate = jnp.dot(h2, wg_ref[:, sl], preferred_element_type=jnp.float32)
        up = jnp.dot(h2, wu_ref[:, sl], preferred_element_type=jnp.float32)
        act = (gate * jax.nn.sigmoid(gate)) * up
        ffn = ffn + jnp.dot(act.astype(MXU_DTYPE), wd_ref[sl, :], preferred_element_type=jnp.float32)
    y_ref[...] = _rms(x1 + ffn, gfin_ref[...])


def _out_call(x2d, oa, ob, woa, wob, g_ffn, wg, wu, wd, g_fin, tm):
    t_tok = x2d.shape[0]
    row = lambda w: pl.BlockSpec((tm, w), lambda i: (i, 0))
    half = N_HEADS_A * V_DIM_A
    return pl.pallas_call(
        _out_kernel, grid=(t_tok // tm,),
        in_specs=[row(D_MODEL), row(half), row(half), _const_spec(woa.shape), _const_spec(wob.shape),
                  _const_spec((1, D_MODEL)), _const_spec(wg.shape), _const_spec(wu.shape), _const_spec(wd.shape),
                  _const_spec((1, D_MODEL))],
        out_specs=row(D_MODEL), out_shape=jax.ShapeDtypeStruct((t_tok, D_MODEL), jnp.float32),
        compiler_params=_cparams(1), name="out",
    )(x2d, oa, ob, woa, wob, g_ffn, wg, wu, wd, g_fin)


def _prep_weights(w_in, w_q_b, w_kv_b, w_o, w_gate, w_up, w_down):
    off = np.concatenate([[0], np.cumsum(IN_SIZES)])
    o_qlat, o_kvlat, o_kr, o_qb, o_kb, o_vb, o_iq, o_ik, o_iw = off[:9]
    zeros = lambda n: jnp.zeros((D_MODEL, n), w_in.dtype)
    cols = [w_in[:, o_qlat:o_qlat + Q_LORA], w_in[:, o_kvlat:o_kvlat + KV_LORA],
            w_in[:, o_qb:o_qb + N_HEADS_B * HEAD_DIM_B], w_in[:, o_iq:o_iq + N_IDX_HEADS * IDX_DIM]]
    cols += [w_in[:, o_kb:o_kb + LANES], w_in[:, o_vb:o_vb + LANES],
             w_in[:, o_ik:o_ik + IDX_DIM], zeros(LANES - IDX_DIM),
             zeros(NOPE_DIM), w_in[:, o_kr:o_kr + ROPE_DIM], zeros(LANES - NOPE_DIM - ROPE_DIM),
             w_in[:, o_iw:o_iw + N_IDX_HEADS], zeros(LANES - N_IDX_HEADS)]
    w_in_p = jnp.concatenate(cols, axis=1).astype(MXU_DTYPE)
    assert w_in_p.shape[1] == _C_END

    qd = NOPE_DIM + ROPE_DIM
    w_qb_p = jnp.pad(w_q_b.reshape(Q_LORA, N_HEADS_A, qd), ((0, 0), (0, 0), (0, LANES - qd)))
    w_qb_p = w_qb_p.reshape(Q_LORA, N_HEADS_A * LANES).astype(MXU_DTYPE)

    kv = w_kv_b.reshape(KV_LORA, N_HEADS_A, NOPE_DIM + V_DIM_A)
    wk = jnp.pad(kv[:, :, :NOPE_DIM], ((0, 0), (0, 0), (0, LANES - NOPE_DIM)))
    wk = wk.reshape(KV_LORA, N_HEADS_A * LANES).astype(MXU_DTYPE)
    wvt = kv[:, :, NOPE_DIM:].reshape(KV_LORA, N_HEADS_A * V_DIM_A).T.astype(MXU_DTYPE)

    half = N_HEADS_A * V_DIM_A
    return (w_in_p, w_qb_p, wk, wvt, w_o[:half].astype(MXU_DTYPE), w_o[half:].astype(MXU_DTYPE),
            w_gate.astype(MXU_DTYPE), w_up.astype(MXU_DTYPE), w_down.astype(MXU_DTYPE))


def _rope_tables(pos):
    posf = pos.astype(jnp.float32)[:, None]
    n = pos.shape[0]

    def cs(d):
        inv = jnp.power(ROPE_THETA, -jnp.arange(0, d, 2, dtype=jnp.float32) / d)
        ang = posf * inv[None, :]
        return jnp.cos(ang), jnp.sin(ang)

    one, zero = (lambda w: jnp.ones((n, w), jnp.float32)), (lambda w: jnp.zeros((n, w), jnp.float32))
    c, s = cs(ROPE_DIM)
    cq = jnp.concatenate([one(NOPE_DIM), c, c, one(LANES - NOPE_DIM - ROPE_DIM)], axis=1)
    sq = jnp.concatenate([zero(NOPE_DIM), -s, s, zero(LANES - NOPE_DIM - ROPE_DIM)], axis=1)
    c, s = cs(ROT_DIM_B)
    c64 = jnp.tile(jnp.concatenate([c, c, one(HEAD_DIM_B - ROT_DIM_B)], axis=1), (1, LANES // HEAD_DIM_B))
    s64 = jnp.tile(jnp.concatenate([-s, s, zero(HEAD_DIM_B - ROT_DIM_B)], axis=1), (1, LANES // HEAD_DIM_B))
    return cq, sq, c64, s64


def _pick_tile(n, pref):
    t = min(pref, n)
    while n % t:
        t //= 2
    return t


def _round_up(n, m):
    return -(-n // m) * m


def _layer(x, pos0, past, weights, norms, final_norm):
    b, l, _ = x.shape
    g_attn, g_q, g_kv, g_ffn = norms
    w_in_p, w_qb_p, wk, wvt, woa, wob, wg, wu, wd = weights
    t_tok = b * l
    x2d = x.reshape(t_tok, D_MODEL)

    tabs = _rope_tables(pos0 + jnp.arange(l))
    if l >= TM // 2:
        tm = _pick_tile(l, TM)
        n_tab_tiles = l // tm
    else:
        tm = _pick_tile(t_tok, TM)
        tabs = tuple(jnp.tile(t, (tm // l, 1)) for t in tabs)
        n_tab_tiles = 1
    (ckv, kr128, kb, vb, ik64, iw, q_mla, q_b, iq, kb16, vb16, ik16, kr32) = _proj_call(
        x2d, tabs, n_tab_tiles, tm, g_attn, w_in_p, g_q, w_qb_p, g_kv)

    def per_head(a):
        halves = [a[:, j * HEAD_DIM_B:(j + 1) * HEAD_DIM_B].reshape(b, l, 1, HEAD_DIM_B) for j in range(N_KV_B)]
        return jnp.concatenate(halves, axis=2)

    new = (ckv.reshape(b, l, KV_LORA), kr32.reshape(b, l, ROPE_DIM), per_head(kb), per_head(vb),
           ik64.reshape(b, l, IDX_DIM))

    per_row = lambda a: a.reshape(b, l, a.shape[-1])
    p_len = 0 if past is None else past[0].shape[1]
    s_len = p_len + l
    lq = _round_up(l, LANES)
    tq_mla, tq_dsa = _pick_tile(lq, TQ_MLA), _pick_tile(lq, TQ_DSA)
    tk_mla, tk_dsa = (2 * TK_MLA, 2 * TK_DSA) if lq < TQ_DSA else (TK_MLA, TK_DSA)
    s_pad = _round_up(s_len, max(tk_mla, tk_dsa))
    if past is None:
        old = [None] * 5
    else:
        p_lat, p_kr, p_k, p_v, p_ik = past
        old = [p_lat,
               jnp.pad(p_kr, ((0, 0), (0, 0), (NOPE_DIM, LANES - NOPE_DIM - ROPE_DIM))),
               p_k.reshape(b, p_len, LANES).astype(MXU_DTYPE), p_v.reshape(b, p_len, LANES).astype(MXU_DTYPE),
               jnp.pad(p_ik, ((0, 0), (0, 0), (0, LANES - IDX_DIM))).astype(MXU_DTYPE)]

    def full_seq(prev, cur):
        parts = ([] if prev is None else [prev]) + [per_row(cur)]
        if s_pad > s_len:
            parts.append(jnp.zeros((b, s_pad - s_len, cur.shape[-1]), cur.dtype))
        return parts[0] if len(parts) == 1 else jnp.concatenate(parts, axis=1)

    c_all, kr_all, kb_all, vb_all, ik_all = (full_seq(o, c) for o, c in zip(old, (ckv, kr128, kb16, vb16, ik16)))
    vbt_all = vb_all.reshape(b, s_pad // tk_dsa, tk_dsa, LANES).transpose(0, 1, 3, 2)

    k_mla, vt_mla = _kvup_call(c_all, kr_all, wk, wvt, tk_mla)

    padq = lambda a: per_row(a) if lq == l else jnp.pad(per_row(a), ((0, 0), (0, lq - l), (0, 0)))
    o_a = _mla_call(padq(q_mla), k_mla, vt_mla, tq=tq_mla, tk=tk_mla, q_off=pos0, s_len=s_len)
    tri = jnp.tril(jnp.ones((tk_dsa, tk_dsa), MXU_DTYPE))
    o_b = _dsa_call(padq(iq), padq(q_b), padq(iw), ik_all, kb_all, vbt_all, tri,
                    tq=tq_dsa, tk=tk_dsa, q_off=pos0, s_len=s_len)
    o_a, o_b = (o[:, :l].reshape(t_tok, -1) for o in (o_a, o_b))

    y = _out_call(x2d, o_a, o_b, woa, wob, g_ffn, wg, wu, wd, final_norm, tm)
    return y.reshape(b, l, D_MODEL), new


def kernel(x_prompt, x_sample, cache_mla_latent, cache_mla_krope, cache_dsa_k, cache_dsa_v, cache_idx_k, attn_norm, w_in, q_a_norm, w_q_b, kv_a_norm, w_kv_b, w_o, ffn_norm, w_gate, w_up, w_down, final_norm):
    depth = w_in.shape[0]
    assert depth == 1, "the fused final norm assumes a single layer"
    past_len = cache_mla_latent.shape[2]
    li = 0
    weights = _prep_weights(w_in[li], w_q_b[li], w_kv_b[li], w_o[li], w_gate[li], w_up[li], w_down[li])
    norms = (attn_norm[li][None, :], q_a_norm[li][None, :], kv_a_norm[li][None, :], ffn_norm[li][None, :])
    fin = final_norm[None, :]
    past = (cache_mla_latent[li], cache_mla_krope[li], cache_dsa_k[li], cache_dsa_v[li], cache_idx_k[li])
    y_p, new_p = _layer(x_prompt, 0, None, weights, norms, fin)
    y_s, new_s = _layer(x_sample, past_len, past, weights, norms, fin)
    return (y_p, y_s) + tuple(a[None] for a in new_p) + tuple(a[None] for a in new_s)
```

```python
import functools

import numpy as np
import jax
import jax.numpy as jnp
from jax import lax
from jax.experimental import pallas as pl
from jax.experimental.pallas import tpu as pltpu

D_MODEL = 1024
CHUNK = 64
ROPE_THETA = 500000.0
EPS = 1e-6
N_HEADS_A = 8
NOPE_DIM = 64
ROPE_DIM = 32
V_DIM_A = 64
Q_LORA = 384
KV_LORA = 256
N_HEADS_B = 8
N_KV_B = 2
HEAD_DIM_B = 64
ROT_DIM_B = 16
N_IDX_HEADS = 8
IDX_DIM = 64
ROT_DIM_IDX = 16
TOPK_MAX = 256
D_FF = 2816
IN_SIZES = (Q_LORA, KV_LORA, ROPE_DIM, N_HEADS_B * HEAD_DIM_B, N_KV_B * HEAD_DIM_B, N_KV_B * HEAD_DIM_B,
            N_IDX_HEADS * IDX_DIM, IDX_DIM, N_IDX_HEADS)

LANES = 128
SUBLANES = 8
MXU_DTYPE = jnp.bfloat16
VMEM_LIMIT = 60 * 1024 * 1024
NEG_BIG = -1e30
LOWEST = -3.0e38
FLT_MIN = 1.1754943508222875e-38
LOG2E = 1.4426950408889634

TM = 512
TQ_MLA = 512
TK_MLA = 512
TQ_DSA = 256
TK_DSA = 512
MLA_HEADS_PER_STEP = 4

_C_QLAT = 0
_C_KVLAT = _C_QLAT + Q_LORA
_C_QB = _C_KVLAT + KV_LORA
_C_IQ = _C_QB + N_HEADS_B * HEAD_DIM_B
_C_KB = _C_IQ + N_IDX_HEADS * IDX_DIM
_C_VB = _C_KB + LANES
_C_IK = _C_VB + LANES
_C_KR = _C_IK + LANES
_C_IW = _C_KR + LANES
_C_END = _C_IW + LANES


def _cparams(n_axes):
    return pltpu.CompilerParams(dimension_semantics=("arbitrary",) * n_axes, vmem_limit_bytes=VMEM_LIMIT)


def _const_spec(shape):
    return pl.BlockSpec(shape, lambda *_: (0,) * len(shape), pipeline_mode=pl.Buffered(1))


def _rms(x, g):
    return (x * lax.rsqrt(jnp.mean(x * x, axis=-1, keepdims=True) + EPS)) * g


def _rope_tile(x, cos, sin, half, first_half):
    partner = jnp.where(first_half, pltpu.roll(x, LANES - half, 1), pltpu.roll(x, half, 1))
    return x * cos + partner * sin


def _kv_up(c, kr, wk_ref, wvt_ref, k_ref, vt_ref):
    cb = c.astype(MXU_DTYPE)
    kn = jnp.dot(cb, wk_ref[...], preferred_element_type=jnp.float32)
    for h in range(N_HEADS_A):
        k_ref[:, h * LANES:(h + 1) * LANES] = (kn[:, h * LANES:(h + 1) * LANES] + kr).astype(k_ref.dtype)
    vt_ref[...] = lax.dot_general(wvt_ref[...], cb, _NT, preferred_element_type=jnp.float32).astype(vt_ref.dtype)


def _proj_kernel(*refs, fuse_kv):
    n_in = 12 if fuse_kv else 10
    x_ref, cq_ref, sq_ref, c64_ref, s64_ref, gat_ref, win_ref, gq_ref, wqb_ref, gkv_ref = refs[:10]
    (ckv_ref, kr_ref, kb_ref, vb_ref, ik_ref, iw_ref, qm_ref, qb_ref, iq_ref,
     kb16_ref, vb16_ref, ik16_ref, kr32_ref) = refs[n_in:n_in + 13]
    x = x_ref[...]
    hb = _rms(x, gat_ref[...]).astype(MXU_DTYPE)
    y = jnp.dot(hb, win_ref[...], preferred_element_type=jnp.float32)

    lane = lax.broadcasted_iota(jnp.int32, (1, LANES), 1)
    first_q = lane < NOPE_DIM + ROPE_DIM // 2
    first_64 = (lane & (HEAD_DIM_B - 1)) < ROT_DIM_B // 2
    cq, sq, c64, s64 = cq_ref[...], sq_ref[...], c64_ref[...], s64_ref[...]

    qn = _rms(y[:, _C_QLAT:_C_QLAT + Q_LORA], gq_ref[...]).astype(MXU_DTYPE)
    qa = jnp.dot(qn, wqb_ref[...], preferred_element_type=jnp.float32)
    scale_a = (NOPE_DIM + ROPE_DIM) ** -0.5 * LOG2E
    for h in range(N_HEADS_A):
        t = _rope_tile(qa[:, h * LANES:(h + 1) * LANES], cq, sq, ROPE_DIM // 2, first_q)
        qm_ref[:, h * LANES:(h + 1) * LANES] = (t * scale_a).astype(qm_ref.dtype)

    ckv = _rms(y[:, _C_KVLAT:_C_KVLAT + KV_LORA], gkv_ref[...])
    ckv_ref[...] = ckv
    kr = _rope_tile(y[:, _C_KR:_C_KR + LANES], cq, sq, ROPE_DIM // 2, first_q)
    if fuse_kv:
        _kv_up(ckv, kr, refs[10], refs[11], refs[n_in + 13], refs[n_in + 14])
    kr_ref[...] = kr
    kr32_ref[...] = pltpu.roll(kr, LANES - NOPE_DIM, 1)[:, :ROPE_DIM]

    scale_b = HEAD_DIM_B ** -0.5 * LOG2E
    low = lane < HEAD_DIM_B
    heads_per_kv = N_HEADS_B // N_KV_B
    for t in range(N_HEADS_B // 2):
        pair = _rope_tile(y[:, _C_QB + t * LANES:_C_QB + (t + 1) * LANES], c64, s64, ROT_DIM_B // 2, first_64)
        pair = pair * scale_b
        swapped = pltpu.roll(pair, HEAD_DIM_B, 1)
        for u in range(2):
            h = 2 * t + u
            half = h // heads_per_kv
            src = pair if u == half else swapped
            tile = jnp.where(low if half == 0 else jnp.logical_not(low), src, 0.0)
            qb_ref[:, h * LANES:(h + 1) * LANES] = tile.astype(qb_ref.dtype)
    for t in range(N_IDX_HEADS // 2):
        pair = _rope_tile(y[:, _C_IQ + t * LANES:_C_IQ + (t + 1) * LANES], c64, s64, ROT_DIM_IDX // 2, first_64)
        swapped = pltpu.roll(pair, IDX_DIM, 1)
        for u in range(2):
            h = 2 * t + u
            iq_ref[:, h * LANES:(h + 1) * LANES] = jnp.where(low, pair if u == 0 else swapped,
                                                             0.0).astype(iq_ref.dtype)

    kb = _rope_tile(y[:, _C_KB:_C_KB + LANES], c64, s64, ROT_DIM_B // 2, first_64)
    kb_ref[...] = kb
    kb16_ref[...] = kb.astype(kb16_ref.dtype)
    vb = y[:, _C_VB:_C_VB + LANES]
    vb_ref[...] = vb
    vb16_ref[...] = vb.astype(vb16_ref.dtype)
    ik = _rope_tile(y[:, _C_IK:_C_IK + LANES], c64, s64, ROT_DIM_IDX // 2, first_64)
    ik_ref[...] = ik[:, :IDX_DIM]
    ik16_ref[...] = ik.astype(ik16_ref.dtype)
    iw_ref[...] = y[:, _C_IW:_C_IW + LANES] * ((N_IDX_HEADS * IDX_DIM) ** -0.5)


def _proj_call(x2d, tabs, n_tab_tiles, tm, g_attn, w_in_p, g_q, w_qb_p, g_kv, kv_weights=None):
    t_tok = x2d.shape[0]
    grid = (t_tok // tm,)
    row = lambda w: pl.BlockSpec((tm, w), lambda i: (i, 0))
    tab = pl.BlockSpec((tm, LANES), lambda i: (i % n_tab_tiles, 0))
    f32, b16 = jnp.float32, MXU_DTYPE
    out_shapes = [
        jax.ShapeDtypeStruct((t_tok, KV_LORA), f32),
        jax.ShapeDtypeStruct((t_tok, LANES), f32),
        jax.ShapeDtypeStruct((t_tok, LANES), f32),
        jax.ShapeDtypeStruct((t_tok, LANES), f32),
        jax.ShapeDtypeStruct((t_tok, IDX_DIM), f32),
        jax.ShapeDtypeStruct((t_tok, LANES), f32),
        jax.ShapeDtypeStruct((t_tok, N_HEADS_A * LANES), b16),
        jax.ShapeDtypeStruct((t_tok, N_HEADS_B * LANES), b16),
        jax.ShapeDtypeStruct((t_tok, N_IDX_HEADS * LANES), b16),
        jax.ShapeDtypeStruct((t_tok, LANES), b16),
        jax.ShapeDtypeStruct((t_tok, LANES), b16),
        jax.ShapeDtypeStruct((t_tok, LANES), b16),
        jax.ShapeDtypeStruct((t_tok, ROPE_DIM), f32),
    ]
    out_specs = [row(s.shape[1]) for s in out_shapes]
    in_specs = [row(D_MODEL), tab, tab, tab, tab, _const_spec((1, D_MODEL)), _const_spec(w_in_p.shape),
                _const_spec((1, Q_LORA)), _const_spec(w_qb_p.shape), _const_spec((1, KV_LORA))]
    operands = [x2d, *tabs, g_attn, w_in_p, g_q, w_qb_p, g_kv]
    if kv_weights is not None:
        vdim = N_HEADS_A * V_DIM_A
        operands += list(kv_weights)
        in_specs += [_const_spec(w.shape) for w in kv_weights]
        out_shapes += [jax.ShapeDtypeStruct((t_tok, N_HEADS_A * LANES), b16),
                       jax.ShapeDtypeStruct((t_tok // (n_tab_tiles * tm), n_tab_tiles, vdim, tm), b16)]
        out_specs += [row(N_HEADS_A * LANES),
                      pl.BlockSpec((None, None, vdim, tm), lambda i: (i // n_tab_tiles, i % n_tab_tiles, 0, 0))]
    return pl.pallas_call(
        functools.partial(_proj_kernel, fuse_kv=kv_weights is not None), grid=grid, in_specs=in_specs,
        out_specs=out_specs, out_shape=out_shapes, compiler_params=_cparams(1), name="proj",
    )(*operands)


_NT = (((1,), (1,)), ((), ()))


def _kvup_kernel(c_ref, kr_ref, wk_ref, wvt_ref, k_ref, vt_ref):
    _kv_up(c_ref[...], kr_ref[...], wk_ref, wvt_ref, k_ref, vt_ref)


def _kvup_call(c3d, kr3d, wk, wvt, tk):
    b, s_pad, _ = c3d.shape
    row = lambda w: pl.BlockSpec((None, tk, w), lambda bi, i: (bi, i, 0))
    vdim = N_HEADS_A * V_DIM_A
    return pl.pallas_call(
        _kvup_kernel, grid=(b, s_pad // tk),
        in_specs=[row(KV_LORA), row(LANES), _const_spec(wk.shape), _const_spec(wvt.shape)],
        out_specs=[row(N_HEADS_A * LANES), pl.BlockSpec((None, None, vdim, tk), lambda bi, i: (bi, i, 0, 0))],
        out_shape=[jax.ShapeDtypeStruct((b, s_pad, N_HEADS_A * LANES), MXU_DTYPE),
                   jax.ShapeDtypeStruct((b, s_pad // tk, vdim, tk), MXU_DTYPE)],
        compiler_params=_cparams(2), name="kvup",
    )(c3d, kr3d, wk, wvt)


_CHUNK_SHIFT = CHUNK.bit_length() - 1


def _admissible_t(p0, kb, tq, tk, s_len):
    kpos = kb * tk + lax.broadcasted_iota(jnp.int32, (tk, 1), 0)
    qpos = p0 + lax.broadcasted_iota(jnp.int32, (1, tq), 1)
    return ((qpos >> _CHUNK_SHIFT) >= (kpos >> _CHUNK_SHIFT)) & (kpos < s_len)


def _check_tiles(tq, tk, q_off, lq):
    assert tk & (tk - 1) == 0 and tk % tq == 0 and tk % CHUNK == 0, (tq, tk)
    assert q_off % tq == 0 or (lq == tq and q_off % tk + tq <= tk), (tq, tk, q_off, lq)


def _last_block(p0, tq, tk, s_len):
    return jnp.minimum(p0 + tq - 1, s_len - 1) >> (tk.bit_length() - 1)


def _fori_grouped(n, body, carry, group):
    shift = group.bit_length() - 1

    def several(j, c):
        for i in range(group):
            c = body(group * j + i, c)
        return c

    carry = lax.fori_loop(0, n >> shift, several, carry)
    return lax.fori_loop((n >> shift) << shift, n, body, carry)


def _softmax_init(n_heads, tq, acc_ref):
    acc_ref[...] = jnp.zeros(acc_ref.shape, jnp.float32)
    return jnp.full((n_heads, tq), NEG_BIG, jnp.float32), jnp.zeros((n_heads, tq), jnp.float32)


def _attn_step(state, values, next_scores, bias, s_ref, acc_ref):
    (m_prev, l_prev), maxima = state
    n_heads = s_ref.shape[0]
    sums, next_max = [], []
    if values is not None:
        m_new = jnp.maximum(m_prev, maxima)
        alpha = jnp.exp2(m_prev - m_new)
    for h in range(n_heads):
        if values is not None:
            p = jnp.exp2(s_ref[h] - m_new[h:h + 1, :])
            sums.append(jnp.sum(p, axis=0, keepdims=True))
            acc_ref[h] = alpha[h:h + 1, :] * acc_ref[h] + jnp.dot(values[h], p.astype(values[h].dtype),
                                                                   preferred_element_type=jnp.float32)
        if next_scores is not None:
            s = next_scores(h)
            if bias is not None:
                s = s + bias
            s_ref[h] = s
            next_max.append(jnp.max(s, axis=0, keepdims=True))
    carry = (m_new, alpha * l_prev + jnp.concatenate(sums, axis=0)) if values is not None else (m_prev, l_prev)
    return carry, (jnp.concatenate(next_max, axis=0) if next_scores is not None else maxima)


def _softmax_finish(l, acc_ref, o_ref):
    inv = 1.0 / l
    o_t = jnp.concatenate([acc_ref[h] * inv[h:h + 1, :] for h in range(l.shape[0])], axis=0)
    o_ref[...] = o_t.T.astype(o_ref.dtype)


def _mla_kernel(q_ref, k_ref, vt_ref, o_ref, s_ref, acc_ref, *, tq, tk, q_off, s_len):
    p0 = q_off + pl.program_id(2) * tq
    last = _last_block(p0, tq, tk, s_len)
    heads = range(MLA_HEADS_PER_STEP)

    def qk(kb):
        start = pl.multiple_of(kb * tk, tk)
        return lambda h: lax.dot_general(k_ref[pl.ds(start, tk), h * LANES:(h + 1) * LANES],
                                         q_ref[:, h * LANES:(h + 1) * LANES], _NT,
                                         preferred_element_type=jnp.float32)

    def mask_bias(kb):
        return jnp.where(_admissible_t(p0, kb, tq, tk, s_len), 0.0, NEG_BIG)

    def values(kb):
        vt_all = vt_ref[kb]
        return [vt_all[h * V_DIM_A:(h + 1) * V_DIM_A, :] for h in heads]

    def body(kb, state):
        return _attn_step(state, values(kb), qk(kb + 1), None, s_ref, acc_ref)

    def tail(state):
        return _attn_step(state, values(last - 1), qk(last), mask_bias(last), s_ref, acc_ref)

    state = _attn_step((_softmax_init(MLA_HEADS_PER_STEP, tq, acc_ref), None), None, qk(0), mask_bias(0), s_ref,
                       acc_ref)
    state = lax.fori_loop(0, jnp.maximum(last - 1, 0), body, state)
    state = lax.cond(last > 0, tail, lambda st: st, state)
    (_, l), _ = _attn_step(state, values(last), None, None, s_ref, acc_ref)
    _softmax_finish(l, acc_ref, o_ref)


def _mla_call(q, k, vt, *, tq, tk, q_off, s_len):
    b, lq, _ = q.shape
    _check_tiles(tq, tk, q_off, lq)
    s_pad = k.shape[1]
    hg = MLA_HEADS_PER_STEP
    kern = functools.partial(_mla_kernel, tq=tq, tk=tk, q_off=q_off, s_len=s_len)
    return pl.pallas_call(
        kern, grid=(b, N_HEADS_A // hg, lq // tq),
        in_specs=[pl.BlockSpec((None, tq, hg * LANES), lambda bi, g, qi: (bi, qi, g)),
                  pl.BlockSpec((None, s_pad, hg * LANES), lambda bi, g, qi: (bi, 0, g)),
                  pl.BlockSpec((None, s_pad // tk, hg * V_DIM_A, tk), lambda bi, g, qi: (bi, 0, g, 0))],
        out_specs=pl.BlockSpec((None, tq, hg * V_DIM_A), lambda bi, g, qi: (bi, qi, g)),
        out_shape=jax.ShapeDtypeStruct((b, lq, N_HEADS_A * V_DIM_A), MXU_DTYPE),
        scratch_shapes=[pltpu.VMEM((hg, tk, tq), jnp.float32), pltpu.VMEM((hg, V_DIM_A, tq), jnp.float32)],
        compiler_params=_cparams(3), name="mla",
    )(q, k, vt)


def _f2key(x):
    b = lax.bitcast_convert_type(x, jnp.int32)
    return jnp.where(b < 0, b ^ jnp.int32(0x7FFFFFFF), b)


def _key2f(k):
    return lax.bitcast_convert_type(jnp.where(k < 0, k ^ jnp.int32(0x7FFFFFFF), k), jnp.float32)


_COUNT_SLAB = 4 * SUBLANES
_SCORE_GROUP = 4
_COUNT_GROUP = 4
_SEARCH_BISECT_EVERY = 8
_SEARCH_MAX_PROBES = 40 * _SEARCH_BISECT_EVERY


def _dsa_kernel(iq_ref, qb_ref, iw_ref, ik_ref, kb_ref, vbt_ref, tri_ref, o_ref,
                sc_ref, wt_ref, s_ref, acc_ref, *, tq, tk, q_off, s_len, topk):
    p0 = q_off + pl.program_id(1) * tq
    last = _last_block(p0, tq, tk, s_len)
    n_blk = last + 1
    row = lambda v: jnp.full((1, tq), v, jnp.float32)
    colsum = lambda x: jnp.sum(x, axis=0, keepdims=True)

    wt_ref[...] = iw_ref[...].T[:SUBLANES, :]

    def score_block(kb):
        start = pl.multiple_of(kb * tk, tk)
        ikb = ik_ref[pl.ds(start, tk), :]
        acc = jnp.zeros((tk, tq), jnp.float32)
        for h in range(N_IDX_HEADS):
            x = lax.dot_general(ikb, iq_ref[:, h * LANES:(h + 1) * LANES], _NT, preferred_element_type=jnp.float32)
            acc = acc + wt_ref[h:h + 1, :] * jnp.maximum(x, 0.0)
        return acc

    def stats(carry, hi_fill, lo_fill):
        mx, mn, n_pos, n_nonneg = carry
        return (jnp.maximum(mx, jnp.max(hi_fill, axis=0, keepdims=True)),
                jnp.minimum(mn, jnp.min(lo_fill, axis=0, keepdims=True)),
                n_pos + colsum(jnp.where(hi_fill >= FLT_MIN, 1.0, 0.0)),
                n_nonneg + colsum(jnp.where(hi_fill >= 0.0, 1.0, 0.0)))

    def p1_body(kb, carry):
        acc = score_block(kb)
        sc_ref[kb] = acc
        return stats(carry, acc, acc)

    carry = _fori_grouped(last, p1_body, (row(-jnp.inf), row(jnp.inf), row(0.0), row(0.0)), _SCORE_GROUP)
    acc = score_block(last)
    adm = _admissible_t(p0, last, tq, tk, s_len)
    masked = jnp.where(adm, acc, -jnp.inf)
    sc_ref[last] = masked
    mx, mn, n_pos, n_nonneg = stats(carry, masked, jnp.where(adm, acc, jnp.inf))

    qpos = p0 + lax.broadcasted_iota(jnp.int32, (1, tq), 1)
    n_adm = jnp.minimum(((qpos >> _CHUNK_SHIFT) + 1) * CHUNK, s_len).astype(jnp.float32)
    k_f = jnp.float32(topk)
    log_k = jnp.log(k_f)
    phi = lambda c: jnp.log(jnp.maximum(c, 0.25)) - log_k

    few = n_adm <= k_f
    pos = n_pos >= k_f
    at_zero = jnp.logical_and(~pos, n_nonneg >= k_f)
    above_max = _key2f(_f2key(jnp.maximum(mx, LOWEST)) + 1)
    lo = jnp.where(pos, FLT_MIN, LOWEST)
    c_lo = jnp.where(pos, n_pos, n_adm)
    hi = jnp.where(pos | at_zero, above_max, 0.0)
    c_hi = jnp.where(pos | at_zero, 0.0, n_nonneg)
    lo_val = jnp.where(pos, 0.0, jnp.maximum(mn, LOWEST))
    tau = jnp.where(few, LOWEST, jnp.where(at_zero, 0.0, lo))
    r = jnp.where(at_zero & ~few, k_f - n_pos, 0.0)
    done = jnp.where(few | at_zero, 1.0, 0.0)

    def count_ge(g):
        def body(kb, cnt):
            hit = jnp.where(sc_ref[kb] >= g, 1.0, 0.0)
            return cnt + jnp.sum(hit.reshape(tk // _COUNT_SLAB, _COUNT_SLAB, tq), axis=0)

        return colsum(_fori_grouped(n_blk, body, jnp.zeros((_COUNT_SLAB, tq), jnp.float32), _COUNT_GROUP))

    def search_cond(st):
        return jnp.logical_and(st[1] > 0, st[0] < _SEARCH_MAX_PROBES)

    def next_probe(it, lo, hi, f_lo, f_hi, lo_val):
        lok, hik = _f2key(lo), _f2key(hi)
        midk = (lok >> 1) + (hik >> 1) + (lok & hik & 1)
        frac = f_lo / jnp.maximum(f_lo - f_hi, 1e-9)
        g_int = lo_val + (hi - lo_val) * frac
        bisect = (it % _SEARCH_BISECT_EVERY) == (_SEARCH_BISECT_EVERY - 1)
        g = jnp.where(bisect, _key2f(midk), g_int)
        gk = jnp.minimum(jnp.maximum(_f2key(g), lok + 1), jnp.maximum(hik - 1, lok + 1))
        return _key2f(gk)

    def search_body(st):
        it, _, g, lo, hi, c_lo, c_hi, f_lo, f_hi, lo_val, side, tau, r, done = st
        c = count_ge(g)
        f_g = phi(c)
        live = done == 0.0
        up = live & (c >= k_f)
        dn = live & (c < k_f)
        f_hi = jnp.where(up & (side > 0.0), f_hi * 0.5, f_hi)
        f_lo = jnp.where(dn & (side < 0.0), f_lo * 0.5, f_lo)
        lo, c_lo, f_lo, lo_val = (jnp.where(up, g, lo), jnp.where(up, c, c_lo), jnp.where(up, f_g, f_lo),
                                  jnp.where(up, g, lo_val))
        hi, c_hi, f_hi = jnp.where(dn, g, hi), jnp.where(dn, c, c_hi), jnp.where(dn, f_g, f_hi)
        side = jnp.where(up, 1.0, jnp.where(dn, -1.0, side))
        adjacent = (_f2key(lo) + 1) >= _f2key(hi)
        fin = live & ((c_lo == k_f) | adjacent)
        tau = jnp.where(fin, lo, tau)
        r = jnp.where(fin & (c_lo > k_f), k_f - c_hi, r)
        done = jnp.where(fin, 1.0, done)
        pending = jnp.sum(1.0 - done).astype(jnp.int32)
        g = next_probe(it + 1, lo, hi, f_lo, f_hi, lo_val)
        return (it + 1, pending, g, lo, hi, c_lo, c_hi, f_lo, f_hi, lo_val, side, tau, r, done)

    pending0 = jnp.sum(1.0 - done).astype(jnp.int32)
    f_lo0, f_hi0 = phi(c_lo), phi(c_hi)
    st = lax.while_loop(search_cond, search_body,
                        (jnp.int32(0), pending0, next_probe(0, lo, hi, f_lo0, f_hi0, lo_val), lo, hi, c_lo, c_hi,
                         f_lo0, f_hi0, lo_val, row(0.0), tau, r, done))
    tau, r = st[11], st[12]

    @pl.when(jnp.sum(r) > 0.0)
    def _():
        tied = r > 0.0
        budget = jnp.where(tied, r, jnp.inf)

        def body(kb, seen):
            blk = sc_ref[kb]
            eq = jnp.logical_and(blk == tau, tied)
            eq_f = jnp.where(eq, 1.0, 0.0)
            rank = jnp.dot(tri_ref[...], eq_f.astype(MXU_DTYPE), preferred_element_type=jnp.float32) + seen
            sc_ref[kb] = jnp.where(jnp.logical_and(eq, rank > budget), -jnp.inf, blk)
            return seen + colsum(eq_f)

        _fori_grouped(n_blk, body, row(0.0), _COUNT_GROUP)

    heads_per_kv = N_HEADS_B // N_KV_B

    def qk(kb):
        start = pl.multiple_of(kb * tk, tk)
        kblk = kb_ref[pl.ds(start, tk), :]
        return lambda h: lax.dot_general(kblk, qb_ref[:, h * LANES:(h + 1) * LANES], _NT,
                                         preferred_element_type=jnp.float32)

    def sel_bias(kb):
        return jnp.where(sc_ref[kb] >= tau, 0.0, NEG_BIG)

    def values(kb):
        vt_all = vbt_ref[kb]
        return [vt_all[(h // heads_per_kv) * HEAD_DIM_B:(h // heads_per_kv + 1) * HEAD_DIM_B, :]
                for h in range(N_HEADS_B)]

    def p3_body(kb, state):
        return _attn_step(state, values(kb), qk(kb + 1), sel_bias(kb + 1), s_ref, acc_ref)

    state = _attn_step((_softmax_init(N_HEADS_B, tq, acc_ref), None), None, qk(0), sel_bias(0), s_ref, acc_ref)
    state = lax.fori_loop(0, last, p3_body, state)
    (_, l), _ = _attn_step(state, values(last), None, None, s_ref, acc_ref)
    _softmax_finish(l, acc_ref, o_ref)


def _dsa_call(iq, qb, iw, ik, kb, vbt, tri, *, tq, tk, q_off, s_len):
    b, lq, _ = iq.shape
    _check_tiles(tq, tk, q_off, lq)
    s_pad = ik.shape[1]
    n_kb = s_pad // tk
    topk = min(TOPK_MAX, s_len // 4)
    kern = functools.partial(_dsa_kernel, tq=tq, tk=tk, q_off=q_off, s_len=s_len, topk=topk)
    qspec = lambda w: pl.BlockSpec((None, tq, w), lambda bi, qi: (bi, qi, 0))
    kspec = pl.BlockSpec((None, s_pad, LANES), lambda bi, qi: (bi, 0, 0))
    return pl.pallas_call(
        kern, grid=(b, lq // tq),
        in_specs=[qspec(N_IDX_HEADS * LANES), qspec(N_HEADS_B * LANES), qspec(LANES), kspec, kspec,
                  pl.BlockSpec((None, n_kb, LANES, tk), lambda bi, qi: (bi, 0, 0, 0)),
                  _const_spec(tri.shape)],
        out_specs=qspec(N_HEADS_B * HEAD_DIM_B),
        out_shape=jax.ShapeDtypeStruct((b, lq, N_HEADS_B * HEAD_DIM_B), MXU_DTYPE),
        scratch_shapes=[pltpu.VMEM((n_kb, tk, tq), jnp.float32), pltpu.VMEM((SUBLANES, tq), jnp.float32),
                        pltpu.VMEM((N_HEADS_B, tk, tq), jnp.float32),
                        pltpu.VMEM((N_HEADS_B, HEAD_DIM_B, tq), jnp.float32)],
        compiler_params=_cparams(2), name="dsa",
    )(iq, qb, iw, ik, kb, vbt, tri)


_FF_CHUNK = 256


def _out_kernel(x_ref, oa_ref, ob_ref, woa_ref, wob_ref, gf_ref, wg_ref, wu_ref, wd_ref, gfin_ref, y_ref):
    o = (jnp.dot(oa_ref[...], woa_ref[...], preferred_element_type=jnp.float32)
         + jnp.dot(ob_ref[...], wob_ref[...], preferred_element_type=jnp.float32))
    x1 = x_ref[...] + o
    h2 = _rms(x1, gf_ref[...]).astype(MXU_DTYPE)
    ffn = jnp.zeros_like(x1)
    for j in range(D_FF // _FF_CHUNK):
        sl = slice(j * _FF_CHUNK, (j + 1) * _FF_CHUNK)
        gate = jnp.dot(h2, wg_ref[:, sl], preferred_element_type=jnp.float32)
        up = jnp.dot(h2, wu_ref[:, sl], preferred_element_type=jnp.float32)
        act = (gate * jax.nn.sigmoid(gate)) * up
        ffn = ffn + jnp.dot(act.astype(MXU_DTYPE), wd_ref[sl, :], preferred_element_type=jnp.float32)
    y_ref[...] = _rms(x1 + ffn, gfin_ref[...])


def _out_call(x2d, oa, ob, woa, wob, g_ffn, wg, wu, wd, g_fin, tm):
    t_tok = x2d.shape[0]
    row = lambda w: pl.BlockSpec((tm, w), lambda i: (i, 0))
    half = N_HEADS_A * V_DIM_A
    return pl.pallas_call(
        _out_kernel, grid=(t_tok // tm,),
        in_specs=[row(D_MODEL), row(half), row(half), _const_spec(woa.shape), _const_spec(wob.shape),
                  _const_spec((1, D_MODEL)), _const_spec(wg.shape), _const_spec(wu.shape), _const_spec(wd.shape),
                  _const_spec((1, D_MODEL))],
        out_specs=row(D_MODEL), out_shape=jax.ShapeDtypeStruct((t_tok, D_MODEL), jnp.float32),
        compiler_params=_cparams(1), name="out",
    )(x2d, oa, ob, woa, wob, g_ffn, wg, wu, wd, g_fin)


def _prep_weights(w_in, w_q_b, w_kv_b, w_o, w_gate, w_up, w_down):
    off = np.concatenate([[0], np.cumsum(IN_SIZES)])
    o_qlat, o_kvlat, o_kr, o_qb, o_kb, o_vb, o_iq, o_ik, o_iw = off[:9]
    zeros = lambda n: jnp.zeros((D_MODEL, n), w_in.dtype)
    cols = [w_in[:, o_qlat:o_qlat + Q_LORA], w_in[:, o_kvlat:o_kvlat + KV_LORA],
            w_in[:, o_qb:o_qb + N_HEADS_B * HEAD_DIM_B], w_in[:, o_iq:o_iq + N_IDX_HEADS * IDX_DIM]]
    cols += [w_in[:, o_kb:o_kb + LANES], w_in[:, o_vb:o_vb + LANES],
             w_in[:, o_ik:o_ik + IDX_DIM], zeros(LANES - IDX_DIM),
             zeros(NOPE_DIM), w_in[:, o_kr:o_kr + ROPE_DIM], zeros(LANES - NOPE_DIM - ROPE_DIM),
             w_in[:, o_iw:o_iw + N_IDX_HEADS], zeros(LANES - N_IDX_HEADS)]
    w_in_p = jnp.concatenate(cols, axis=1).astype(MXU_DTYPE)
    assert w_in_p.shape[1] == _C_END

    qd = NOPE_DIM + ROPE_DIM
    w_qb_p = jnp.pad(w_q_b.reshape(Q_LORA, N_HEADS_A, qd), ((0, 0), (0, 0), (0, LANES - qd)))
    w_qb_p = w_qb_p.reshape(Q_LORA, N_HEADS_A * LANES).astype(MXU_DTYPE)

    kv = w_kv_b.reshape(KV_LORA, N_HEADS_A, NOPE_DIM + V_DIM_A)
    wk = jnp.pad(kv[:, :, :NOPE_DIM], ((0, 0), (0, 0), (0, LANES - NOPE_DIM)))
    wk = wk.reshape(KV_LORA, N_HEADS_A * LANES).astype(MXU_DTYPE)
    wvt = kv[:, :, NOPE_DIM:].reshape(KV_LORA, N_HEADS_A * V_DIM_A).T.astype(MXU_DTYPE)

    half = N_HEADS_A * V_DIM_A
    return (w_in_p, w_qb_p, wk, wvt, w_o[:half].astype(MXU_DTYPE), w_o[half:].astype(MXU_DTYPE),
            w_gate.astype(MXU_DTYPE), w_up.astype(MXU_DTYPE), w_down.astype(MXU_DTYPE))


def _rope_tables(pos):
    posf = pos.astype(jnp.float32)[:, None]
    n = pos.shape[0]

    def cs(d):
        inv = jnp.power(ROPE_THETA, -jnp.arange(0, d, 2, dtype=jnp.float32) / d)
        ang = posf * inv[None, :]
        return jnp.cos(ang), jnp.sin(ang)

    one, zero = (lambda w: jnp.ones((n, w), jnp.float32)), (lambda w: jnp.zeros((n, w), jnp.float32))
    c, s = cs(ROPE_DIM)
    cq = jnp.concatenate([one(NOPE_DIM), c, c, one(LANES - NOPE_DIM - ROPE_DIM)], axis=1)
    sq = jnp.concatenate([zero(NOPE_DIM), -s, s, zero(LANES - NOPE_DIM - ROPE_DIM)], axis=1)
    c, s = cs(ROT_DIM_B)
    c64 = jnp.tile(jnp.concatenate([c, c, one(HEAD_DIM_B - ROT_DIM_B)], axis=1), (1, LANES // HEAD_DIM_B))
    s64 = jnp.tile(jnp.concatenate([-s, s, zero(HEAD_DIM_B - ROT_DIM_B)], axis=1), (1, LANES // HEAD_DIM_B))
    return cq, sq, c64, s64


def _pick_tile(n, pref):
    t = min(pref, n)
    while n % t:
        t //= 2
    return t


def _round_up(n, m):
    return -(-n // m) * m


def _layer(x, pos0, past, weights, norms, final_norm):
    b, l, _ = x.shape
    g_attn, g_q, g_kv, g_ffn = norms
    w_in_p, w_qb_p, wk, wvt, woa, wob, wg, wu, wd = weights
    t_tok = b * l
    x2d = x.reshape(t_tok, D_MODEL)

    tabs = _rope_tables(pos0 + jnp.arange(l))
    if l >= TM // 2:
        tm = _pick_tile(l, TM)
        n_tab_tiles = l // tm
    else:
        tm = _pick_tile(t_tok, TM)
        tabs = tuple(jnp.tile(t, (tm // l, 1)) for t in tabs)
        n_tab_tiles = 1
    fuse_kv = (past is None and l >= TM // 2 and tm == TK_MLA and l % max(TK_MLA, TK_DSA) == 0
               and _round_up(l, LANES) >= TQ_DSA)
    proj_out = _proj_call(x2d, tabs, n_tab_tiles, tm, g_attn, w_in_p, g_q, w_qb_p, g_kv,
                          (wk, wvt) if fuse_kv else None)
    (ckv, kr128, kb, vb, ik64, iw, q_mla, q_b, iq, kb16, vb16, ik16, kr32) = proj_out[:13]

    def per_head(a):
        halves = [a[:, j * HEAD_DIM_B:(j + 1) * HEAD_DIM_B].reshape(b, l, 1, HEAD_DIM_B) for j in range(N_KV_B)]
        return jnp.concatenate(halves, axis=2)

    new = (ckv.reshape(b, l, KV_LORA), kr32.reshape(b, l, ROPE_DIM), per_head(kb), per_head(vb),
           ik64.reshape(b, l, IDX_DIM))

    per_row = lambda a: a.reshape(b, l, a.shape[-1])
    p_len = 0 if past is None else past[0].shape[1]
    s_len = p_len + l
    lq = _round_up(l, LANES)
    tq_mla, tq_dsa = _pick_tile(lq, TQ_MLA), _pick_tile(lq, TQ_DSA)
    tk_mla, tk_dsa = (2 * TK_MLA, 2 * TK_DSA) if lq < TQ_DSA else (TK_MLA, TK_DSA)
    s_pad = _round_up(s_len, max(tk_mla, tk_dsa))
    if past is None:
        old = [None] * 5
    else:
        p_lat, p_kr, p_k, p_v, p_ik = past
        old = [p_lat,
               jnp.pad(p_kr, ((0, 0), (0, 0), (NOPE_DIM, LANES - NOPE_DIM - ROPE_DIM))),
               p_k.reshape(b, p_len, LANES).astype(MXU_DTYPE), p_v.reshape(b, p_len, LANES).astype(MXU_DTYPE),
               jnp.pad(p_ik, ((0, 0), (0, 0), (0, LANES - IDX_DIM))).astype(MXU_DTYPE)]

    def full_seq(prev, cur):
        parts = ([] if prev is None else [prev]) + [per_row(cur)]
        if s_pad > s_len:
            parts.append(jnp.zeros((b, s_pad - s_len, cur.shape[-1]), cur.dtype))
        return parts[0] if len(parts) == 1 else jnp.concatenate(parts, axis=1)

    c_all, kr_all, kb_all, vb_all, ik_all = (full_seq(o, c) for o, c in zip(old, (ckv, kr128, kb16, vb16, ik16)))
    vbt_all = vb_all.reshape(b, s_pad // tk_dsa, tk_dsa, LANES).transpose(0, 1, 3, 2)

    if fuse_kv:
        k_mla, vt_mla = per_row(proj_out[13]), proj_out[14]
    else:
        k_mla, vt_mla = _kvup_call(c_all, kr_all, wk, wvt, tk_mla)

    padq = lambda a: per_row(a) if lq == l else jnp.pad(per_row(a), ((0, 0), (0, lq - l), (0, 0)))
    o_a = _mla_call(padq(q_mla), k_mla, vt_mla, tq=tq_mla, tk=tk_mla, q_off=pos0, s_len=s_len)
    tri = jnp.tril(jnp.ones((tk_dsa, tk_dsa), MXU_DTYPE))
    o_b = _dsa_call(padq(iq), padq(q_b), padq(iw), ik_all, kb_all, vbt_all, tri,
                    tq=tq_dsa, tk=tk_dsa, q_off=pos0, s_len=s_len)
    o_a, o_b = (o[:, :l].reshape(t_tok, -1) for o in (o_a, o_b))

    y = _out_call(x2d, o_a, o_b, woa, wob, g_ffn, wg, wu, wd, final_norm, tm)
    return y.reshape(b, l, D_MODEL), new


def kernel(x_prompt, x_sample, cache_mla_latent, cache_mla_krope, cache_dsa_k, cache_dsa_v, cache_idx_k, attn_norm, w_in, q_a_norm, w_q_b, kv_a_norm, w_kv_b, w_o, ffn_norm, w_gate, w_up, w_down, final_norm):
    depth = w_in.shape[0]
    assert depth == 1, "the fused final norm assumes a single layer"
    past_len = cache_mla_latent.shape[2]
    li = 0
    weights = _prep_weights(w_in[li], w_q_b[li], w_kv_b[li], w_o[li], w_gate[li], w_up[li], w_down[li])
    norms = (attn_norm[li][None, :], q_a_norm[li][None, :], kv_a_norm[li][None, :], ffn_norm[li][None, :])
    fin = final_norm[None, :]
    past = (cache_mla_latent[li], cache_mla_krope[li], cache_dsa_k[li], cache_dsa_v[li], cache_idx_k[li])
    y_p, new_p = _layer(x_prompt, 0, None, weights, norms, fin)
    y_s, new_s = _layer(x_sample, past_len, past, weights, norms, fin)
    return (y_p, y_s) + tuple(a[None] for a in new_p) + tuple(a[None] for a in new_s)
```
